```python
import jax, jax.numpy as jnp
from jax import lax
import numpy as np

D_MODEL = 1024
BATCH = 16
SEQ = 2048
DEPTH = 4

RWKV_HEAD = 64
RWKV_WIDTH = 1024
RWKV_HEADS = RWKV_WIDTH // RWKV_HEAD
DECAY_LORA = 64
ICLR_LORA = 64
VRES_LORA = 32
RWKV_SHIFT_WIDTH = 3 * RWKV_WIDTH + 2 * DECAY_LORA + 2 * ICLR_LORA
RET_HEADS = 8
RET_QK_HEAD = 64
RET_V_HEAD = 128
RET_QK_WIDTH = RET_HEADS * RET_QK_HEAD
RET_V_WIDTH = RET_HEADS * RET_V_HEAD
RET_CHUNK = 128
ROPE_BASE = 10000.0
N_IN = RWKV_SHIFT_WIDTH + RWKV_WIDTH + 2 * RET_QK_WIDTH + 2 * RET_V_WIDTH + 2 * D_MODEL
NORM_EPS = 1e-6
GN_EPS = 64e-5

kernel_name = "hybrid_rwkv7_retention_gated_encoder"


def _cut_points(widths):
    pts, s = [], 0
    for w in widths[:-1]:
        s += w
        pts.append(s)
    return pts


def rmsnorm(x, g):
    xf = x.astype(jnp.float32)
    y = xf * lax.rsqrt(jnp.mean(xf * xf, axis=-1, keepdims=True) + NORM_EPS)
    return (y * g.astype(jnp.float32)).astype(x.dtype)


def head_norm(y, eps):
    mu = jnp.mean(y, axis=-1, keepdims=True)
    var = jnp.mean(jnp.square(y - mu), axis=-1, keepdims=True)
    yn = (y - mu) * lax.rsqrt(var + eps)
    return yn.reshape(y.shape[0], y.shape[1], -1)


def centred_shift(p, mu_prev, mu_next):
    prev = jnp.pad(p[:, :-1], ((0, 0), (1, 0), (0, 0)))
    nxt = jnp.pad(p[:, 1:], ((0, 0), (0, 1), (0, 0)))
    return p + mu_prev * (prev - p) + mu_next * (nxt - p)


def wkv7_scan(r, w, k, v, a, b, reverse):
    B, _, H, N = r.shape

    def step(S, inp):
        r_t, w_t, k_t, v_t, a_t, b_t = inp
        sa = jnp.einsum('bhvk,bhk->bhv', S, a_t)
        S = S * w_t[:, :, None, :] + sa[..., None] * b_t[:, :, None, :] + v_t[..., None] * k_t[:, :, None, :]
        y = jnp.einsum('bhvk,bhk->bhv', S, r_t)
        return S, y

    xs = tuple(jnp.moveaxis(t, 1, 0) for t in (r, w, k, v, a, b))
    S0 = jnp.zeros((B, H, N, N), jnp.float32)
    _, ys = lax.scan(step, S0, xs, reverse=reverse)
    return jnp.moveaxis(ys, 0, 1)


def rwkv7_branch(r, k, v, dec_lo, iclr_lo, w_decay_up, decay_bias, w_iclr_up, iclr_bias,
                 k_k, k_a, r_k, lnx_gain, lnx_bias):
    f32 = jnp.float32
    r, k, v = r.astype(f32), k.astype(f32), v.astype(f32)
    B, S, _ = r.shape
    hs = lambda t: t.reshape(B, S, RWKV_HEADS, RWKV_HEAD)
    kk = hs(k * k_k)
    kk = kk * lax.rsqrt(jnp.sum(kk * kk, axis=-1, keepdims=True) + 1e-12)
    rh, vh = hs(r), hs(v)
    y = jnp.zeros_like(rh)
    bonus = jnp.zeros_like(rh)
    for d in range(2):
        decay_logit = -jax.nn.softplus(-(decay_bias[d] + jnp.tanh(dec_lo[d].astype(f32)) @ w_decay_up[d])) - 0.5
        w = jnp.exp(-jnp.exp(decay_logit))
        a = jax.nn.sigmoid(iclr_bias[d] + iclr_lo[d].astype(f32) @ w_iclr_up[d])
        kd = k * (1.0 + (a - 1.0) * k_a)
        ah, kdh = hs(a), hs(kd)
        y = y + wkv7_scan(rh, hs(w), kdh, vh, -kk, kk * ah, reverse=(d == 1))
        bonus = bonus + jnp.sum(rh * kdh * r_k, axis=-1, keepdims=True) * vh
    return head_norm(y, GN_EPS) * lnx_gain + lnx_bias + bonus.reshape(B, S, -1)


def rotary(x, pos):
    half = x.shape[-1] // 2
    freqs = jnp.power(ROPE_BASE, -jnp.arange(half, dtype=jnp.float32) / half)
    ang = pos[:, None] * freqs[None, :]
    cos = jnp.cos(ang)[None, :, None, :]
    sin = jnp.sin(ang)[None, :, None, :]
    x1, x2 = x[..., :half], x[..., half:]
    return jnp.concatenate([x1 * cos - x2 * sin, x1 * sin + x2 * cos], axis=-1)


def retention_chunkwise(q, k, v, log_g, strict):
    f32 = jnp.float32
    B, H, S, dk = q.shape
    dv = v.shape[-1]
    C = RET_CHUNK
    NC = S // C
    qc = q.reshape(B, H, NC, C, dk)
    kc = k.reshape(B, H, NC, C, dk)
    vc = v.reshape(B, H, NC, C, dv)
    idx = jnp.arange(C)
    diff = idx[:, None] - idx[None, :]
    mask = (diff > 0) if strict else (diff >= 0)
    dmat = jnp.where(mask[None], jnp.exp(jnp.where(mask, diff, 0)[None].astype(f32) * log_g[:, None, None]), 0.0)
    scores = jnp.einsum('bhncd,bhnmd->bhncm', qc, kc) * dmat[None, :, None]
    intra = jnp.einsum('bhncm,bhnme->bhnce', scores, vc)
    posf = idx.astype(f32)
    xi = jnp.exp((posf + 1.0) * log_g[:, None])
    zeta = jnp.exp((C - 1.0 - posf) * log_g[:, None])
    kv = jnp.einsum('bhncd,hc,bhnce->bhnde', kc, zeta, vc)
    chunk_decay = jnp.exp(C * log_g)[None, :, None, None]

    def step(state, kv_j):
        return state * chunk_decay + kv_j, state

    _, s_prev = lax.scan(step, jnp.zeros((B, H, dk, dv), f32), jnp.moveaxis(kv, 2, 0))
    s_prev = jnp.moveaxis(s_prev, 0, 2)
    cross = jnp.einsum('bhncd,bhnde->bhnce', qc, s_prev) * xi[None, :, None, :, None]
    return (intra + cross).reshape(B, H, S, dv)


def retention_branch(q, k, v, gain):
    f32 = jnp.float32
    B, S, _ = q.shape
    q = q.astype(f32).reshape(B, S, RET_HEADS, RET_QK_HEAD)
    k = k.astype(f32).reshape(B, S, RET_HEADS, RET_QK_HEAD)
    v = v.astype(f32).reshape(B, S, RET_HEADS, RET_V_HEAD)
    pos = jnp.arange(S, dtype=f32)
    q = rotary(q, pos) * (RET_QK_HEAD ** -0.5)
    k = rotary(k, pos)
    q, k, v = (t.transpose(0, 2, 1, 3) for t in (q, k, v))
    log_g = jnp.log1p(-jnp.exp2(-5.0 - jnp.arange(RET_HEADS, dtype=f32)))
    fwd = retention_chunkwise(q, k, v, log_g, strict=False)
    bwd = jnp.flip(retention_chunkwise(jnp.flip(q, 2), jnp.flip(k, 2), jnp.flip(v, 2), log_g, strict=True), 2)
    y = (fwd + bwd).transpose(0, 2, 1, 3)
    return head_norm(y, NORM_EPS) * gain


def setup_inputs(seed: int = 0) -> dict:
    key = jax.random.key(seed)
    ks = jax.random.split(key, 32)
    n = jax.random.normal
    f32 = jnp.float32
    L, D, DA, DB = DEPTH, D_MODEL, RWKV_WIDTH, RET_V_WIDTH
    decay_base = jnp.linspace(-5.0, 1.0, DA, dtype=f32)
    return {
        "x": n(ks[0], (BATCH, SEQ, D), f32),
        "norm_gain": 1.0 + 0.02 * n(ks[1], (L, D), f32),
        "w_in": n(ks[2], (L, D, N_IN), f32) * D ** -0.5,
        "w_vres_down": n(ks[3], (L - 1, D, VRES_LORA), f32) * D ** -0.5,
        "shift_prev": jax.random.uniform(ks[4], (L, RWKV_SHIFT_WIDTH), f32, 0.1, 0.6),
        "shift_next": jax.random.uniform(ks[5], (L, RWKV_SHIFT_WIDTH), f32, 0.1, 0.6),
        "w_decay_up": n(ks[6], (L, 2, DECAY_LORA, DA), f32) * 0.1,
        "decay_bias": decay_base + 0.3 * n(ks[7], (L, 2, DA), f32),
        "w_iclr_up": n(ks[8], (L, 2, ICLR_LORA, DA), f32) * 0.1,
        "iclr_bias": 0.1 * n(ks[9], (L, 2, DA), f32),
        "w_vres_up": n(ks[10], (L - 1, VRES_LORA, DA), f32) * 0.1,
        "vres_bias": 0.1 * n(ks[11], (L - 1, DA), f32),
        "k_k": 0.85 + 0.02 * n(ks[12], (L, DA), f32),
        "k_a": 1.0 + 0.02 * n(ks[13], (L, DA), f32),
        "r_k": 0.1 * n(ks[14], (L, RWKV_HEADS, RWKV_HEAD), f32),
        "lnx_gain": 1.0 + 0.02 * n(ks[15], (L, DA), f32),
        "lnx_bias": 0.02 * n(ks[16], (L, DA), f32),
        "w_branch_a": n(ks[17], (L, DA, D), f32) * DA ** -0.5,
        "ret_norm_gain": 1.0 + 0.02 * n(ks[18], (L, DB), f32),
        "w_branch_b": n(ks[19], (L, DB, D), f32) * DB ** -0.5,
        "w_out": n(ks[20], (L, D, D), f32) * D ** -0.5,
        "final_gain": 1.0 + 0.02 * n(ks[21], (D,), f32),
    }


def reference(x, norm_gain, w_in, w_vres_down, shift_prev, shift_next, w_decay_up, decay_bias,
              w_iclr_up, iclr_bias, w_vres_up, vres_bias, k_k, k_a, r_k, lnx_gain, lnx_bias,
              w_branch_a, ret_norm_gain, w_branch_b, w_out, final_gain):
    shift_cuts = _cut_points([RWKV_WIDTH] * 3 + [DECAY_LORA] * 2 + [ICLR_LORA] * 2)
    rest_cuts = _cut_points([RWKV_WIDTH, RET_QK_WIDTH, RET_QK_WIDTH, RET_V_WIDTH, RET_V_WIDTH,
                             D_MODEL, D_MODEL, VRES_LORA])
    v_first = None
    for l in range(DEPTH):
        hn = rmsnorm(x, norm_gain[l])
        w_l = w_in[l] if l == 0 else jnp.concatenate([w_in[l], w_vres_down[l - 1]], axis=1)
        proj = hn @ w_l
        shifted = centred_shift(proj[..., :RWKV_SHIFT_WIDTH], shift_prev[l], shift_next[l])
        r, k, v, dec_f, dec_b, iclr_f, iclr_b = jnp.split(shifted, shift_cuts, axis=-1)
        gate_a, q_b, k_b, v_b, gate_b, mg_a, mg_b, vres_lo = jnp.split(proj[..., RWKV_SHIFT_WIDTH:], rest_cuts, axis=-1)
        if l == 0:
            v_first = v
        else:
            v = v + (v_first - v) * jax.nn.sigmoid(vres_bias[l - 1] + vres_lo @ w_vres_up[l - 1])
        y_a = rwkv7_branch(r, k, v, (dec_f, dec_b), (iclr_f, iclr_b), w_decay_up[l], decay_bias[l],
                           w_iclr_up[l], iclr_bias[l], k_k[l], k_a[l], r_k[l], lnx_gain[l], lnx_bias[l])
        y_a = y_a.astype(x.dtype) * jax.nn.silu(gate_a)
        y_b = retention_branch(q_b, k_b, v_b, ret_norm_gain[l]).astype(x.dtype) * jax.nn.silu(gate_b)
        merged = jax.nn.sigmoid(mg_a) * (y_a @ w_branch_a[l]) + jax.nn.sigmoid(mg_b) * (y_b @ w_branch_b[l])
        x = x + merged @ w_out[l]
    return rmsnorm(x, final_gain)
```

```python
import functools
import math

import numpy as np
import jax
import jax.numpy as jnp
from jax import lax
from jax.experimental import pallas as pl
from jax.experimental.pallas import tpu as pltpu

F32 = jnp.float32
BF16 = jnp.bfloat16

D = 1024
HEAD = 64
LANES = 128
CH = 64
RET_HEADS = 8
RET_C = 256
NORM_EPS = 1e-6
GN_EPS = 64e-5
ROPE_BASE = 10000.0
VMEM_LIMIT = 56 * 1024 * 1024

C_R, C_K, C_V, C_GA, C_VB, C_GB, C_MA, C_MB = (i * 1024 for i in range(8))
C_QB, C_KB, C_DEC, C_ICL, C_VRES, NPAD = 8192, 8704, 9216, 9344, 9472, 9600

P_LORA = 3
P_INV = 3
P_APPLY = 1
P_STATE = 3

_NN = (((1,), (0,)), ((), ()))
_NT = (((1,), (1,)), ((), ()))
_TN = (((0,), (0,)), ((), ()))


def _dg(a, b, dims):
    return lax.dot_general(a, b, dims, preferred_element_type=F32)


def _split2(x):
    hi = x.astype(BF16)
    lo = (x - hi.astype(F32)).astype(BF16)
    return hi, lo


def _mm(a, b, dims=_NN, prec=1):
    if prec == 1:
        return _dg(a.astype(BF16), b.astype(BF16), dims)
    ah, al = _split2(a)
    bh, bl = _split2(b)
    return _dg(ah, bh, dims) + (_dg(ah, bl, dims) + _dg(al, bh, dims))


def _mm_sel_l(sel, x):
    s = sel.astype(BF16)
    h1 = x.astype(BF16)
    r1 = x - h1.astype(F32)
    h2 = r1.astype(BF16)
    h3 = (r1 - h2.astype(F32)).astype(BF16)
    return _dg(s, h1, _NN) + (_dg(s, h2, _NN) + _dg(s, h3, _NN))


def _mm_sel_r(x, sel):
    s = sel.astype(BF16)
    h1, h2 = _split2(x)
    return _dg(h1, s, _NN) + _dg(h2, s, _NN)


def _iota(shape, axis):
    return lax.broadcasted_iota(jnp.int32, shape, axis)


def _pair_bd(x, bdmask):
    return jnp.where(bdmask, jnp.concatenate([x, x], axis=0), 0.0)


def _wkv_chunk(r, kd, v, a, b, logw, rev, cst):
    bdmask = cst["bd"]
    tri = cst["tri_b"] if rev else cst["tri_f"]
    strict = cst["strict_b"] if rev else cst["strict_f"]
    incl = cst["incl_b"] if rev else cst["incl_f"]
    m_d, m_o1, m_o2 = (cst["inv_b"] if rev else cst["inv_f"])
    eye = cst["eye_cat"]

    cum = _mm_sel_l(tri, logw)
    e_cum = jnp.exp(cum)
    e_inv = jnp.exp(-cum)
    rt = r * e_cum
    at = a * jnp.exp(cum - logw)
    kt = kd * e_inv
    bt = b * e_inv
    tot = cum[CH - 1:CH, :] if not rev else cum[0:1, :]
    w_tot = jnp.exp(tot)
    k_end = kt * w_tot
    b_end = bt * w_tot

    lhs = jnp.concatenate([at, rt], axis=0)
    rhs = jnp.concatenate([_pair_bd(bt, bdmask), _pair_bd(kt, bdmask)], axis=0)
    amat = _mm(lhs, rhs, _NT)
    n_cat = jnp.where(strict, amat[:CH, :LANES], 0.0)
    ak_cat = jnp.where(strict, amat[:CH, LANES:], 0.0)
    rb_cat = jnp.where(incl, amat[CH:, :LANES], 0.0)
    rk_cat = jnp.where(incl, amat[CH:, LANES:], 0.0)

    def compose(x, y):
        return _mm(x, _pair_bd(y, bdmask), prec=P_INV)

    p1 = jnp.where(m_d, n_cat, 0.0)
    p2 = compose(p1, p1)
    p4 = compose(p2, p2)
    p8 = compose(p4, p4)
    ta = eye + p1 + p2 + compose(p1, p2)
    tb = eye + p4 + p8 + compose(p4, p8)
    t16 = compose(ta, tb)
    t32 = t16 + compose(compose(t16, jnp.where(m_o1, n_cat, 0.0)), t16)
    tinv = t32 + compose(compose(t32, jnp.where(m_o2, n_cat, 0.0)), t32)

    def apply(x_cat, z, prec=P_APPLY):
        return _mm(x_cat, _pair_bd(z, bdmask), prec=prec)

    a_true = at
    ah = apply(tinv, a_true)
    u0 = apply(tinv, apply(ak_cat, v))
    rh = rt + apply(rb_cat, ah)
    y0 = _mm(jnp.concatenate([rb_cat, rk_cat], axis=1),
             jnp.concatenate([_pair_bd(u0, bdmask), _pair_bd(v, bdmask)], axis=0), prec=P_APPLY)
    m = jnp.where(bdmask, _mm(b_end, ah, _TN, prec=P_STATE), 0.0)
    m = m + jnp.where(cst["eye128"], w_tot, 0.0)
    g = jnp.where(bdmask, _mm(jnp.concatenate([b_end, k_end], axis=0),
                              jnp.concatenate([u0, v], axis=0), _TN, prec=P_STATE), 0.0)
    return rh, y0, m, g


def _wkv_consts():
    row = _iota((CH, LANES), 0)
    col = _iota((CH, LANES), 1) % CH
    r2 = _iota((CH, CH), 0)
    c2 = _iota((CH, CH), 1)
    r128 = _iota((LANES, LANES), 0)
    c128 = _iota((LANES, LANES), 1)

    def inv_masks(lower):
        late, early = (row, col) if lower else (col, row)
        m_d = (early < late) & ((row // 16) == (col // 16))
        m_o1 = ((row // 32) == (col // 32)) & ((late // 16) % 2 == 1) & ((early // 16) % 2 == 0)
        m_o2 = ((late // 32) == 1) & ((early // 32) == 0)
        return m_d, m_o1, m_o2

    return dict(
        bd=(r128 // CH) == (c128 // CH),
        eye128=r128 == c128,
        eye_cat=jnp.where(row == col, 1.0, 0.0).astype(F32),
        tri_f=jnp.where(c2 <= r2, 1.0, 0.0).astype(F32),
        tri_b=jnp.where(c2 >= r2, 1.0, 0.0).astype(F32),
        strict_f=col < row, strict_b=col > row,
        incl_f=col <= row, incl_b=col >= row,
        inv_f=inv_masks(True), inv_b=inv_masks(False),
    )


def _shifted(ref, start, mp, mn, seq):
    cur = ref[pl.ds(start, CH), :]
    pstart = pl.multiple_of(jnp.maximum(start - 8, 0), 8)
    nstart = pl.multiple_of(jnp.minimum(start + CH, seq - 8), 8)
    prow = jnp.where(start > 0, ref[pl.ds(pstart, 8), :][7:8, :], 0.0)
    nrow = jnp.where(start + CH < seq, ref[pl.ds(nstart, 8), :][0:1, :], 0.0)
    row = _iota((CH, LANES), 0)
    prev = jnp.where(row == 0, prow, pltpu.roll(cur, 1, 0))
    nxt = jnp.where(row == CH - 1, nrow, pltpu.roll(cur, CH - 1, 0))
    return cur + mp * (prev - cur) + mn * (nxt - cur)


(V_MPR, V_MPK, V_MPV, V_MNR, V_MNK, V_MNV, V_DBF, V_DBB, V_IBF, V_IBB,
 V_KK, V_KA, V_RK, V_LG, V_LB, V_VB) = range(16)
W_DF, W_DB, W_IF, W_IB, W_VR = range(5)


def _wkv_kernel(*refs, seq, first):
    if first:
        (r_ref, k_ref, v_ref, ga_ref, dec_ref, icl_ref, vec_ref, mul_ref, wl_ref,
         ya_ref, vf_out_ref, rh_s, y0_s, m_s, g_s, yf_s, yb_s, bon_s) = refs
        vres_ref = vf_ref = None
    else:
        (r_ref, k_ref, v_ref, ga_ref, dec_ref, icl_ref, vres_ref, vf_ref, vec_ref, mul_ref, wl_ref,
         ya_ref, rh_s, y0_s, m_s, g_s, yf_s, yb_s, bon_s) = refs
        vf_out_ref = None
    nc = seq // CH
    cst = _wkv_consts()
    bdmask = cst["bd"]
    ones_bd = jnp.where(bdmask, 1.0, 0.0).astype(F32)

    def vec(i):
        return vec_ref[i:i + 1, :]

    def prep(c, _):
        start = pl.multiple_of(c * CH, CH)
        r = _shifted(r_ref, start, vec(V_MPR), vec(V_MNR), seq)
        k = _shifted(k_ref, start, vec(V_MPK), vec(V_MNK), seq)
        v = _shifted(v_ref, start, vec(V_MPV), vec(V_MNV), seq)
        dec = _shifted(dec_ref, start, mul_ref[0:1, :], mul_ref[2:3, :], seq)
        icl = _shifted(icl_ref, start, mul_ref[1:2, :], mul_ref[3:4, :], seq)
        if first:
            vf_out_ref[pl.ds(start, CH), :] = v
        else:
            mix = jax.nn.sigmoid(vec(V_VB) + _mm(vres_ref[pl.ds(start, CH), :], wl_ref[W_VR], prec=P_LORA))
            v = v + (vf_ref[pl.ds(start, CH), :] - v) * mix
        kk = k * vec(V_KK)
        kk = kk * lax.rsqrt(_mm_sel_r(kk * kk, ones_bd) + 1e-12)
        th = jnp.tanh(dec)
        bon = jnp.zeros((CH, LANES), F32)
        for d in range(2):
            z = vec(V_DBF + d) + _mm(th, wl_ref[W_DF + d], prec=P_LORA)
            dl = -(jnp.maximum(-z, 0.0) + jnp.log1p(jnp.exp(-jnp.abs(z)))) - 0.5
            logw = -jnp.exp(dl)
            ag = jax.nn.sigmoid(vec(V_IBF + d) + _mm(icl, wl_ref[W_IF + d], prec=P_LORA))
            kd = k * (1.0 + (ag - 1.0) * vec(V_KA))
            bon = bon + r * kd * vec(V_RK)
            rh, y0, m, g = _wkv_chunk(r, kd, v, -kk, kk * ag, logw, d == 1, cst)
            rh_s[d, c] = rh
            y0_s[d, c] = y0
            m_s[d, c] = m
            g_s[d, c] = g
        bon_s[pl.ds(start, CH), :] = _mm_sel_r(bon, ones_bd) * v
        return 0

    lax.fori_loop(0, nc, prep, 0)

    def scan(c, carry):
        sf, sb = carry
        cb = nc - 1 - c
        yf_s[pl.ds(pl.multiple_of(c * CH, CH), CH), :] = _mm(rh_s[0, c], sf, prec=P_STATE) + y0_s[0, c]
        yb_s[pl.ds(pl.multiple_of(cb * CH, CH), CH), :] = _mm(rh_s[1, cb], sb, prec=P_STATE) + y0_s[1, cb]
        sf = _mm(m_s[0, c], sf, prec=P_STATE) + g_s[0, c]
        sb = _mm(m_s[1, cb], sb, prec=P_STATE) + g_s[1, cb]
        return sf, sb

    zero = jnp.zeros((LANES, LANES), F32)
    lax.fori_loop(0, nc, scan, (zero, zero))

    def post(c, _):
        sl = pl.ds(pl.multiple_of(c * CH, CH), CH)
        y = yf_s[sl, :] + yb_s[sl, :]
        mu = _mm_sel_r(y, ones_bd) * (1.0 / HEAD)
        dv = y - mu
        var = _mm_sel_r(dv * dv, ones_bd) * (1.0 / HEAD)
        out = dv * lax.rsqrt(var + GN_EPS) * vec(V_LG) + vec(V_LB) + bon_s[sl, :]
        ya_ref[sl, :] = out * jax.nn.silu(ga_ref[sl, :])
        return 0

    lax.fori_loop(0, nc, post, 0)


def _wkv_call(proj3, vfirst, vecs, mul, wl, *, first):
    bsz, seq, _ = proj3.shape
    nc = seq // CH

    def col(base):
        blk = base // LANES
        return pl.BlockSpec((None, seq, LANES), lambda b, p, blk=blk: (b, 0, blk + p))

    def fixed(base):
        blk = base // LANES
        return pl.BlockSpec((None, seq, LANES), lambda b, p, blk=blk: (b, 0, blk))

    in_specs = [col(C_R), col(C_K), col(C_V), col(C_GA), fixed(C_DEC), fixed(C_ICL)]
    args = [proj3] * 6
    if not first:
        in_specs += [fixed(C_VRES), pl.BlockSpec((None, seq, LANES), lambda b, p: (b, 0, p))]
        args += [proj3, vfirst]
    in_specs += [pl.BlockSpec((16, LANES), lambda b, p: (0, p)),
                 pl.BlockSpec((8, LANES), lambda b, p: (0, 0)),
                 pl.BlockSpec((5, LANES, LANES), lambda b, p: (0, 0, p))]
    args += [vecs, mul, wl]
    out_block = pl.BlockSpec((None, seq, LANES), lambda b, p: (b, 0, p))
    out_sds = jax.ShapeDtypeStruct((bsz, seq, D), F32)
    out_shape = (out_sds, out_sds) if first else out_sds
    out_specs = (out_block, out_block) if first else out_block
    scratch = [pltpu.VMEM((2, nc, CH, LANES), F32), pltpu.VMEM((2, nc, CH, LANES), F32),
               pltpu.VMEM((2, nc, LANES, LANES), F32), pltpu.VMEM((2, nc, LANES, LANES), F32),
               pltpu.VMEM((seq, LANES), F32), pltpu.VMEM((seq, LANES), F32), pltpu.VMEM((seq, LANES), F32)]
    return pl.pallas_call(
        functools.partial(_wkv_kernel, seq=seq, first=first),
        grid=(bsz, D // LANES),
        in_specs=in_specs, out_specs=out_specs, out_shape=out_shape,
        scratch_shapes=scratch,
        compiler_params=pltpu.CompilerParams(dimension_semantics=("arbitrary", "arbitrary"),
                                             vmem_limit_bytes=VMEM_LIMIT),
        name="wkv7",
    )(*args)


def _proj_kernel(x_ref, g_ref, w_ref, o_ref, xn_ref):
    @pl.when(pl.program_id(1) == 0)
    def _():
        x = x_ref[...]
        ms = jnp.mean(x * x, axis=-1, keepdims=True)
        xn_ref[...] = (x * lax.rsqrt(ms + NORM_EPS) * g_ref[...]).astype(BF16)

    o_ref[...] = jnp.dot(xn_ref[...], w_ref[...], preferred_element_type=F32)


def _proj_call(x2, gain, w):
    t = x2.shape[0]
    tm = min(1024, t)
    tn = 1920
    return pl.pallas_call(
        _proj_kernel,
        grid=(t // tm, NPAD // tn),
        in_specs=[pl.BlockSpec((tm, D), lambda i, j: (i, 0)),
                  pl.BlockSpec((1, D), lambda i, j: (0, 0)),
                  pl.BlockSpec((D, tn), lambda i, j: (0, j))],
        out_specs=pl.BlockSpec((tm, tn), lambda i, j: (i, j)),
        out_shape=jax.ShapeDtypeStruct((t, NPAD), F32),
        scratch_shapes=[pltpu.VMEM((tm, D), BF16)],
        compiler_params=pltpu.CompilerParams(dimension_semantics=("arbitrary", "arbitrary"),
                                             vmem_limit_bytes=VMEM_LIMIT),
        name="proj",
    )(x2, gain, w)


def _ret_kernel(q_ref, k_ref, v_ref, gb_ref, cos_ref, sin_ref, lg_ref, gain_ref, o_ref, qr_s, kr_s, y_s, *, seq):
    nc = seq // RET_C
    p = pl.program_id(1)
    scale = HEAD ** -0.5

    def rot(c, _):
        sl = pl.ds(pl.multiple_of(c * RET_C, RET_C), RET_C)
        cs, sn = cos_ref[sl, :], sin_ref[sl, :]
        q = q_ref[sl, :]
        k = k_ref[sl, :]
        qr_s[sl, :] = (q * cs + pltpu.roll(q, 64, 1) * sn) * scale
        kr_s[sl, :] = k * cs + pltpu.roll(k, 64, 1) * sn
        return 0

    lax.fori_loop(0, nc, rot, 0)

    lane = _iota((RET_C, LANES), 1)
    rowi = _iota((RET_C, LANES), 0).astype(F32)
    absd = jnp.abs(_iota((RET_C, RET_C), 0) - _iota((RET_C, RET_C), 1)).astype(F32)
    for j in range(2):
        hmask = ((lane // 32) % 2) == j
        lg = lg_ref[pl.ds(2 * p + j, 1), :]
        lg128 = lg[:, :LANES]
        dmat = jnp.exp(absd * lg)
        xi_f = jnp.exp((rowi + 1.0) * lg128)
        ze_f = jnp.exp((RET_C - 1.0 - rowi) * lg128)
        xi_b = jnp.exp((RET_C - rowi) * lg128)
        ze_b = jnp.exp(rowi * lg128)
        cdec = jnp.exp(RET_C * lg128)
        vs = slice(j * LANES, (j + 1) * LANES)
        fst = jnp.zeros((LANES, LANES), F32)
        for i in range(nc):
            sl = slice(i * RET_C, (i + 1) * RET_C)
            qh = jnp.where(hmask, qr_s[sl, :], 0.0)
            kh = kr_s[sl, :]
            vh = v_ref[sl, vs]
            sc = _mm(qh, kh, _NT) * dmat
            y = _mm(sc, vh)
            if i > 0:
                y = y + _mm(qh * xi_f, fst)
            if i < nc - 1:
                fst = fst * cdec + _mm(jnp.where(hmask, kh, 0.0) * ze_f, vh, _TN)
            y_s[sl, vs] = y
        gst = jnp.zeros((LANES, LANES), F32)
        for i in range(nc - 1, -1, -1):
            sl = slice(i * RET_C, (i + 1) * RET_C)
            y = y_s[sl, vs]
            if i < nc - 1:
                qh = jnp.where(hmask, qr_s[sl, :], 0.0)
                y = y + _mm(qh * xi_b, gst)
            if i > 0:
                kh = jnp.where(hmask, kr_s[sl, :], 0.0)
                gst = gst * cdec + _mm(kh * ze_b, v_ref[sl, vs], _TN)
            mu = jnp.mean(y, axis=-1, keepdims=True)
            dv = y - mu
            var = jnp.mean(dv * dv, axis=-1, keepdims=True)
            out = dv * lax.rsqrt(var + NORM_EPS) * gain_ref[:, vs]
            o_ref[sl, vs] = out * jax.nn.silu(gb_ref[sl, vs])


def _ret_call(proj3, cos_t, sin_t, lg_t, gain):
    bsz, seq, _ = proj3.shape
    w2 = 2 * LANES

    def spec(base, width):
        blk = base // width
        return pl.BlockSpec((None, seq, width), lambda b, p, blk=blk: (b, 0, blk + p))

    return pl.pallas_call(
        functools.partial(_ret_kernel, seq=seq),
        grid=(bsz, RET_HEADS // 2),
        in_specs=[spec(C_QB, LANES), spec(C_KB, LANES), spec(C_VB, w2), spec(C_GB, w2),
                  pl.BlockSpec((seq, LANES), lambda b, p: (0, 0)),
                  pl.BlockSpec((seq, LANES), lambda b, p: (0, 0)),
                  pl.BlockSpec((RET_HEADS, w2), lambda b, p: (0, 0)),
                  pl.BlockSpec((1, w2), lambda b, p: (0, p))],
        out_specs=pl.BlockSpec((None, seq, w2), lambda b, p: (b, 0, p)),
        out_shape=jax.ShapeDtypeStruct((bsz, seq, D), F32),
        scratch_shapes=[pltpu.VMEM((seq, LANES), F32), pltpu.VMEM((seq, LANES), F32), pltpu.VMEM((seq, w2), F32)],
        compiler_params=pltpu.CompilerParams(dimension_semantics=("arbitrary", "arbitrary"),
                                             vmem_limit_bytes=VMEM_LIMIT),
        name="retention",
    )(proj3, proj3, proj3, proj3, cos_t, sin_t, lg_t, gain)


def _merge_kernel(x_ref, ya_ref, yb_ref, ma_ref, mb_ref, wa_ref, wb_ref, wo_ref, o_ref):
    za = jnp.dot(ya_ref[...].astype(BF16), wa_ref[...], preferred_element_type=F32)
    zb = jnp.dot(yb_ref[...].astype(BF16), wb_ref[...], preferred_element_type=F32)
    merged = jax.nn.sigmoid(ma_ref[...]) * za + jax.nn.sigmoid(mb_ref[...]) * zb
    o_ref[...] = x_ref[...] + jnp.dot(merged.astype(BF16), wo_ref[...], preferred_element_type=F32)


def _merge_call(x2, ya2, yb2, proj2, wa, wb, wo):
    t = x2.shape[0]
    tm = min(512, t)
    row = pl.BlockSpec((tm, D), lambda i: (i, 0))
    wspec = pl.BlockSpec((D, D), lambda i: (0, 0))
    return pl.pallas_call(
        _merge_kernel,
        grid=(t // tm,),
        in_specs=[row, row, row,
                  pl.BlockSpec((tm, D), lambda i: (i, C_MA // D)),
                  pl.BlockSpec((tm, D), lambda i: (i, C_MB // D)),
                  wspec, wspec, wspec],
        out_specs=row,
        out_shape=jax.ShapeDtypeStruct((t, D), F32),
        compiler_params=pltpu.CompilerParams(dimension_semantics=("arbitrary",),
                                             vmem_limit_bytes=VMEM_LIMIT),
        name="merge",
    )(x2, ya2, yb2, proj2, proj2, wa, wb, wo)


def _final_kernel(x_ref, g_ref, o_ref):
    x = x_ref[...]
    ms = jnp.mean(x * x, axis=-1, keepdims=True)
    o_ref[...] = x * lax.rsqrt(ms + NORM_EPS) * g_ref[...]


def _final_call(x2, gain):
    t = x2.shape[0]
    tm = min(1024, t)
    return pl.pallas_call(
        _final_kernel,
        grid=(t // tm,),
        in_specs=[pl.BlockSpec((tm, D), lambda i: (i, 0)), pl.BlockSpec((1, D), lambda i: (0, 0))],
        out_specs=pl.BlockSpec((tm, D), lambda i: (i, 0)),
        out_shape=jax.ShapeDtypeStruct((t, D), F32),
        compiler_params=pltpu.CompilerParams(dimension_semantics=("arbitrary",)),
        name="final_norm",
    )(x2, gain)


def _qk_perm():
    idx = []
    for pair in range(RET_HEADS // 2):
        for lane in range(LANES):
            seg, i = lane // 32, lane % 32
            head, half = 2 * pair + seg % 2, seg // 2
            idx.append(head * HEAD + half * 32 + i)
    return np.asarray(idx, np.int32)


def _pack_w_in(w_in, w_vres_down):
    depth = w_in.shape[0]
    perm = _qk_perm()
    vres = jnp.concatenate([jnp.zeros((1, D, 32), F32), w_vres_down], axis=0)
    vres = jnp.pad(vres, ((0, 0), (0, 0), (0, LANES - 32)))
    parts = [w_in[:, :, 0:3072], w_in[:, :, 3328:4352], w_in[:, :, 5376:6400], w_in[:, :, 6400:7424],
             w_in[:, :, 7424:8448], w_in[:, :, 8448:9472],
             w_in[:, :, 4352:4864][:, :, perm], w_in[:, :, 4864:5376][:, :, perm],
             w_in[:, :, 3072:3328], vres]
    w = jnp.concatenate(parts, axis=2)
    assert w.shape == (depth, D, NPAD)
    return w.astype(BF16)


def _rope_tables(seq):
    half = 32
    freqs = jnp.power(ROPE_BASE, -jnp.arange(half, dtype=F32) / half)
    ang = jnp.arange(seq, dtype=F32)[:, None] * freqs[None, :]
    cos = jnp.tile(jnp.cos(ang), (1, 4))
    sin = jnp.tile(jnp.sin(ang), (1, 4))
    sign = jnp.where(jnp.arange(LANES) < 64, -1.0, 1.0).astype(F32)
    return cos, sin * sign[None, :]


def kernel(x, norm_gain, w_in, w_vres_down, shift_prev, shift_next, w_decay_up, decay_bias, w_iclr_up, iclr_bias,
           w_vres_up, vres_bias, k_k, k_a, r_k, lnx_gain, lnx_bias, w_branch_a, ret_norm_gain, w_branch_b, w_out,
           final_gain):
    bsz, seq, _ = x.shape
    depth = w_in.shape[0]
    assert seq % RET_C == 0 and seq % CH == 0
    t = bsz * seq

    w_all = _pack_w_in(w_in, w_vres_down)
    wa_all, wb_all, wo_all = (w.astype(BF16) for w in (w_branch_a, w_branch_b, w_out))
    cos_t, sin_t = _rope_tables(seq)
    log_g = jnp.log1p(-jnp.exp2(-5.0 - jnp.arange(RET_HEADS, dtype=F32)))
    lg_t = jnp.broadcast_to(log_g[:, None], (RET_HEADS, 2 * LANES))
    zpad = jnp.zeros((64, D), F32)

    x2 = x.reshape(t, D)
    vfirst = None
    for l in range(depth):
        proj2 = _proj_call(x2, norm_gain[l][None, :], w_all[l])
        proj3 = proj2.reshape(bsz, seq, NPAD)

        vb = vres_bias[l - 1] if l > 0 else jnp.zeros((D,), F32)
        vecs = jnp.stack([shift_prev[l, 0:1024], shift_prev[l, 1024:2048], shift_prev[l, 2048:3072],
                          shift_next[l, 0:1024], shift_next[l, 1024:2048], shift_next[l, 2048:3072],
                          decay_bias[l, 0], decay_bias[l, 1], iclr_bias[l, 0], iclr_bias[l, 1],
                          k_k[l], k_a[l], r_k[l].reshape(D), lnx_gain[l], lnx_bias[l], vb], axis=0)
        mul = jnp.concatenate([jnp.stack([shift_prev[l, 3072:3200], shift_prev[l, 3200:3328],
                                          shift_next[l, 3072:3200], shift_next[l, 3200:3328]], axis=0),
                               jnp.zeros((4, LANES), F32)], axis=0)
        wvr = jnp.pad(w_vres_up[l - 1], ((0, LANES - 32), (0, 0))) if l > 0 else jnp.zeros((LANES, D), F32)
        wl = jnp.stack([jnp.concatenate([w_decay_up[l, 0], zpad], axis=0),
                        jnp.concatenate([zpad, w_decay_up[l, 1]], axis=0),
                        jnp.concatenate([w_iclr_up[l, 0], zpad], axis=0),
                        jnp.concatenate([zpad, w_iclr_up[l, 1]], axis=0), wvr], axis=0)

        if l == 0:
            ya, vfirst = _wkv_call(proj3, None, vecs, mul, wl, first=True)
        else:
            ya = _wkv_call(proj3, vfirst, vecs, mul, wl, first=False)
        yb = _ret_call(proj3, cos_t, sin_t, lg_t, ret_norm_gain[l][None, :])
        x2 = _merge_call(x2, ya.reshape(t, D), yb.reshape(t, D), proj2, wa_all[l], wb_all[l], wo_all[l])

    return _final_call(x2, final_gain[None, :]).reshape(bsz, seq, D)
```

```python
import functools

import numpy as np
import jax
import jax.numpy as jnp
from jax import lax
from jax.experimental import pallas as pl
from jax.experimental.pallas import tpu as pltpu

F32 = jnp.float32
BF16 = jnp.bfloat16

D = 1024
HEAD = 64
LANES = 128
CH = 64
SLAB = 256
RET_HEADS = 8
RET_C = 256
NORM_EPS = 1e-6
GN_EPS = 64e-5
ROPE_BASE = 10000.0
VMEM_LIMIT = 56 * 1024 * 1024

C_R, C_K, C_V, C_GA, C_VB, C_GB, C_MA, C_MB = (i * 1024 for i in range(8))
C_QB, C_KB, C_DEC, C_ICL, C_VRES, NPAD = 8192, 8704, 9216, 9344, 9472, 9600

P_LORA = 1
P_WKV = 1

_NN = (((1,), (0,)), ((), ()))
_NT = (((1,), (1,)), ((), ()))
_TN = (((0,), (0,)), ((), ()))


def _dg(a, b, dims):
    return lax.dot_general(a, b, dims, preferred_element_type=F32)


def _split2(x):
    hi = x.astype(BF16)
    lo = (x - hi.astype(F32)).astype(BF16)
    return hi, lo


def _mm(a, b, dims=_NN, prec=1):
    if prec == 1:
        return _dg(a.astype(BF16), b.astype(BF16), dims)
    ah, al = _split2(a)
    bh, bl = _split2(b)
    return _dg(ah, bh, dims) + (_dg(ah, bl, dims) + _dg(al, bh, dims))


def _headsum(xs, ones_bd):
    out = []
    for x in xs:
        hi, lo = _split2(x)
        sel = ones_bd.astype(BF16)
        out.append(_dg(hi, sel, _NN) + _dg(lo, sel, _NN))
    return out


def _iota(shape, axis):
    return lax.broadcasted_iota(jnp.int32, shape, axis)


def _pair_bd(x, bdmask):
    return jnp.where(bdmask, jnp.concatenate([x, x], axis=0), 0.0)


def _wkv_consts():
    row = _iota((CH, LANES), 0)
    col = _iota((CH, LANES), 1) % CH
    r128 = _iota((LANES, LANES), 0)
    c128 = _iota((LANES, LANES), 1)

    def masks(lower):
        late, early = (row, col) if lower else (col, row)
        return dict(
            strict=early < late, incl=early <= late,
            m_d=(early < late) & ((row // 16) == (col // 16)),
            m_o1=((row // 32) == (col // 32)) & ((late // 16) % 2 == 1) & ((early // 16) % 2 == 0),
            m_o2=((late // 32) == 1) & ((early // 32) == 0))

    return dict(bd=(r128 // CH) == (c128 // CH), eye128=r128 == c128,
                eye_cat=jnp.where(row == col, 1.0, 0.0).astype(F32),
                fwd=masks(True), bwd=masks(False))


def _wkv_chunks(items, cst):
    bdmask = cst["bd"]
    zero_bd = jnp.zeros((LANES, LANES), F32)

    def bd(x):
        return _pair_bd(x, bdmask)

    def mm(a, b, dims=_NN):
        return _mm(a, b, dims, prec=P_WKV)

    st = []
    for it in items:
        mk = cst["bwd"] if it["rev"] else cst["fwd"]
        cum, logw = it["cum"], it["logw"]
        e_inv = jnp.exp(-cum)
        rt = it["r"] * jnp.exp(cum)
        at = it["a"] * jnp.exp(cum - logw)
        kt = it["kd"] * e_inv
        bt = it["b"] * e_inv
        w_tot = jnp.exp(cum[0:1, :] if it["rev"] else cum[CH - 1:CH, :])
        st.append(dict(mk=mk, rt=rt, at=at, kt=kt, bt=bt, w_tot=w_tot, v=it["v"],
                       kb_end=jnp.concatenate([bt * w_tot, kt * w_tot], axis=0)))

    for s in st:
        amat = mm(jnp.concatenate([s["at"], s["rt"]], axis=0),
                  jnp.concatenate([bd(s["bt"]), bd(s["kt"])], axis=0), _NT)
        mk = s["mk"]
        s["n"] = jnp.where(mk["strict"], amat[:CH, :LANES], 0.0)
        s["ak"] = jnp.where(mk["strict"], amat[:CH, LANES:], 0.0)
        s["rb"] = jnp.where(mk["incl"], amat[CH:, :LANES], 0.0)
        s["rk"] = jnp.where(mk["incl"], amat[CH:, LANES:], 0.0)
    for s in st:
        s["x1"] = mm(s["ak"], bd(s["v"]))
    for s in st:
        s["p"] = jnp.where(s["mk"]["m_d"], s["n"], 0.0)
        s["t"] = cst["eye_cat"] + s["p"]
    for s in st:
        s["q"] = mm(s["p"], bd(s["p"]))
    for _ in range(2):
        for s in st:
            qt = mm(jnp.concatenate([s["q"], s["t"]], axis=0), bd(s["q"]))
            s["t"] = s["t"] + qt[CH:]
            s["q"] = qt[:CH]
    for s in st:
        s["t"] = s["t"] + mm(s["t"], bd(s["q"]))
    for key in ("m_o1", "m_o2"):
        for s in st:
            s["x"] = mm(s["t"], bd(jnp.where(s["mk"][key], s["n"], 0.0)))
        for s in st:
            s["t"] = s["t"] + mm(s["x"], bd(s["t"]))
    for s in st:
        au = mm(s["t"], jnp.concatenate([bd(s["at"]), bd(s["x1"])], axis=1))
        s["ah"], s["u0"] = au[:, :LANES], au[:, LANES:]
    out = []
    for s in st:
        ry = mm(s["rb"], jnp.concatenate([bd(s["ah"]), bd(s["u0"])], axis=1))
        y0 = ry[:, LANES:] + mm(s["rk"], bd(s["v"]))
        rhs2 = jnp.concatenate([jnp.concatenate([s["ah"], s["u0"]], axis=1),
                                jnp.concatenate([jnp.zeros((CH, LANES), F32), s["v"]], axis=1)], axis=0)
        mg = mm(s["kb_end"], rhs2, _TN)
        m = jnp.where(bdmask, mg[:, :LANES], 0.0) + jnp.where(cst["eye128"], s["w_tot"], 0.0)
        g = jnp.where(bdmask, mg[:, LANES:], 0.0)
        out.append((s["rt"] + ry[:, :LANES], y0, m, g))
    return out


def _shifted(ref, start, n, mp, mn, seq):
    cur = ref[pl.ds(start, n), :]
    pstart = pl.multiple_of(jnp.maximum(start - 8, 0), 8)
    nstart = pl.multiple_of(jnp.minimum(start + n, seq - 8), 8)
    prow = jnp.where(start > 0, ref[pl.ds(pstart, 8), :][7:8, :], 0.0)
    nrow = jnp.where(start + n < seq, ref[pl.ds(nstart, 8), :][0:1, :], 0.0)
    row = _iota((n, LANES), 0)
    prev = jnp.where(row == 0, prow, pltpu.roll(cur, 1, 0))
    nxt = jnp.where(row == n - 1, nrow, pltpu.roll(cur, n - 1, 0))
    return cur + mp * (prev - cur) + mn * (nxt - cur)


def _chunk_cumsum(x, rev):
    n = x.shape[0]
    pos = _iota((n, LANES), 0) % CH
    s = 1
    while s < CH:
        if rev:
            x = x + jnp.where(pos < CH - s, pltpu.roll(x, n - s, 0), 0.0)
        else:
            x = x + jnp.where(pos >= s, pltpu.roll(x, s, 0), 0.0)
        s *= 2
    return x


(V_MPR, V_MPK, V_MPV, V_MNR, V_MNK, V_MNV, V_DBF, V_DBB, V_IBF, V_IBB,
 V_KK, V_KA, V_RK, V_LG, V_LB, V_VB) = range(16)
W_DEC, W_ICL, W_VR = range(3)


def _wkv_kernel(*refs, seq, first):
    if first:
        (r_ref, k_ref, v_ref, ga_ref, dec_ref, icl_ref, vec_ref, mul_ref, wl_ref,
         ya_ref, vf_out_ref, lhs_s, add_s, yf_s, yb_s, bon_s, st_s) = refs
        vres_ref = vf_ref = None
    else:
        (r_ref, k_ref, v_ref, ga_ref, dec_ref, icl_ref, vres_ref, vf_ref, vec_ref, mul_ref, wl_ref,
         ya_ref, lhs_s, add_s, yf_s, yb_s, bon_s, st_s) = refs
        vf_out_ref = None
    nc = seq // CH
    per = SLAB // CH
    cst = _wkv_consts()
    ones_bd = jnp.where(cst["bd"], 1.0, 0.0).astype(F32)

    def vec(i):
        return vec_ref[i:i + 1, :]

    def prep(sb, _):
        start = pl.multiple_of(sb * SLAB, SLAB)
        rows = pl.ds(start, SLAB)
        r = _shifted(r_ref, start, SLAB, vec(V_MPR), vec(V_MNR), seq)
        k = _shifted(k_ref, start, SLAB, vec(V_MPK), vec(V_MNK), seq)
        v = _shifted(v_ref, start, SLAB, vec(V_MPV), vec(V_MNV), seq)
        dec = _shifted(dec_ref, start, SLAB, mul_ref[0:1, :], mul_ref[2:3, :], seq)
        icl = _shifted(icl_ref, start, SLAB, mul_ref[1:2, :], mul_ref[3:4, :], seq)
        if first:
            vf_out_ref[rows, :] = v
        else:
            mix = jax.nn.sigmoid(vec(V_VB) + _mm(vres_ref[rows, :], wl_ref[W_VR], prec=P_LORA)[:, :LANES])
            v = v + (vf_ref[rows, :] - v) * mix
        zd = _mm(jnp.tanh(dec), wl_ref[W_DEC], prec=P_LORA)
        zi = _mm(icl, wl_ref[W_ICL], prec=P_LORA)
        kk = k * vec(V_KK)
        kds, ags, lws = [], [], []
        for d in range(2):
            z = vec(V_DBF + d) + zd[:, d * LANES:(d + 1) * LANES]
            dl = -(jnp.maximum(-z, 0.0) + jnp.log1p(jnp.exp(-jnp.abs(z)))) - 0.5
            lws.append(-jnp.exp(dl))
            ag = jax.nn.sigmoid(vec(V_IBF + d) + zi[:, d * LANES:(d + 1) * LANES])
            kds.append(k * (1.0 + (ag - 1.0) * vec(V_KA)))
            ags.append(ag)
        ssq, bsum = _headsum([kk * kk, r * (kds[0] + kds[1]) * vec(V_RK)], ones_bd)
        kk = kk * lax.rsqrt(ssq + 1e-12)
        bon_s[rows, :] = bsum * v
        items = []
        for d in range(2):
            cum = _chunk_cumsum(lws[d], d == 1)
            b = kk * ags[d]
            for i in range(per):
                sl = slice(i * CH, (i + 1) * CH)
                items.append(dict(r=r[sl], kd=kds[d][sl], v=v[sl], a=-kk[sl], b=b[sl], logw=lws[d][sl],
                                  cum=cum[sl], rev=d == 1))
        res = _wkv_chunks(items, cst)
        for d in range(2):
            for i in range(per):
                rh, y0, m, g = res[d * per + i]
                c = sb * per + i
                lhs_s[d, c, :CH, :] = rh
                lhs_s[d, c, CH:, :] = m
                add_s[d, c, :CH, :] = y0
                add_s[d, c, CH:, :] = g
        return 0

    lax.fori_loop(0, seq // SLAB, prep, 0)

    st_s[...] = jnp.zeros((2, LANES, LANES), F32)

    def scan(c, _):
        cb = nc - 1 - c
        of = _mm(lhs_s[0, c], st_s[0], prec=P_WKV) + add_s[0, c]
        ob = _mm(lhs_s[1, cb], st_s[1], prec=P_WKV) + add_s[1, cb]
        yf_s[pl.ds(pl.multiple_of(c * CH, CH), CH), :] = of[:CH]
        yb_s[pl.ds(pl.multiple_of(cb * CH, CH), CH), :] = ob[:CH]
        st_s[0] = of[CH:]
        st_s[1] = ob[CH:]
        return 0

    lax.fori_loop(0, nc, scan, 0)

    def post(sb, _):
        rows = pl.ds(pl.multiple_of(sb * SLAB, SLAB), SLAB)
        y = yf_s[rows, :] + yb_s[rows, :]
        mu = _headsum([y], ones_bd)[0] * (1.0 / HEAD)
        dv = y - mu
        var = _headsum([dv * dv], ones_bd)[0] * (1.0 / HEAD)
        out = dv * lax.rsqrt(var + GN_EPS) * vec(V_LG) + vec(V_LB) + bon_s[rows, :]
        ya_ref[rows, :] = out * jax.nn.silu(ga_ref[rows, :])
        return 0

    lax.fori_loop(0, seq // SLAB, post, 0)


def _wkv_call(proj3, vfirst, vecs, mul, wl, *, first):
    bsz, seq, _ = proj3.shape
    nc = seq // CH

    def col(base):
        blk = base // LANES
        return pl.BlockSpec((None, seq, LANES), lambda b, p, blk=blk: (b, 0, blk + p))

    def fixed(base):
        blk = base // LANES
        return pl.BlockSpec((None, seq, LANES), lambda b, p, blk=blk: (b, 0, blk))

    in_specs = [col(C_R), col(C_K), col(C_V), col(C_GA), fixed(C_DEC), fixed(C_ICL)]
    args = [proj3] * 6
    if not first:
        in_specs += [fixed(C_VRES), pl.BlockSpec((None, seq, LANES), lambda b, p: (b, 0, p))]
        args += [proj3, vfirst]
    in_specs += [pl.BlockSpec((16, LANES), lambda b, p: (0, p)),
                 pl.BlockSpec((8, LANES), lambda b, p: (0, 0)),
                 pl.BlockSpec((None, 3, LANES, 2 * LANES), lambda b, p: (p, 0, 0, 0))]
    args += [vecs, mul, wl]
    out_block = pl.BlockSpec((None, seq, LANES), lambda b, p: (b, 0, p))
    out_sds = jax.ShapeDtypeStruct((bsz, seq, D), F32)
    out_shape = (out_sds, out_sds) if first else out_sds
    out_specs = (out_block, out_block) if first else out_block
    scratch = [pltpu.VMEM((2, nc, CH + LANES, LANES), F32), pltpu.VMEM((2, nc, CH + LANES, LANES), F32),
               pltpu.VMEM((seq, LANES), F32), pltpu.VMEM((seq, LANES), F32), pltpu.VMEM((seq, LANES), F32),
               pltpu.VMEM((2, LANES, LANES), F32)]
    return pl.pallas_call(
        functools.partial(_wkv_kernel, seq=seq, first=first),
        grid=(bsz, D // LANES),
        in_specs=in_specs, out_specs=out_specs, out_shape=out_shape,
        scratch_shapes=scratch,
        compiler_params=pltpu.CompilerParams(dimension_semantics=("arbitrary", "arbitrary"),
                                             vmem_limit_bytes=VMEM_LIMIT),
        name="wkv7",
    )(*args)


def _proj_kernel(x_ref, g_ref, w_ref, o_ref, xn_ref):
    @pl.when(pl.program_id(1) == 0)
    def _():
        x = x_ref[...]
        ms = jnp.mean(x * x, axis=-1, keepdims=True)
        xn_ref[...] = (x * lax.rsqrt(ms + NORM_EPS) * g_ref[...]).astype(BF16)

    o_ref[...] = jnp.dot(xn_ref[...], w_ref[...], preferred_element_type=F32)


def _proj_call(x2, gain, w):
    t = x2.shape[0]
    tm = min(1024, t)
    tn = 1920
    return pl.pallas_call(
        _proj_kernel,
        grid=(t // tm, NPAD // tn),
        in_specs=[pl.BlockSpec((tm, D), lambda i, j: (i, 0)),
                  pl.BlockSpec((1, D), lambda i, j: (0, 0)),
                  pl.BlockSpec((D, tn), lambda i, j: (0, j))],
        out_specs=pl.BlockSpec((tm, tn), lambda i, j: (i, j)),
        out_shape=jax.ShapeDtypeStruct((t, NPAD), F32),
        scratch_shapes=[pltpu.VMEM((tm, D), BF16)],
        compiler_params=pltpu.CompilerParams(dimension_semantics=("arbitrary", "arbitrary"),
                                             vmem_limit_bytes=VMEM_LIMIT),
        name="proj",
    )(x2, gain, w)


def _ret_kernel(q_ref, k_ref, v_ref, gb_ref, cos_ref, sin_ref, lg_ref, gain_ref, o_ref, qr_s, kr_s, y_s, *, seq):
    nc = seq // RET_C
    p = pl.program_id(1)
    scale = HEAD ** -0.5

    def rot(c, _):
        sl = pl.ds(pl.multiple_of(c * RET_C, RET_C), RET_C)
        cs, sn = cos_ref[sl, :], sin_ref[sl, :]
        q = q_ref[sl, :]
        k = k_ref[sl, :]
        qr_s[sl, :] = (q * cs + pltpu.roll(q, 64, 1) * sn) * scale
        kr_s[sl, :] = k * cs + pltpu.roll(k, 64, 1) * sn
        return 0

    lax.fori_loop(0, nc, rot, 0)

    lane = _iota((RET_C, LANES), 1)
    rowi = _iota((RET_C, LANES), 0).astype(F32)
    absd = jnp.abs(_iota((RET_C, RET_C), 0) - _iota((RET_C, RET_C), 1)).astype(F32)
    for j in range(2):
        hmask = ((lane // 32) % 2) == j
        lg = lg_ref[pl.ds(2 * p + j, 1), :]
        lg128 = lg[:, :LANES]
        dmat = jnp.exp(absd * lg)
        xi_f = jnp.exp((rowi + 1.0) * lg128)
        ze_f = jnp.exp((RET_C - 1.0 - rowi) * lg128)
        xi_b = jnp.exp((RET_C - rowi) * lg128)
        ze_b = jnp.exp(rowi * lg128)
        cdec = jnp.exp(RET_C * lg128)
        vs = slice(j * LANES, (j + 1) * LANES)
        fst = jnp.zeros((LANES, LANES), F32)
        for i in range(nc):
            sl = slice(i * RET_C, (i + 1) * RET_C)
            qh = jnp.where(hmask, qr_s[sl, :], 0.0)
            kh = kr_s[sl, :]
            vh = v_ref[sl, vs]
            sc = _mm(qh, kh, _NT) * dmat
            y = _mm(sc, vh)
            if i > 0:
                y = y + _mm(qh * xi_f, fst)
            if i < nc - 1:
                fst = fst * cdec + _mm(jnp.where(hmask, kh, 0.0) * ze_f, vh, _TN)
            y_s[sl, vs] = y
        gst = jnp.zeros((LANES, LANES), F32)
        for i in range(nc - 1, -1, -1):
            sl = slice(i * RET_C, (i + 1) * RET_C)
            y = y_s[sl, vs]
            if i < nc - 1:
                qh = jnp.where(hmask, qr_s[sl, :], 0.0)
                y = y + _mm(qh * xi_b, gst)
            if i > 0:
                kh = jnp.where(hmask, kr_s[sl, :], 0.0)
                gst = gst * cdec + _mm(kh * ze_b, v_ref[sl, vs], _TN)
            mu = jnp.mean(y, axis=-1, keepdims=True)
            dv = y - mu
            var = jnp.mean(dv * dv, axis=-1, keepdims=True)
            out = dv * lax.rsqrt(var + NORM_EPS) * gain_ref[:, vs]
            o_ref[sl, vs] = out * jax.nn.silu(gb_ref[sl, vs])


def _ret_call(proj3, cos_t, sin_t, lg_t, gain):
    bsz, seq, _ = proj3.shape
    w2 = 2 * LANES

    def spec(base, width):
        blk = base // width
        return pl.BlockSpec((None, seq, width), lambda b, p, blk=blk: (b, 0, blk + p))

    return pl.pallas_call(
        functools.partial(_ret_kernel, seq=seq),
        grid=(bsz, RET_HEADS // 2),
        in_specs=[spec(C_QB, LANES), spec(C_KB, LANES), spec(C_VB, w2), spec(C_GB, w2),
                  pl.BlockSpec((seq, LANES), lambda b, p: (0, 0)),
                  pl.BlockSpec((seq, LANES), lambda b, p: (0, 0)),
                  pl.BlockSpec((RET_HEADS, w2), lambda b, p: (0, 0)),
                  pl.BlockSpec((1, w2), lambda b, p: (0, p))],
        out_specs=pl.BlockSpec((None, seq, w2), lambda b, p: (b, 0, p)),
        out_shape=jax.ShapeDtypeStruct((bsz, seq, D), F32),
        scratch_shapes=[pltpu.VMEM((seq, LANES), F32), pltpu.VMEM((seq, LANES), F32), pltpu.VMEM((seq, w2), F32)],
        compiler_params=pltpu.CompilerParams(dimension_semantics=("arbitrary", "arbitrary"),
                                             vmem_limit_bytes=VMEM_LIMIT),
        name="retention",
    )(proj3, proj3, proj3, proj3, cos_t, sin_t, lg_t, gain)


def _merge_kernel(x_ref, ya_ref, yb_ref, ma_ref, mb_ref, wa_ref, wb_ref, wo_ref, o_ref):
    za = jnp.dot(ya_ref[...].astype(BF16), wa_ref[...], preferred_element_type=F32)
    zb = jnp.dot(yb_ref[...].astype(BF16), wb_ref[...], preferred_element_type=F32)
    merged = jax.nn.sigmoid(ma_ref[...]) * za + jax.nn.sigmoid(mb_ref[...]) * zb
    o_ref[...] = x_ref[...] + jnp.dot(merged.astype(BF16), wo_ref[...], preferred_element_type=F32)


def _merge_call(x2, ya2, yb2, proj2, wa, wb, wo):
    t = x2.shape[0]
    tm = min(512, t)
    row = pl.BlockSpec((tm, D), lambda i: (i, 0))
    wspec = pl.BlockSpec((D, D), lambda i: (0, 0))
    return pl.pallas_call(
        _merge_kernel,
        grid=(t // tm,),
        in_specs=[row, row, row,
                  pl.BlockSpec((tm, D), lambda i: (i, C_MA // D)),
                  pl.BlockSpec((tm, D), lambda i: (i, C_MB // D)),
                  wspec, wspec, wspec],
        out_specs=row,
        out_shape=jax.ShapeDtypeStruct((t, D), F32),
        compiler_params=pltpu.CompilerParams(dimension_semantics=("arbitrary",),
                                             vmem_limit_bytes=VMEM_LIMIT),
        name="merge",
    )(x2, ya2, yb2, proj2, proj2, wa, wb, wo)


def _final_kernel(x_ref, g_ref, o_ref):
    x = x_ref[...]
    ms = jnp.mean(x * x, axis=-1, keepdims=True)
    o_ref[...] = x * lax.rsqrt(ms + NORM_EPS) * g_ref[...]


def _final_call(x2, gain):
    t = x2.shape[0]
    tm = min(1024, t)
    return pl.pallas_call(
        _final_kernel,
        grid=(t // tm,),
        in_specs=[pl.BlockSpec((tm, D), lambda i: (i, 0)), pl.BlockSpec((1, D), lambda i: (0, 0))],
        out_specs=pl.BlockSpec((tm, D), lambda i: (i, 0)),
        out_shape=jax.ShapeDtypeStruct((t, D), F32),
        compiler_params=pltpu.CompilerParams(dimension_semantics=("arbitrary",)),
        name="final_norm",
    )(x2, gain)


def _qk_perm():
    idx = []
    for pair in range(RET_HEADS // 2):
        for lane in range(LANES):
            seg, i = lane // 32, lane % 32
            head, half = 2 * pair + seg % 2, seg // 2
            idx.append(head * HEAD + half * 32 + i)
    return np.asarray(idx, np.int32)


def _pack_w_in(w_in, w_vres_down):
    depth = w_in.shape[0]
    perm = _qk_perm()
    vres = jnp.concatenate([jnp.zeros((1, D, 32), F32), w_vres_down], axis=0)
    vres = jnp.pad(vres, ((0, 0), (0, 0), (0, LANES - 32)))
    parts = [w_in[:, :, 0:3072], w_in[:, :, 3328:4352], w_in[:, :, 5376:6400], w_in[:, :, 6400:7424],
             w_in[:, :, 7424:8448], w_in[:, :, 8448:9472],
             w_in[:, :, 4352:4864][:, :, perm], w_in[:, :, 4864:5376][:, :, perm],
             w_in[:, :, 3072:3328], vres]
    w = jnp.concatenate(parts, axis=2)
    assert w.shape == (depth, D, NPAD)
    return w.astype(BF16)


def _pair_cat(wf, wb):
    pairs = D // LANES
    cat = jnp.concatenate([wf.reshape(LANES, pairs, LANES), wb.reshape(LANES, pairs, LANES)], axis=2)
    return cat.transpose(1, 0, 2)


def _rope_tables(seq):
    half = 32
    freqs = jnp.power(ROPE_BASE, -jnp.arange(half, dtype=F32) / half)
    ang = jnp.arange(seq, dtype=F32)[:, None] * freqs[None, :]
    cos = jnp.tile(jnp.cos(ang), (1, 4))
    sin = jnp.tile(jnp.sin(ang), (1, 4))
    sign = jnp.where(jnp.arange(LANES) < 64, -1.0, 1.0).astype(F32)
    return cos, sin * sign[None, :]


def kernel(x, norm_gain, w_in, w_vres_down, shift_prev, shift_next, w_decay_up, decay_bias, w_iclr_up, iclr_bias,
           w_vres_up, vres_bias, k_k, k_a, r_k, lnx_gain, lnx_bias, w_branch_a, ret_norm_gain, w_branch_b, w_out,
           final_gain):
    bsz, seq, _ = x.shape
    depth = w_in.shape[0]
    assert seq % RET_C == 0 and seq % SLAB == 0 and SLAB % CH == 0
    t = bsz * seq

    w_all = _pack_w_in(w_in, w_vres_down)
    wa_all, wb_all, wo_all = (w.astype(BF16) for w in (w_branch_a, w_branch_b, w_out))
    cos_t, sin_t = _rope_tables(seq)
    log_g = jnp.log1p(-jnp.exp2(-5.0 - jnp.arange(RET_HEADS, dtype=F32)))
    lg_t = jnp.broadcast_to(log_g[:, None], (RET_HEADS, 2 * LANES))
    zpad = jnp.zeros((64, D), F32)
    zmat = jnp.zeros((LANES, D), F32)

    x2 = x.reshape(t, D)
    vfirst = None
    for l in range(depth):
        proj2 = _proj_call(x2, norm_gain[l][None, :], w_all[l])
        proj3 = proj2.reshape(bsz, seq, NPAD)

        vb = vres_bias[l - 1] if l > 0 else jnp.zeros((D,), F32)
        vecs = jnp.stack([shift_prev[l, 0:1024], shift_prev[l, 1024:2048], shift_prev[l, 2048:3072],
                          shift_next[l, 0:1024], shift_next[l, 1024:2048], shift_next[l, 2048:3072],
                          decay_bias[l, 0], decay_bias[l, 1], iclr_bias[l, 0], iclr_bias[l, 1],
                          k_k[l], k_a[l], r_k[l].reshape(D), lnx_gain[l], lnx_bias[l], vb], axis=0)
        mul = jnp.concatenate([jnp.stack([shift_prev[l, 3072:3200], shift_prev[l, 3200:3328],
                                          shift_next[l, 3072:3200], shift_next[l, 3200:3328]], axis=0),
                               jnp.zeros((4, LANES), F32)], axis=0)
        wvr = jnp.pad(w_vres_up[l - 1], ((0, LANES - 32), (0, 0))) if l > 0 else zmat
        wl = jnp.stack([_pair_cat(jnp.concatenate([w_decay_up[l, 0], zpad], axis=0),
                                  jnp.concatenate([zpad, w_decay_up[l, 1]], axis=0)),
                        _pair_cat(jnp.concatenate([w_iclr_up[l, 0], zpad], axis=0),
                                  jnp.concatenate([zpad, w_iclr_up[l, 1]], axis=0)),
                        _pair_cat(wvr, zmat)], axis=1)

        if l == 0:
            ya, vfirst = _wkv_call(proj3, None, vecs, mul, wl, first=True)
        else:
            ya = _wkv_call(proj3, vfirst, vecs, mul, wl, first=False)
        yb = _ret_call(proj3, cos_t, sin_t, lg_t, ret_norm_gain[l][None, :])
        x2 = _merge_call(x2, ya.reshape(t, D), yb.reshape(t, D), proj2, wa_all[l], wb_all[l], wo_all[l])

    return _final_call(x2, final_gain[None, :]).reshape(bsz, seq, D)
```

```python
import functools

import numpy as np
import jax
import jax.numpy as jnp
from jax import lax
from jax.experimental import pallas as pl
from jax.experimental.pallas import tpu as pltpu

F32 = jnp.float32
BF16 = jnp.bfloat16

D = 1024
HEAD = 64
LANES = 128
CH = 64
SLAB = 512
RET_HEADS = 8
RET_C = 256
NORM_EPS = 1e-6
GN_EPS = 64e-5
EXP_M_HALF = 0.6065306597126334
ROPE_BASE = 10000.0
VMEM_LIMIT = 56 * 1024 * 1024

C_R, C_K, C_V, C_GA, C_VB, C_GB, C_MA, C_MB = (i * 1024 for i in range(8))
C_QB, C_KB, C_DEC, C_ICL, C_VRES, NPAD = 8192, 8704, 9216, 9344, 9472, 9600

P_LORA = 1
P_WKV = 1

_NN = (((1,), (0,)), ((), ()))
_NT = (((1,), (1,)), ((), ()))
_TN = (((0,), (0,)), ((), ()))


def _dg(a, b, dims):
    return lax.dot_general(a, b, dims, preferred_element_type=F32)


def _split2(x):
    hi = x.astype(BF16)
    lo = (x - hi.astype(F32)).astype(BF16)
    return hi, lo


def _mm(a, b, dims=_NN, prec=1):
    if prec == 1:
        return _dg(a.astype(BF16), b.astype(BF16), dims)
    ah, al = _split2(a)
    bh, bl = _split2(b)
    return _dg(ah, bh, dims) + (_dg(ah, bl, dims) + _dg(al, bh, dims))


def _headsum(xs, ones_bd):
    out = []
    for x in xs:
        hi, lo = _split2(x)
        out.append(_dg(hi, ones_bd, _NN) + _dg(lo, ones_bd, _NN))
    return out


def _iota(shape, axis):
    return lax.broadcasted_iota(jnp.int32, shape, axis)


def _pair_bd(x, bdmask):
    return jnp.where(bdmask, jnp.concatenate([x, x], axis=0), 0.0)


def _wkv_consts():
    row = _iota((CH, LANES), 0)
    col = _iota((CH, LANES), 1) % CH
    r128 = _iota((LANES, LANES), 0)
    c128 = _iota((LANES, LANES), 1)

    def masks(lower):
        late, early = (row, col) if lower else (col, row)
        return dict(
            strict=early < late, incl=early <= late,
            m_d=(early < late) & ((row // 16) == (col // 16)),
            m_o1=((row // 32) == (col // 32)) & ((late // 16) % 2 == 1) & ((early // 16) % 2 == 0),
            m_o2=((late // 32) == 1) & ((early // 32) == 0))

    return dict(bd=(r128 // CH) == (c128 // CH), eye128=r128 == c128,
                eye_cat=jnp.where(row == col, 1.0, 0.0).astype(F32),
                fwd=masks(True), bwd=masks(False))


def _wkv_chunks(items, cst):
    bdmask = cst["bd"]
    zero_bd = jnp.zeros((LANES, LANES), F32)

    def bd(x):
        return _pair_bd(x, bdmask)

    def mm(a, b, dims=_NN):
        return _mm(a, b, dims, prec=P_WKV)

    st = []
    for it in items:
        mk = cst["bwd"] if it["rev"] else cst["fwd"]
        cum, logw = it["cum"], it["logw"]
        e_inv = jnp.exp(-cum)
        rt = it["r"] * jnp.exp(cum)
        at = it["a"] * jnp.exp(cum - logw)
        kt = it["kd"] * e_inv
        bt = it["b"] * e_inv
        w_tot = jnp.exp(cum[0:1, :] if it["rev"] else cum[CH - 1:CH, :])
        st.append(dict(mk=mk, rt=rt, at=at, kt=kt, bt=bt, w_tot=w_tot, v=it["v"],
                       kb_end=jnp.concatenate([bt * w_tot, kt * w_tot], axis=0)))

    for s in st:
        amat = mm(jnp.concatenate([s["at"], s["rt"]], axis=0),
                  jnp.concatenate([bd(s["bt"]), bd(s["kt"])], axis=0), _NT)
        mk = s["mk"]
        s["n"] = jnp.where(mk["strict"], amat[:CH, :LANES], 0.0)
        s["ak"] = jnp.where(mk["strict"], amat[:CH, LANES:], 0.0)
        s["rb"] = jnp.where(mk["incl"], amat[CH:, :LANES], 0.0)
        s["rk"] = jnp.where(mk["incl"], amat[CH:, LANES:], 0.0)
    for s in st:
        s["x1"] = mm(s["ak"], bd(s["v"]))
    for s in st:
        s["p"] = jnp.where(s["mk"]["m_d"], s["n"], 0.0)
        s["t"] = cst["eye_cat"] + s["p"]
    for s in st:
        s["q"] = mm(s["p"], bd(s["p"]))
    for _ in range(2):
        for s in st:
            qt = mm(jnp.concatenate([s["q"], s["t"]], axis=0), bd(s["q"]))
            s["t"] = s["t"] + qt[CH:]
            s["q"] = qt[:CH]
    for s in st:
        s["t"] = s["t"] + mm(s["t"], bd(s["q"]))
    for key in ("m_o1", "m_o2"):
        for s in st:
            s["x"] = mm(s["t"], bd(jnp.where(s["mk"][key], s["n"], 0.0)))
        for s in st:
            s["t"] = s["t"] + mm(s["x"], bd(s["t"]))
    for s in st:
        au = mm(s["t"], jnp.concatenate([bd(s["at"]), bd(s["x1"])], axis=1))
        s["ah"], s["u0"] = au[:, :LANES], au[:, LANES:]
    out = []
    for s in st:
        ry = mm(s["rb"], jnp.concatenate([bd(s["ah"]), bd(s["u0"])], axis=1))
        y0 = ry[:, LANES:] + mm(s["rk"], bd(s["v"]))
        rhs2 = jnp.concatenate([jnp.concatenate([s["ah"], s["u0"]], axis=1),
                                jnp.concatenate([jnp.zeros((CH, LANES), F32), s["v"]], axis=1)], axis=0)
        mg = mm(s["kb_end"], rhs2, _TN)
        m = jnp.where(bdmask, mg[:, :LANES], 0.0) + jnp.where(cst["eye128"], s["w_tot"], 0.0)
        g = jnp.where(bdmask, mg[:, LANES:], 0.0)
        out.append((s["rt"] + ry[:, :LANES], y0, m, g))
    return out


PAD = 8


def _shifted(pad_ref, start, n, mp, mn):
    cur = pad_ref[pl.ds(start + PAD, n), :]
    prev = pad_ref[pl.ds(start + PAD - 1, n), :]
    nxt = pad_ref[pl.ds(start + PAD + 1, n), :]
    return cur * (1.0 - mp - mn) + mp * prev + mn * nxt


def _chunk_cumsum(x, rev):
    n = x.shape[0]
    sub = 8
    pos = _iota((n, LANES), 0) % CH
    s = 1
    while s < sub:
        if rev:
            x = x + jnp.where(pos < CH - s, pltpu.roll(x, n - s, 0), 0.0)
        else:
            x = x + jnp.where(pos >= s, pltpu.roll(x, s, 0), 0.0)
        s *= 2
    g = CH // sub
    x = x.reshape(n // CH, g, sub, LANES)
    s = 1
    while s < g:
        z = jnp.zeros((n // CH, s, sub, LANES), F32)
        if rev:
            x = x + jnp.concatenate([x[:, s:], z], axis=1)
        else:
            x = x + jnp.concatenate([z, x[:, :-s]], axis=1)
        s *= 2
    return x.reshape(n, LANES)


(V_MPR, V_MPK, V_MPV, V_MNR, V_MNK, V_MNV, V_DBF, V_DBB, V_IBF, V_IBB,
 V_KK, V_KA, V_RK, V_LG, V_LB, V_VB) = range(16)
W_DEC, W_ICL, W_VR = range(3)


def _wkv_kernel(*refs, seq, first):
    if first:
        (r_ref, k_ref, v_ref, ga_ref, dec_ref, icl_ref, vec_ref, mul_ref, wl_ref,
         ya_ref, vf_out_ref, lhs_s, add_s, yf_s, yb_s, bon_s, st_s, pad_s) = refs
        vres_ref = vf_ref = None
    else:
        (r_ref, k_ref, v_ref, ga_ref, dec_ref, icl_ref, vres_ref, vf_ref, vec_ref, mul_ref, wl_ref,
         ya_ref, lhs_s, add_s, yf_s, yb_s, bon_s, st_s, pad_s) = refs
        vf_out_ref = None
    nc = seq // CH
    per = SLAB // CH
    cst = _wkv_consts()
    ones_bd = jnp.where(cst["bd"], 1.0, 0.0).astype(BF16)

    def vec(i):
        return vec_ref[i:i + 1, :]

    shift_in = (r_ref, k_ref, v_ref, dec_ref, icl_ref)
    for a, ref in enumerate(shift_in):
        pad_s[a, 0:PAD, :] = jnp.zeros((PAD, LANES), F32)
        pad_s[a, seq + PAD:seq + 2 * PAD, :] = jnp.zeros((PAD, LANES), F32)

    def stage(sb, _):
        rows = pl.ds(pl.multiple_of(sb * SLAB, SLAB), SLAB)
        dst = pl.ds(pl.multiple_of(sb * SLAB, SLAB) + PAD, SLAB)
        for a, ref in enumerate(shift_in):
            pad_s[a, dst, :] = ref[rows, :]
        return 0

    lax.fori_loop(0, seq // SLAB, stage, 0)

    def prep(sb, _):
        start = pl.multiple_of(sb * SLAB, SLAB)
        rows = pl.ds(start, SLAB)
        r = _shifted(pad_s.at[0], start, SLAB, vec(V_MPR), vec(V_MNR))
        k = _shifted(pad_s.at[1], start, SLAB, vec(V_MPK), vec(V_MNK))
        v = _shifted(pad_s.at[2], start, SLAB, vec(V_MPV), vec(V_MNV))
        dec = _shifted(pad_s.at[3], start, SLAB, mul_ref[0:1, :], mul_ref[2:3, :])
        icl = _shifted(pad_s.at[4], start, SLAB, mul_ref[1:2, :], mul_ref[3:4, :])
        if first:
            vf_out_ref[rows, :] = v
        else:
            mix = jax.nn.sigmoid(vec(V_VB) + _mm(vres_ref[rows, :], wl_ref[W_VR], prec=P_LORA)[:, :LANES])
            v = v + (vf_ref[rows, :] - v) * mix
        zd = _mm(jnp.tanh(dec), wl_ref[W_DEC], prec=P_LORA)
        zi = _mm(icl, wl_ref[W_ICL], prec=P_LORA)
        kk = k * vec(V_KK)
        kds, ags, lws = [], [], []
        for d in range(2):
            z = vec(V_DBF + d) + zd[:, d * LANES:(d + 1) * LANES]
            lws.append(-EXP_M_HALF * jax.nn.sigmoid(z))
            ag = jax.nn.sigmoid(vec(V_IBF + d) + zi[:, d * LANES:(d + 1) * LANES])
            kds.append(k * (1.0 + (ag - 1.0) * vec(V_KA)))
            ags.append(ag)
        ssq, bsum = _headsum([kk * kk, r * (kds[0] + kds[1]) * vec(V_RK)], ones_bd)
        kk = kk * lax.rsqrt(ssq + 1e-12)
        bon_s[rows, :] = bsum * v
        items = []
        for d in range(2):
            cum = _chunk_cumsum(lws[d], d == 1)
            b = kk * ags[d]
            for i in range(per):
                sl = slice(i * CH, (i + 1) * CH)
                items.append(dict(r=r[sl], kd=kds[d][sl], v=v[sl], a=-kk[sl], b=b[sl], logw=lws[d][sl],
                                  cum=cum[sl], rev=d == 1))
        res = _wkv_chunks(items, cst)
        for d in range(2):
            for i in range(per):
                rh, y0, m, g = res[d * per + i]
                c = sb * per + i
                lhs_s[d, c, :CH, :] = rh
                lhs_s[d, c, CH:, :] = m
                add_s[d, c, :CH, :] = y0
                add_s[d, c, CH:, :] = g
        return 0

    lax.fori_loop(0, seq // SLAB, prep, 0)

    st_s[...] = jnp.zeros((2, LANES, LANES), F32)

    def scan(c, _):
        cb = nc - 1 - c
        of = _mm(lhs_s[0, c], st_s[0], prec=P_WKV) + add_s[0, c]
        ob = _mm(lhs_s[1, cb], st_s[1], prec=P_WKV) + add_s[1, cb]
        yf_s[pl.ds(pl.multiple_of(c * CH, CH), CH), :] = of[:CH]
        yb_s[pl.ds(pl.multiple_of(cb * CH, CH), CH), :] = ob[:CH]
        st_s[0] = of[CH:]
        st_s[1] = ob[CH:]
        return 0

    lax.fori_loop(0, nc, scan, 0)

    def post(sb, _):
        rows = pl.ds(pl.multiple_of(sb * SLAB, SLAB), SLAB)
        y = yf_s[rows, :] + yb_s[rows, :]
        mu = _headsum([y], ones_bd)[0] * (1.0 / HEAD)
        dv = y - mu
        var = _headsum([dv * dv], ones_bd)[0] * (1.0 / HEAD)
        out = dv * lax.rsqrt(var + GN_EPS) * vec(V_LG) + vec(V_LB) + bon_s[rows, :]
        ya_ref[rows, :] = out * jax.nn.silu(ga_ref[rows, :])
        return 0

    lax.fori_loop(0, seq // SLAB, post, 0)


def _wkv_call(proj3, vfirst, vecs, mul, wl, *, first):
    bsz, seq, _ = proj3.shape
    nc = seq // CH

    def col(base):
        blk = base // LANES
        return pl.BlockSpec((None, seq, LANES), lambda b, p, blk=blk: (b, 0, blk + p))

    def fixed(base):
        blk = base // LANES
        return pl.BlockSpec((None, seq, LANES), lambda b, p, blk=blk: (b, 0, blk))

    in_specs = [col(C_R), col(C_K), col(C_V), col(C_GA), fixed(C_DEC), fixed(C_ICL)]
    args = [proj3] * 6
    if not first:
        in_specs += [fixed(C_VRES), pl.BlockSpec((None, seq, LANES), lambda b, p: (b, 0, p))]
        args += [proj3, vfirst]
    in_specs += [pl.BlockSpec((16, LANES), lambda b, p: (0, p)),
                 pl.BlockSpec((8, LANES), lambda b, p: (0, 0)),
                 pl.BlockSpec((None, 3, LANES, 2 * LANES), lambda b, p: (p, 0, 0, 0))]
    args += [vecs, mul, wl]
    out_block = pl.BlockSpec((None, seq, LANES), lambda b, p: (b, 0, p))
    out_sds = jax.ShapeDtypeStruct((bsz, seq, D), F32)
    out_shape = (out_sds, out_sds) if first else out_sds
    out_specs = (out_block, out_block) if first else out_block
    scratch = [pltpu.VMEM((2, nc, CH + LANES, LANES), F32), pltpu.VMEM((2, nc, CH + LANES, LANES), F32),
               pltpu.VMEM((seq, LANES), F32), pltpu.VMEM((seq, LANES), F32), pltpu.VMEM((seq, LANES), F32),
               pltpu.VMEM((2, LANES, LANES), F32), pltpu.VMEM((5, seq + 2 * PAD, LANES), F32)]
    return pl.pallas_call(
        functools.partial(_wkv_kernel, seq=seq, first=first),
        grid=(bsz, D // LANES),
        in_specs=in_specs, out_specs=out_specs, out_shape=out_shape,
        scratch_shapes=scratch,
        compiler_params=pltpu.CompilerParams(dimension_semantics=("arbitrary", "arbitrary"),
                                             vmem_limit_bytes=VMEM_LIMIT),
        name="wkv7",
    )(*args)


def _proj_kernel(x_ref, g_ref, w_ref, o_ref, xn_ref):
    @pl.when(pl.program_id(1) == 0)
    def _():
        x = x_ref[...]
        ms = jnp.mean(x * x, axis=-1, keepdims=True)
        xn_ref[...] = (x * lax.rsqrt(ms + NORM_EPS) * g_ref[...]).astype(BF16)

    o_ref[...] = jnp.dot(xn_ref[...], w_ref[...], preferred_element_type=F32)


def _proj_call(x2, gain, w):
    t = x2.shape[0]
    tm = min(1024, t)
    tn = 1920
    return pl.pallas_call(
        _proj_kernel,
        grid=(t // tm, NPAD // tn),
        in_specs=[pl.BlockSpec((tm, D), lambda i, j: (i, 0)),
                  pl.BlockSpec((1, D), lambda i, j: (0, 0)),
                  pl.BlockSpec((D, tn), lambda i, j: (0, j))],
        out_specs=pl.BlockSpec((tm, tn), lambda i, j: (i, j)),
        out_shape=jax.ShapeDtypeStruct((t, NPAD), F32),
        scratch_shapes=[pltpu.VMEM((tm, D), BF16)],
        compiler_params=pltpu.CompilerParams(dimension_semantics=("arbitrary", "arbitrary"),
                                             vmem_limit_bytes=VMEM_LIMIT),
        name="proj",
    )(x2, gain, w)


def _ret_kernel(q_ref, k_ref, v_ref, gb_ref, cos_ref, sin_ref, lg_ref, gain_ref, o_ref, qr_s, kr_s, y_s, *, seq):
    nc = seq // RET_C
    p = pl.program_id(1)
    scale = HEAD ** -0.5

    def rot(c, _):
        sl = pl.ds(pl.multiple_of(c * RET_C, RET_C), RET_C)
        cs, sn = cos_ref[sl, :], sin_ref[sl, :]
        q = q_ref[sl, :]
        k = k_ref[sl, :]
        qr_s[sl, :] = (q * cs + pltpu.roll(q, 64, 1) * sn) * scale
        kr_s[sl, :] = k * cs + pltpu.roll(k, 64, 1) * sn
        return 0

    lax.fori_loop(0, nc, rot, 0)

    lane = _iota((RET_C, LANES), 1)
    rowi = _iota((RET_C, LANES), 0).astype(F32)
    absd = jnp.abs(_iota((RET_C, RET_C), 0) - _iota((RET_C, RET_C), 1)).astype(F32)
    for j in range(2):
        hmask = ((lane // 32) % 2) == j
        lg = lg_ref[pl.ds(2 * p + j, 1), :]
        lg128 = lg[:, :LANES]
        dmat = jnp.exp(absd * lg)
        xi_f = jnp.exp((rowi + 1.0) * lg128)
        ze_f = jnp.exp((RET_C - 1.0 - rowi) * lg128)
        xi_b = jnp.exp((RET_C - rowi) * lg128)
        ze_b = jnp.exp(rowi * lg128)
        cdec = jnp.exp(RET_C * lg128)
        vs = slice(j * LANES, (j + 1) * LANES)
        fst = jnp.zeros((LANES, LANES), F32)
        for i in range(nc):
            sl = slice(i * RET_C, (i + 1) * RET_C)
            qh = jnp.where(hmask, qr_s[sl, :], 0.0)
            kh = kr_s[sl, :]
            vh = v_ref[sl, vs]
            sc = _mm(qh, kh, _NT) * dmat
            y = _mm(sc, vh)
            if i > 0:
                y = y + _mm(qh * xi_f, fst)
            if i < nc - 1:
                fst = fst * cdec + _mm(jnp.where(hmask, kh, 0.0) * ze_f, vh, _TN)
            y_s[sl, vs] = y
        gst = jnp.zeros((LANES, LANES), F32)
        for i in range(nc - 1, -1, -1):
            sl = slice(i * RET_C, (i + 1) * RET_C)
            y = y_s[sl, vs]
            if i < nc - 1:
                qh = jnp.where(hmask, qr_s[sl, :], 0.0)
                y = y + _mm(qh * xi_b, gst)
            if i > 0:
                kh = jnp.where(hmask, kr_s[sl, :], 0.0)
                gst = gst * cdec + _mm(kh * ze_b, v_ref[sl, vs], _TN)
            mu = jnp.mean(y, axis=-1, keepdims=True)
            dv = y - mu
            var = jnp.mean(dv * dv, axis=-1, keepdims=True)
            out = dv * lax.rsqrt(var + NORM_EPS) * gain_ref[:, vs]
            o_ref[sl, vs] = out * jax.nn.silu(gb_ref[sl, vs])


def _ret_call(proj3, cos_t, sin_t, lg_t, gain):
    bsz, seq, _ = proj3.shape
    w2 = 2 * LANES

    def spec(base, width):
        blk = base // width
        return pl.BlockSpec((None, seq, width), lambda b, p, blk=blk: (b, 0, blk + p))

    return pl.pallas_call(
        functools.partial(_ret_kernel, seq=seq),
        grid=(bsz, RET_HEADS // 2),
        in_specs=[spec(C_QB, LANES), spec(C_KB, LANES), spec(C_VB, w2), spec(C_GB, w2),
                  pl.BlockSpec((seq, LANES), lambda b, p: (0, 0)),
                  pl.BlockSpec((seq, LANES), lambda b, p: (0, 0)),
                  pl.BlockSpec((RET_HEADS, w2), lambda b, p: (0, 0)),
                  pl.BlockSpec((1, w2), lambda b, p: (0, p))],
        out_specs=pl.BlockSpec((None, seq, w2), lambda b, p: (b, 0, p)),
        out_shape=jax.ShapeDtypeStruct((bsz, seq, D), F32),
        scratch_shapes=[pltpu.VMEM((seq, LANES), F32), pltpu.VMEM((seq, LANES), F32), pltpu.VMEM((seq, w2), F32)],
        compiler_params=pltpu.CompilerParams(dimension_semantics=("arbitrary", "arbitrary"),
                                             vmem_limit_bytes=VMEM_LIMIT),
        name="retention",
    )(proj3, proj3, proj3, proj3, cos_t, sin_t, lg_t, gain)


def _merge_kernel(x_ref, ya_ref, yb_ref, ma_ref, mb_ref, wa_ref, wb_ref, wo_ref, o_ref):
    za = jnp.dot(ya_ref[...].astype(BF16), wa_ref[...], preferred_element_type=F32)
    zb = jnp.dot(yb_ref[...].astype(BF16), wb_ref[...], preferred_element_type=F32)
    merged = jax.nn.sigmoid(ma_ref[...]) * za + jax.nn.sigmoid(mb_ref[...]) * zb
    o_ref[...] = x_ref[...] + jnp.dot(merged.astype(BF16), wo_ref[...], preferred_element_type=F32)


def _merge_call(x2, ya2, yb2, proj2, wa, wb, wo):
    t = x2.shape[0]
    tm = min(512, t)
    row = pl.BlockSpec((tm, D), lambda i: (i, 0))
    wspec = pl.BlockSpec((D, D), lambda i: (0, 0))
    return pl.pallas_call(
        _merge_kernel,
        grid=(t // tm,),
        in_specs=[row, row, row,
                  pl.BlockSpec((tm, D), lambda i: (i, C_MA // D)),
                  pl.BlockSpec((tm, D), lambda i: (i, C_MB // D)),
                  wspec, wspec, wspec],
        out_specs=row,
        out_shape=jax.ShapeDtypeStruct((t, D), F32),
        compiler_params=pltpu.CompilerParams(dimension_semantics=("arbitrary",),
                                             vmem_limit_bytes=VMEM_LIMIT),
        name="merge",
    )(x2, ya2, yb2, proj2, proj2, wa, wb, wo)


def _final_kernel(x_ref, g_ref, o_ref):
    x = x_ref[...]
    ms = jnp.mean(x * x, axis=-1, keepdims=True)
    o_ref[...] = x * lax.rsqrt(ms + NORM_EPS) * g_ref[...]


def _final_call(x2, gain):
    t = x2.shape[0]
    tm = min(1024, t)
    return pl.pallas_call(
        _final_kernel,
        grid=(t // tm,),
        in_specs=[pl.BlockSpec((tm, D), lambda i: (i, 0)), pl.BlockSpec((1, D), lambda i: (0, 0))],
        out_specs=pl.BlockSpec((tm, D), lambda i: (i, 0)),
        out_shape=jax.ShapeDtypeStruct((t, D), F32),
        compiler_params=pltpu.CompilerParams(dimension_semantics=("arbitrary",)),
        name="final_norm",
    )(x2, gain)


def _qk_perm():
    idx = []
    for pair in range(RET_HEADS // 2):
        for lane in range(LANES):
            seg, i = lane // 32, lane % 32
            head, half = 2 * pair + seg % 2, seg // 2
            idx.append(head * HEAD + half * 32 + i)
    return np.asarray(idx, np.int32)


def _pack_w_in(w_in, w_vres_down):
    depth = w_in.shape[0]
    perm = _qk_perm()
    vres = jnp.concatenate([jnp.zeros((1, D, 32), F32), w_vres_down], axis=0)
    vres = jnp.pad(vres, ((0, 0), (0, 0), (0, LANES - 32)))
    parts = [w_in[:, :, 0:3072], w_in[:, :, 3328:4352], w_in[:, :, 5376:6400], w_in[:, :, 6400:7424],
             w_in[:, :, 7424:8448], w_in[:, :, 8448:9472],
             w_in[:, :, 4352:4864][:, :, perm], w_in[:, :, 4864:5376][:, :, perm],
             w_in[:, :, 3072:3328], vres]
    w = jnp.concatenate(parts, axis=2)
    assert w.shape == (depth, D, NPAD)
    return w.astype(BF16)


def _pair_cat(wf, wb):
    pairs = D // LANES
    cat = jnp.concatenate([wf.reshape(LANES, pairs, LANES), wb.reshape(LANES, pairs, LANES)], axis=2)
    return cat.transpose(1, 0, 2)


def _rope_tables(seq):
    half = 32
    freqs = jnp.power(ROPE_BASE, -jnp.arange(half, dtype=F32) / half)
    ang = jnp.arange(seq, dtype=F32)[:, None] * freqs[None, :]
    cos = jnp.tile(jnp.cos(ang), (1, 4))
    sin = jnp.tile(jnp.sin(ang), (1, 4))
    sign = jnp.where(jnp.arange(LANES) < 64, -1.0, 1.0).astype(F32)
    return cos, sin * sign[None, :]


def kernel(x, norm_gain, w_in, w_vres_down, shift_prev, shift_next, w_decay_up, decay_bias, w_iclr_up, iclr_bias,
           w_vres_up, vres_bias, k_k, k_a, r_k, lnx_gain, lnx_bias, w_branch_a, ret_norm_gain, w_branch_b, w_out,
           final_gain):
    bsz, seq, _ = x.shape
    depth = w_in.shape[0]
    assert seq % RET_C == 0 and seq % SLAB == 0 and SLAB % CH == 0
    t = bsz * seq

    w_all = _pack_w_in(w_in, w_vres_down)
    wa_all, wb_all, wo_all = (w.astype(BF16) for w in (w_branch_a, w_branch_b, w_out))
    cos_t, sin_t = _rope_tables(seq)
    log_g = jnp.log1p(-jnp.exp2(-5.0 - jnp.arange(RET_HEADS, dtype=F32)))
    lg_t = jnp.broadcast_to(log_g[:, None], (RET_HEADS, 2 * LANES))
    zpad = jnp.zeros((64, D), F32)
    zmat = jnp.zeros((LANES, D), F32)

    x2 = x.reshape(t, D)
    vfirst = None
    for l in range(depth):
        proj2 = _proj_call(x2, norm_gain[l][None, :], w_all[l])
        proj3 = proj2.reshape(bsz, seq, NPAD)

        vb = vres_bias[l - 1] if l > 0 else jnp.zeros((D,), F32)
        vecs = jnp.stack([shift_prev[l, 0:1024], shift_prev[l, 1024:2048], shift_prev[l, 2048:3072],
                          shift_next[l, 0:1024], shift_next[l, 1024:2048], shift_next[l, 2048:3072],
                          decay_bias[l, 0], decay_bias[l, 1], iclr_bias[l, 0], iclr_bias[l, 1],
                          k_k[l], k_a[l], r_k[l].reshape(D), lnx_gain[l], lnx_bias[l], vb], axis=0)
        mul = jnp.concatenate([jnp.stack([shift_prev[l, 3072:3200], shift_prev[l, 3200:3328],
                                          shift_next[l, 3072:3200], shift_next[l, 3200:3328]], axis=0),
                               jnp.zeros((4, LANES), F32)], axis=0)
        wvr = jnp.pad(w_vres_up[l - 1], ((0, LANES - 32), (0, 0))) if l > 0 else zmat
        wl = jnp.stack([_pair_cat(jnp.concatenate([w_decay_up[l, 0], zpad], axis=0),
                                  jnp.concatenate([zpad, w_decay_up[l, 1]], axis=0)),
                        _pair_cat(jnp.concatenate([w_iclr_up[l, 0], zpad], axis=0),
                                  jnp.concatenate([zpad, w_iclr_up[l, 1]], axis=0)),
                        _pair_cat(wvr, zmat)], axis=1)

        if l == 0:
            ya, vfirst = _wkv_call(proj3, None, vecs, mul, wl, first=True)
        else:
            ya = _wkv_call(proj3, vfirst, vecs, mul, wl, first=False)
        yb = _ret_call(proj3, cos_t, sin_t, lg_t, ret_norm_gain[l][None, :])
        x2 = _merge_call(x2, ya.reshape(t, D), yb.reshape(t, D), proj2, wa_all[l], wb_all[l], wo_all[l])

    return _final_call(x2, final_gain[None, :]).reshape(bsz, seq, D)
```

```python
import functools

import numpy as np
import jax
import jax.numpy as jnp
from jax import lax
from jax.experimental import pallas as pl
from jax.experimental.pallas import tpu as pltpu

F32 = jnp.float32
BF16 = jnp.bfloat16

D = 1024
HEAD = 64
LANES = 128
CH = 64
SLAB = 512
RET_HEADS = 8
RET_C = 256
NORM_EPS = 1e-6
GN_EPS = 64e-5
EXP_M_HALF = 0.6065306597126334
ROPE_BASE = 10000.0
VMEM_LIMIT = 56 * 1024 * 1024

C_R, C_K, C_V, C_GA, C_VB, C_GB, C_MA, C_MB = (i * 1024 for i in range(8))
C_QB, C_KB, C_DEC, C_ICL, C_VRES, NPAD = 8192, 8704, 9216, 9344, 9472, 9600

P_LORA = 1
P_WKV = 1

_NN = (((1,), (0,)), ((), ()))
_NT = (((1,), (1,)), ((), ()))
_TN = (((0,), (0,)), ((), ()))


def _dg(a, b, dims):
    return lax.dot_general(a, b, dims, preferred_element_type=F32)


def _split2(x):
    hi = x.astype(BF16)
    lo = (x - hi.astype(F32)).astype(BF16)
    return hi, lo


def _mm(a, b, dims=_NN, prec=1):
    if prec == 1:
        return _dg(a.astype(BF16), b.astype(BF16), dims)
    ah, al = _split2(a)
    bh, bl = _split2(b)
    return _dg(ah, bh, dims) + (_dg(ah, bl, dims) + _dg(al, bh, dims))


def _headsum(xs, ones_bd):
    out = []
    for x in xs:
        hi, lo = _split2(x)
        out.append(_dg(hi, ones_bd, _NN) + _dg(lo, ones_bd, _NN))
    return out


def _iota(shape, axis):
    return lax.broadcasted_iota(jnp.int32, shape, axis)


def _pair_bd(x, bdmask):
    return jnp.where(bdmask, jnp.concatenate([x, x], axis=0), 0.0)


def _wkv_consts():
    row = _iota((CH, LANES), 0)
    col = _iota((CH, LANES), 1) % CH
    r128 = _iota((LANES, LANES), 0)
    c128 = _iota((LANES, LANES), 1)

    def masks(lower):
        late, early = (row, col) if lower else (col, row)
        return dict(
            strict=early < late, incl=early <= late,
            m_d=(early < late) & ((row // 16) == (col // 16)),
            m_o1=((row // 32) == (col // 32)) & ((late // 16) % 2 == 1) & ((early // 16) % 2 == 0),
            m_o2=((late // 32) == 1) & ((early // 32) == 0))

    return dict(bd=(r128 // CH) == (c128 // CH), eye128=r128 == c128,
                eye_cat=jnp.where(row == col, 1.0, 0.0).astype(F32),
                fwd=masks(True), bwd=masks(False))


def _wkv_mxu_steps(items, cst):
    bdmask = cst["bd"]

    def bd(x):
        return _pair_bd(x, bdmask)

    def mm(a, b, dims=_NN):
        return _mm(a, b, dims, prec=P_WKV)

    st = [dict(it=it, mk=cst["bwd"] if it["rev"] else cst["fwd"]) for it in items]
    for s in st:
        ld = s["it"]["load"]
        amat = mm(jnp.concatenate([ld("at"), ld("rt")], axis=0),
                  jnp.concatenate([bd(ld("bt")), bd(ld("kt"))], axis=0), _NT)
        mk = s["mk"]
        s["n"] = jnp.where(mk["strict"], amat[:CH, :LANES], 0.0)
        s["akrk"] = jnp.concatenate([jnp.where(mk["strict"], amat[:CH, LANES:], 0.0),
                                     jnp.where(mk["incl"], amat[CH:, LANES:], 0.0)], axis=0)
        s["rb"] = jnp.where(mk["incl"], amat[CH:, :LANES], 0.0)
    yield
    for s in st:
        xv = mm(s["akrk"], bd(s["it"]["load"]("v")))
        s["x1"], s["y0v"] = xv[:CH], xv[CH:]
    yield
    for s in st:
        s["p"] = jnp.where(s["mk"]["m_d"], s["n"], 0.0)
        s["t"] = cst["eye_cat"] + s["p"]
        s["q"] = mm(s["p"], bd(s["p"]))
    yield
    for _ in range(2):
        for s in st:
            qt = mm(jnp.concatenate([s["q"], s["t"]], axis=0), bd(s["q"]))
            s["t"] = s["t"] + qt[CH:]
            s["q"] = qt[:CH]
        yield
    for s in st:
        s["t"] = s["t"] + mm(s["t"], bd(s["q"]))
    yield
    for key in ("m_o1", "m_o2"):
        for s in st:
            s["x"] = mm(s["t"], bd(jnp.where(s["mk"][key], s["n"], 0.0)))
        yield
        for s in st:
            s["t"] = s["t"] + mm(s["x"], bd(s["t"]))
        yield
    for s in st:
        au = mm(s["t"], jnp.concatenate([bd(s["it"]["load"]("at")), bd(s["x1"])], axis=1))
        s["ah"], s["u0"] = au[:, :LANES], au[:, LANES:]
    yield
    for s in st:
        ld = s["it"]["load"]
        w_tot = s["it"]["w_tot"]()
        ry = mm(s["rb"], jnp.concatenate([bd(s["ah"]), bd(s["u0"])], axis=1))
        rhs2 = jnp.concatenate([jnp.concatenate([s["ah"], s["u0"]], axis=1),
                                jnp.concatenate([jnp.zeros((CH, LANES), F32), ld("v")], axis=1)], axis=0)
        kb_end = jnp.concatenate([ld("bt") * w_tot, ld("kt") * w_tot], axis=0)
        mg = mm(kb_end, rhs2, _TN)
        m = jnp.where(bdmask, mg[:, :LANES], 0.0) + jnp.where(cst["eye128"], w_tot, 0.0)
        g = jnp.where(bdmask, mg[:, LANES:], 0.0)
        s["it"]["store"](ld("rt") + ry[:, :LANES], ry[:, LANES:] + s["y0v"], m, g)
    yield


def _run(gen):
    for _ in gen:
        pass


def _interleave(main, side, ratio):
    for _ in main:
        for _ in range(ratio):
            next(side, None)
    _run(side)


def _chain(gens):
    for g in gens:
        yield from g


PAD = 8


def _shifted(pad_ref, start, n, mp, mn):
    cur = pad_ref[pl.ds(start + PAD, n), :]
    prev = pad_ref[pl.ds(start + PAD - 1, n), :]
    nxt = pad_ref[pl.ds(start + PAD + 1, n), :]
    return cur * (1.0 - mp - mn) + mp * prev + mn * nxt


def _chunk_cumsum(x, rev):
    n = x.shape[0]
    sub = 8
    pos = _iota((n, LANES), 0) % CH
    s = 1
    while s < sub:
        if rev:
            x = x + jnp.where(pos < CH - s, pltpu.roll(x, n - s, 0), 0.0)
        else:
            x = x + jnp.where(pos >= s, pltpu.roll(x, s, 0), 0.0)
        s *= 2
    g = CH // sub
    x = x.reshape(n // CH, g, sub, LANES)
    s = 1
    while s < g:
        z = jnp.zeros((n // CH, s, sub, LANES), F32)
        if rev:
            x = x + jnp.concatenate([x[:, s:], z], axis=1)
        else:
            x = x + jnp.concatenate([z, x[:, :-s]], axis=1)
        s *= 2
    return x.reshape(n, LANES)


(V_MPR, V_MPK, V_MPV, V_MNR, V_MNK, V_MNV, V_DBF, V_DBB, V_IBF, V_IBB,
 V_KK, V_KA, V_RK, V_LG, V_LB, V_VB) = range(16)
W_DEC, W_ICL, W_VR = range(3)
I_RT, I_AT, I_KT, I_BT = range(4)
I_NAMES = dict(rt=I_RT, at=I_AT, kt=I_KT, bt=I_BT)


def _wkv_kernel(*refs, seq, first):
    if first:
        (r_ref, k_ref, v_ref, ga_ref, dec_ref, icl_ref, vec_ref, mul_ref, wl_ref,
         ya_ref, vf_out_ref, lhs_s, add_s, yf_s, yb_s, bon_s, pad_s, it_s, v_s, wt_s) = refs
        vres_ref = vf_ref = None
    else:
        (r_ref, k_ref, v_ref, ga_ref, dec_ref, icl_ref, vres_ref, vf_ref, vec_ref, mul_ref, wl_ref,
         ya_ref, lhs_s, add_s, yf_s, yb_s, bon_s, pad_s, it_s, v_s, wt_s) = refs
        vf_out_ref = None
    ns = seq // SLAB
    per = SLAB // CH
    cst = _wkv_consts()
    ones_bd = jnp.where(cst["bd"], 1.0, 0.0).astype(BF16)

    def vec(i):
        return vec_ref[i:i + 1, :]

    shift_in = (r_ref, k_ref, v_ref, dec_ref, icl_ref)
    for a in range(len(shift_in)):
        pad_s[a, 0:PAD, :] = jnp.zeros((PAD, LANES), F32)
        pad_s[a, seq + PAD:seq + 2 * PAD, :] = jnp.zeros((PAD, LANES), F32)

    def stage(sb, _):
        rows = pl.ds(pl.multiple_of(sb * SLAB, SLAB), SLAB)
        dst = pl.ds(pl.multiple_of(sb * SLAB, SLAB) + PAD, SLAB)
        for a, ref in enumerate(shift_in):
            pad_s[a, dst, :] = ref[rows, :]
        return 0

    lax.fori_loop(0, ns, stage, 0)

    def prologue(sb):
        start = sb * SLAB
        rows = slice(start, start + SLAB)
        r = _shifted(pad_s.at[0], start, SLAB, vec(V_MPR), vec(V_MNR))
        yield
        k = _shifted(pad_s.at[1], start, SLAB, vec(V_MPK), vec(V_MNK))
        yield
        v = _shifted(pad_s.at[2], start, SLAB, vec(V_MPV), vec(V_MNV))
        yield
        dec = _shifted(pad_s.at[3], start, SLAB, mul_ref[0:1, :], mul_ref[2:3, :])
        icl = _shifted(pad_s.at[4], start, SLAB, mul_ref[1:2, :], mul_ref[3:4, :])
        yield
        if first:
            vf_out_ref[rows, :] = v
        else:
            mix = jax.nn.sigmoid(vec(V_VB) + _mm(vres_ref[rows, :], wl_ref[W_VR], prec=P_LORA)[:, :LANES])
            v = v + (vf_ref[rows, :] - v) * mix
        v_s[rows, :] = v
        yield
        zd = _mm(jnp.tanh(dec), wl_ref[W_DEC], prec=P_LORA)
        zi = _mm(icl, wl_ref[W_ICL], prec=P_LORA)
        kk = k * vec(V_KK)
        yield
        kds, ags, lws = [], [], []
        for d in range(2):
            z = vec(V_DBF + d) + zd[:, d * LANES:(d + 1) * LANES]
            lws.append(-EXP_M_HALF * jax.nn.sigmoid(z))
            ag = jax.nn.sigmoid(vec(V_IBF + d) + zi[:, d * LANES:(d + 1) * LANES])
            kds.append(k * (1.0 + (ag - 1.0) * vec(V_KA)))
            ags.append(ag)
            yield
        ssq, bsum = _headsum([kk * kk, r * (kds[0] + kds[1]) * vec(V_RK)], ones_bd)
        kk = kk * lax.rsqrt(ssq + 1e-12)
        bon_s[rows, :] = bsum * v
        yield
        for d in range(2):
            cum = _chunk_cumsum(lws[d], d == 1)
            yield
            e_inv = jnp.exp(-cum)
            it_s[d, I_RT, rows, :] = r * jnp.exp(cum)
            it_s[d, I_AT, rows, :] = -kk * jnp.exp(cum - lws[d])
            yield
            it_s[d, I_KT, rows, :] = kds[d] * e_inv
            it_s[d, I_BT, rows, :] = (kk * ags[d]) * e_inv
            for i in range(per):
                c = sb * per + i
                row = i * CH if d == 1 else i * CH + CH - 1
                wt_s[d, c * 8:(c + 1) * 8, :] = jnp.broadcast_to(jnp.exp(cum[row:row + 1, :]), (8, LANES))
            yield

    def chunk_items(step):
        buf = step % 2
        items = []
        for d in range(2):
            slab = step if d == 0 else ns - 1 - step
            for i in range(per):
                c = slab * per + i
                crow = slice(c * CH, (c + 1) * CH)

                def load(name, d=d, crow=crow):
                    if name == "v":
                        return v_s[crow, :]
                    return it_s[d, I_NAMES[name], crow, :]

                def w_tot(d=d, c=c):
                    return wt_s[d, c * 8:c * 8 + 1, :]

                def store(rh, y0, m, g, d=d, i=i):
                    lhs_s[buf, d, i, :CH, :] = rh
                    lhs_s[buf, d, i, CH:, :] = m
                    add_s[buf, d, i, :CH, :] = y0
                    add_s[buf, d, i, CH:, :] = g

                items.append(dict(rev=d == 1, load=load, w_tot=w_tot, store=store))
        return items

    state = [jnp.zeros((LANES, LANES), F32), jnp.zeros((LANES, LANES), F32)]

    def scan(step):
        buf = step % 2
        for j in range(per):
            for d in range(2):
                i = j if d == 0 else per - 1 - j
                slab = step if d == 0 else ns - 1 - step
                o = _mm(lhs_s[buf, d, i], state[d], prec=P_WKV) + add_s[buf, d, i]
                rows = slice((slab * per + i) * CH, (slab * per + i + 1) * CH)
                (yf_s if d == 0 else yb_s)[rows, :] = o[:CH]
                state[d] = o[CH:]
            yield

    def post(sb):
        rows = slice(sb * SLAB, (sb + 1) * SLAB)
        y = yf_s[rows, :] + yb_s[rows, :]
        mu = _headsum([y], ones_bd)[0] * (1.0 / HEAD)
        yield
        dv = y - mu
        var = _headsum([dv * dv], ones_bd)[0] * (1.0 / HEAD)
        yield
        out = dv * lax.rsqrt(var + GN_EPS) * vec(V_LG) + vec(V_LB) + bon_s[rows, :]
        ya_ref[rows, :] = out * jax.nn.silu(ga_ref[rows, :])
        yield

    ends = sorted({0, ns - 1})
    inner = [sb for sb in range(ns) if sb not in ends]
    for sb in ends:
        _run(prologue(sb))
    for step in range(ns):
        side = _chain([prologue(sb) for sb in inner]) if step == 0 else scan(step - 1)
        _interleave(_wkv_mxu_steps(chunk_items(step), cst), side, 3 if step == 0 else 1)
    _interleave(scan(ns - 1), _chain([post(sb) for sb in inner]), 1)
    for sb in ends:
        _run(post(sb))


def _wkv_call(proj3, vfirst, vecs, mul, wl, *, first):
    bsz, seq, _ = proj3.shape
    nc = seq // CH
    per = SLAB // CH

    def col(base):
        blk = base // LANES
        return pl.BlockSpec((None, seq, LANES), lambda b, p, blk=blk: (b, 0, blk + p))

    def fixed(base):
        blk = base // LANES
        return pl.BlockSpec((None, seq, LANES), lambda b, p, blk=blk: (b, 0, blk))

    in_specs = [col(C_R), col(C_K), col(C_V), col(C_GA), fixed(C_DEC), fixed(C_ICL)]
    args = [proj3] * 6
    if not first:
        in_specs += [fixed(C_VRES), pl.BlockSpec((None, seq, LANES), lambda b, p: (b, 0, p))]
        args += [proj3, vfirst]
    in_specs += [pl.BlockSpec((16, LANES), lambda b, p: (0, p)),
                 pl.BlockSpec((8, LANES), lambda b, p: (0, 0)),
                 pl.BlockSpec((None, 3, LANES, 2 * LANES), lambda b, p: (p, 0, 0, 0))]
    args += [vecs, mul, wl]
    out_block = pl.BlockSpec((None, seq, LANES), lambda b, p: (b, 0, p))
    out_sds = jax.ShapeDtypeStruct((bsz, seq, D), F32)
    out_shape = (out_sds, out_sds) if first else out_sds
    out_specs = (out_block, out_block) if first else out_block
    scratch = [pltpu.VMEM((2, 2, per, CH + LANES, LANES), F32),
               pltpu.VMEM((2, 2, per, CH + LANES, LANES), F32),
               pltpu.VMEM((seq, LANES), F32), pltpu.VMEM((seq, LANES), F32), pltpu.VMEM((seq, LANES), F32),
               pltpu.VMEM((5, seq + 2 * PAD, LANES), F32),
               pltpu.VMEM((2, 4, seq, LANES), F32),
               pltpu.VMEM((seq, LANES), F32),
               pltpu.VMEM((2, nc * 8, LANES), F32)]
    return pl.pallas_call(
        functools.partial(_wkv_kernel, seq=seq, first=first),
        grid=(bsz, D // LANES),
        in_specs=in_specs, out_specs=out_specs, out_shape=out_shape,
        scratch_shapes=scratch,
        compiler_params=pltpu.CompilerParams(dimension_semantics=("arbitrary", "arbitrary"),
                                             vmem_limit_bytes=VMEM_LIMIT),
        name="wkv7",
    )(*args)


def _proj_kernel(x_ref, g_ref, w_ref, o_ref, xn_ref):
    @pl.when(pl.program_id(1) == 0)
    def _():
        x = x_ref[...]
        ms = jnp.mean(x * x, axis=-1, keepdims=True)
        xn_ref[...] = (x * lax.rsqrt(ms + NORM_EPS) * g_ref[...]).astype(BF16)

    o_ref[...] = jnp.dot(xn_ref[...], w_ref[...], preferred_element_type=F32)


def _proj_call(x2, gain, w):
    t = x2.shape[0]
    tm = min(1024, t)
    tn = 1920
    return pl.pallas_call(
        _proj_kernel,
        grid=(t // tm, NPAD // tn),
        in_specs=[pl.BlockSpec((tm, D), lambda i, j: (i, 0)),
                  pl.BlockSpec((1, D), lambda i, j: (0, 0)),
                  pl.BlockSpec((D, tn), lambda i, j: (0, j))],
        out_specs=pl.BlockSpec((tm, tn), lambda i, j: (i, j)),
        out_shape=jax.ShapeDtypeStruct((t, NPAD), F32),
        scratch_shapes=[pltpu.VMEM((tm, D), BF16)],
        compiler_params=pltpu.CompilerParams(dimension_semantics=("arbitrary", "arbitrary"),
                                             vmem_limit_bytes=VMEM_LIMIT),
        name="proj",
    )(x2, gain, w)


def _ret_kernel(q_ref, k_ref, v_ref, gb_ref, cos_ref, sin_ref, lg_ref, gain_ref, o_ref, qr_s, kr_s, y_s, *, seq):
    nc = seq // RET_C
    p = pl.program_id(1)
    scale = HEAD ** -0.5

    def rot(c, _):
        sl = pl.ds(pl.multiple_of(c * RET_C, RET_C), RET_C)
        cs, sn = cos_ref[sl, :], sin_ref[sl, :]
        q = q_ref[sl, :]
        k = k_ref[sl, :]
        qr_s[sl, :] = (q * cs + pltpu.roll(q, 64, 1) * sn) * scale
        kr_s[sl, :] = k * cs + pltpu.roll(k, 64, 1) * sn
        return 0

    lax.fori_loop(0, nc, rot, 0)

    lane = _iota((RET_C, LANES), 1)
    rowi = _iota((RET_C, LANES), 0).astype(F32)
    absd = jnp.abs(_iota((RET_C, RET_C), 0) - _iota((RET_C, RET_C), 1)).astype(F32)
    for j in range(2):
        hmask = ((lane // 32) % 2) == j
        lg = lg_ref[pl.ds(2 * p + j, 1), :]
        lg128 = lg[:, :LANES]
        dmat = jnp.exp(absd * lg)
        xi_f = jnp.exp((rowi + 1.0) * lg128)
        ze_f = jnp.exp((RET_C - 1.0 - rowi) * lg128)
        xi_b = jnp.exp((RET_C - rowi) * lg128)
        ze_b = jnp.exp(rowi * lg128)
        cdec = jnp.exp(RET_C * lg128)
        vs = slice(j * LANES, (j + 1) * LANES)
        fst = jnp.zeros((LANES, LANES), F32)
        for i in range(nc):
            sl = slice(i * RET_C, (i + 1) * RET_C)
            qh = jnp.where(hmask, qr_s[sl, :], 0.0)
            kh = kr_s[sl, :]
            vh = v_ref[sl, vs]
            sc = _mm(qh, kh, _NT) * dmat
            y = _mm(sc, vh)
            if i > 0:
                y = y + _mm(qh * xi_f, fst)
            if i < nc - 1:
                fst = fst * cdec + _mm(jnp.where(hmask, kh, 0.0) * ze_f, vh, _TN)
            y_s[sl, vs] = y
        gst = jnp.zeros((LANES, LANES), F32)
        for i in range(nc - 1, -1, -1):
            sl = slice(i * RET_C, (i + 1) * RET_C)
            y = y_s[sl, vs]
            if i < nc - 1:
                qh = jnp.where(hmask, qr_s[sl, :], 0.0)
                y = y + _mm(qh * xi_b, gst)
            if i > 0:
                kh = jnp.where(hmask, kr_s[sl, :], 0.0)
                gst = gst * cdec + _mm(kh * ze_b, v_ref[sl, vs], _TN)
            mu = jnp.mean(y, axis=-1, keepdims=True)
            dv = y - mu
            var = jnp.mean(dv * dv, axis=-1, keepdims=True)
            out = dv * lax.rsqrt(var + NORM_EPS) * gain_ref[:, vs]
            o_ref[sl, vs] = out * jax.nn.silu(gb_ref[sl, vs])


def _ret_call(proj3, cos_t, sin_t, lg_t, gain):
    bsz, seq, _ = proj3.shape
    w2 = 2 * LANES

    def spec(base, width):
        blk = base // width
        return pl.BlockSpec((None, seq, width), lambda b, p, blk=blk: (b, 0, blk + p))

    return pl.pallas_call(
        functools.partial(_ret_kernel, seq=seq),
        grid=(bsz, RET_HEADS // 2),
        in_specs=[spec(C_QB, LANES), spec(C_KB, LANES), spec(C_VB, w2), spec(C_GB, w2),
                  pl.BlockSpec((seq, LANES), lambda b, p: (0, 0)),
                  pl.BlockSpec((seq, LANES), lambda b, p: (0, 0)),
                  pl.BlockSpec((RET_HEADS, w2), lambda b, p: (0, 0)),
                  pl.BlockSpec((1, w2), lambda b, p: (0, p))],
        out_specs=pl.BlockSpec((None, seq, w2), lambda b, p: (b, 0, p)),
        out_shape=jax.ShapeDtypeStruct((bsz, seq, D), F32),
        scratch_shapes=[pltpu.VMEM((seq, LANES), F32), pltpu.VMEM((seq, LANES), F32), pltpu.VMEM((seq, w2), F32)],
        compiler_params=pltpu.CompilerParams(dimension_semantics=("arbitrary", "arbitrary"),
                                             vmem_limit_bytes=VMEM_LIMIT),
        name="retention",
    )(proj3, proj3, proj3, proj3, cos_t, sin_t, lg_t, gain)


def _merge_kernel(x_ref, ya_ref, yb_ref, ma_ref, mb_ref, wa_ref, wb_ref, wo_ref, o_ref):
    za = jnp.dot(ya_ref[...].astype(BF16), wa_ref[...], preferred_element_type=F32)
    zb = jnp.dot(yb_ref[...].astype(BF16), wb_ref[...], preferred_element_type=F32)
    merged = jax.nn.sigmoid(ma_ref[...]) * za + jax.nn.sigmoid(mb_ref[...]) * zb
    o_ref[...] = x_ref[...] + jnp.dot(merged.astype(BF16), wo_ref[...], preferred_element_type=F32)


def _merge_call(x2, ya2, yb2, proj2, wa, wb, wo):
    t = x2.shape[0]
    tm = min(512, t)
    row = pl.BlockSpec((tm, D), lambda i: (i, 0))
    wspec = pl.BlockSpec((D, D), lambda i: (0, 0))
    return pl.pallas_call(
        _merge_kernel,
        grid=(t // tm,),
        in_specs=[row, row, row,
                  pl.BlockSpec((tm, D), lambda i: (i, C_MA // D)),
                  pl.BlockSpec((tm, D), lambda i: (i, C_MB // D)),
                  wspec, wspec, wspec],
        out_specs=row,
        out_shape=jax.ShapeDtypeStruct((t, D), F32),
        compiler_params=pltpu.CompilerParams(dimension_semantics=("arbitrary",),
                                             vmem_limit_bytes=VMEM_LIMIT),
        name="merge",
    )(x2, ya2, yb2, proj2, proj2, wa, wb, wo)


def _final_kernel(x_ref, g_ref, o_ref):
    x = x_ref[...]
    ms = jnp.mean(x * x, axis=-1, keepdims=True)
    o_ref[...] = x * lax.rsqrt(ms + NORM_EPS) * g_ref[...]


def _final_call(x2, gain):
    t = x2.shape[0]
    tm = min(1024, t)
    return pl.pallas_call(
        _final_kernel,
        grid=(t // tm,),
        in_specs=[pl.BlockSpec((tm, D), lambda i: (i, 0)), pl.BlockSpec((1, D), lambda i: (0, 0))],
        out_specs=pl.BlockSpec((tm, D), lambda i: (i, 0)),
        out_shape=jax.ShapeDtypeStruct((t, D), F32),
        compiler_params=pltpu.CompilerParams(dimension_semantics=("arbitrary",)),
        name="final_norm",
    )(x2, gain)


def _qk_perm():
    idx = []
    for pair in range(RET_HEADS // 2):
        for lane in range(LANES):
            seg, i = lane // 32, lane % 32
            head, half = 2 * pair + seg % 2, seg // 2
            idx.append(head * HEAD + half * 32 + i)
    return np.asarray(idx, np.int32)


def _pack_w_in(w_in, w_vres_down):
    depth = w_in.shape[0]
    perm = _qk_perm()
    vres = jnp.concatenate([jnp.zeros((1, D, 32), F32), w_vres_down], axis=0)
    vres = jnp.pad(vres, ((0, 0), (0, 0), (0, LANES - 32)))
    parts = [w_in[:, :, 0:3072], w_in[:, :, 3328:4352], w_in[:, :, 5376:6400], w_in[:, :, 6400:7424],
             w_in[:, :, 7424:8448], w_in[:, :, 8448:9472],
             w_in[:, :, 4352:4864][:, :, perm], w_in[:, :, 4864:5376][:, :, perm],
             w_in[:, :, 3072:3328], vres]
    w = jnp.concatenate(parts, axis=2)
    assert w.shape == (depth, D, NPAD)
    return w.astype(BF16)


def _pair_cat(wf, wb):
    pairs = D // LANES
    cat = jnp.concatenate([wf.reshape(LANES, pairs, LANES), wb.reshape(LANES, pairs, LANES)], axis=2)
    return cat.transpose(1, 0, 2)


def _rope_tables(seq):
    half = 32
    freqs = jnp.power(ROPE_BASE, -jnp.arange(half, dtype=F32) / half)
    ang = jnp.arange(seq, dtype=F32)[:, None] * freqs[None, :]
    cos = jnp.tile(jnp.cos(ang), (1, 4))
    sin = jnp.tile(jnp.sin(ang), (1, 4))
    sign = jnp.where(jnp.arange(LANES) < 64, -1.0, 1.0).astype(F32)
    return cos, sin * sign[None, :]


def kernel(x, norm_gain, w_in, w_vres_down, shift_prev, shift_next, w_decay_up, decay_bias, w_iclr_up, iclr_bias,
           w_vres_up, vres_bias, k_k, k_a, r_k, lnx_gain, lnx_bias, w_branch_a, ret_norm_gain, w_branch_b, w_out,
           final_gain):
    bsz, seq, _ = x.shape
    depth = w_in.shape[0]
    assert seq % RET_C == 0 and seq % SLAB == 0 and SLAB % CH == 0
    t = bsz * seq

    w_all = _pack_w_in(w_in, w_vres_down)
    wa_all, wb_all, wo_all = (w.astype(BF16) for w in (w_branch_a, w_branch_b, w_out))
    cos_t, sin_t = _rope_tables(seq)
    log_g = jnp.log1p(-jnp.exp2(-5.0 - jnp.arange(RET_HEADS, dtype=F32)))
    lg_t = jnp.broadcast_to(log_g[:, None], (RET_HEADS, 2 * LANES))
    zpad = jnp.zeros((64, D), F32)
    zmat = jnp.zeros((LANES, D), F32)

    x2 = x.reshape(t, D)
    vfirst = None
    for l in range(depth):
        proj2 = _proj_call(x2, norm_gain[l][None, :], w_all[l])
        proj3 = proj2.reshape(bsz, seq, NPAD)

        vb = vres_bias[l - 1] if l > 0 else jnp.zeros((D,), F32)
        vecs = jnp.stack([shift_prev[l, 0:1024], shift_prev[l, 1024:2048], shift_prev[l, 2048:3072],
                          shift_next[l, 0:1024], shift_next[l, 1024:2048], shift_next[l, 2048:3072],
                          decay_bias[l, 0], decay_bias[l, 1], iclr_bias[l, 0], iclr_bias[l, 1],
                          k_k[l], k_a[l], r_k[l].reshape(D), lnx_gain[l], lnx_bias[l], vb], axis=0)
        mul = jnp.concatenate([jnp.stack([shift_prev[l, 3072:3200], shift_prev[l, 3200:3328],
                                          shift_next[l, 3072:3200], shift_next[l, 3200:3328]], axis=0),
                               jnp.zeros((4, LANES), F32)], axis=0)
        wvr = jnp.pad(w_vres_up[l - 1], ((0, LANES - 32), (0, 0))) if l > 0 else zmat
        wl = jnp.stack([_pair_cat(jnp.concatenate([w_decay_up[l, 0], zpad], axis=0),
                                  jnp.concatenate([zpad, w_decay_up[l, 1]], axis=0)),
                        _pair_cat(jnp.concatenate([w_iclr_up[l, 0], zpad], axis=0),
                                  jnp.concatenate([zpad, w_iclr_up[l, 1]], axis=0)),
                        _pair_cat(wvr, zmat)], axis=1)

        if l == 0:
            ya, vfirst = _wkv_call(proj3, None, vecs, mul, wl, first=True)
        else:
            ya = _wkv_call(proj3, vfirst, vecs, mul, wl, first=False)
        yb = _ret_call(proj3, cos_t, sin_t, lg_t, ret_norm_gain[l][None, :])
        x2 = _merge_call(x2, ya.reshape(t, D), yb.reshape(t, D), proj2, wa_all[l], wb_all[l], wo_all[l])

    return _final_call(x2, final_gain[None, :]).reshape(bsz, seq, D)
```

```python
import functools

import numpy as np
import jax
import jax.numpy as jnp
from jax import lax
from jax.experimental import pallas as pl
from jax.experimental.pallas import tpu as pltpu

F32 = jnp.float32
BF16 = jnp.bfloat16

D = 1024
HEAD = 64
LANES = 128
CH = 64
SLAB = 512
RET_HEADS = 8
RET_C = 256
NORM_EPS = 1e-6
GN_EPS = 64e-5
EXP_M_HALF = 0.6065306597126334
ROPE_BASE = 10000.0
VMEM_LIMIT = 56 * 1024 * 1024

C_R, C_K, C_V, C_GA, C_VB, C_GB, C_MA, C_MB = (i * 1024 for i in range(8))
C_QB, C_KB, C_DEC, C_ICL, C_VRES, NPAD = 8192, 8704, 9216, 9344, 9472, 9600

P_LORA = 1
P_WKV = 1

_NN = (((1,), (0,)), ((), ()))
_NT = (((1,), (1,)), ((), ()))
_TN = (((0,), (0,)), ((), ()))


def _dg(a, b, dims):
    return lax.dot_general(a, b, dims, preferred_element_type=F32)


def _split2(x):
    hi = x.astype(BF16)
    lo = (x - hi.astype(F32)).astype(BF16)
    return hi, lo


def _mm(a, b, dims=_NN, prec=1):
    if prec == 1:
        return _dg(a.astype(BF16), b.astype(BF16), dims)
    ah, al = _split2(a)
    bh, bl = _split2(b)
    return _dg(ah, bh, dims) + (_dg(ah, bl, dims) + _dg(al, bh, dims))


def _headsum(xs, ones_bd):
    return [_dg(x.astype(BF16), ones_bd, _NN) for x in xs]


def _iota(shape, axis):
    return lax.broadcasted_iota(jnp.int32, shape, axis)


def _pair_bd(x, bdmask):
    return jnp.where(bdmask, jnp.concatenate([x, x], axis=0), 0.0)


def _wkv_consts():
    row = _iota((CH, LANES), 0)
    col = _iota((CH, LANES), 1) % CH
    r128 = _iota((LANES, LANES), 0)
    c128 = _iota((LANES, LANES), 1)

    def masks(lower):
        late, early = (row, col) if lower else (col, row)
        return dict(
            strict=early < late, incl=early <= late,
            m_d=(early < late) & ((row // 16) == (col // 16)),
            m_o1=((row // 32) == (col // 32)) & ((late // 16) % 2 == 1) & ((early // 16) % 2 == 0),
            m_o2=((late // 32) == 1) & ((early // 32) == 0))

    return dict(bd=(r128 // CH) == (c128 // CH), eye128=r128 == c128,
                eye_cat=jnp.where(row == col, 1.0, 0.0).astype(F32),
                fwd=masks(True), bwd=masks(False))


def _wkv_mxu_steps(items, cst):
    bdmask = cst["bd"]

    def bd(x):
        return _pair_bd(x, bdmask)

    def mm(a, b, dims=_NN):
        return _mm(a, b, dims, prec=P_WKV)

    st = [dict(it=it, mk=cst["bwd"] if it["rev"] else cst["fwd"]) for it in items]
    for s in st:
        ld = s["it"]["load"]
        amat = mm(jnp.concatenate([ld("at"), ld("rt")], axis=0),
                  jnp.concatenate([bd(ld("bt")), bd(ld("kt"))], axis=0), _NT)
        mk = s["mk"]
        s["n"] = amat[:CH, :LANES]
        s["akrk"] = jnp.concatenate([jnp.where(mk["strict"], amat[:CH, LANES:], 0.0),
                                     jnp.where(mk["incl"], amat[CH:, LANES:], 0.0)], axis=0)
        s["rb"] = jnp.where(mk["incl"], amat[CH:, :LANES], 0.0)
    yield
    for s in st:
        xv = mm(s["akrk"], bd(s["it"]["load"]("v")))
        s["x1"], s["y0v"] = xv[:CH], xv[CH:]
    yield
    for s in st:
        s["p"] = jnp.where(s["mk"]["m_d"], s["n"], 0.0)
        s["t"] = cst["eye_cat"] + s["p"]
        s["q"] = mm(s["p"], bd(s["p"]))
    yield
    for _ in range(2):
        for s in st:
            qt = mm(jnp.concatenate([s["q"], s["t"]], axis=0), bd(s["q"]))
            s["t"] = s["t"] + qt[CH:]
            s["q"] = qt[:CH]
        yield
    for s in st:
        s["t"] = s["t"] + mm(s["t"], bd(s["q"]))
    yield
    for key in ("m_o1", "m_o2"):
        for s in st:
            s["x"] = mm(s["t"], bd(jnp.where(s["mk"][key], s["n"], 0.0)))
        yield
        for s in st:
            s["t"] = s["t"] + mm(s["x"], bd(s["t"]))
        yield
    for s in st:
        au = mm(s["t"], jnp.concatenate([bd(s["it"]["load"]("at")), bd(s["x1"])], axis=1))
        s["ah"], s["u0"] = au[:, :LANES], au[:, LANES:]
    yield
    for s in st:
        ld = s["it"]["load"]
        w_tot = s["it"]["w_tot"]()
        ry = mm(s["rb"], jnp.concatenate([bd(s["ah"]), bd(s["u0"])], axis=1))
        rhs2 = jnp.concatenate([jnp.concatenate([s["ah"], s["u0"]], axis=1),
                                jnp.concatenate([jnp.zeros((CH, LANES), F32), ld("v")], axis=1)], axis=0)
        kb_end = jnp.concatenate([ld("bt") * w_tot, ld("kt") * w_tot], axis=0)
        mg = mm(kb_end, rhs2, _TN)
        m = jnp.where(bdmask, mg[:, :LANES], 0.0) + jnp.where(cst["eye128"], w_tot, 0.0)
        g = jnp.where(bdmask, mg[:, LANES:], 0.0)
        s["it"]["store"](ld("rt") + ry[:, :LANES], ry[:, LANES:] + s["y0v"], m, g)
    yield


def _run(gen):
    for _ in gen:
        pass


def _interleave(main, side, ratio):
    for _ in main:
        for _ in range(ratio):
            next(side, None)
    _run(side)


def _chain(gens):
    for g in gens:
        yield from g


def _sigmoid(x):
    return 0.5 * jnp.tanh(0.5 * x) + 0.5


PAD = 8


def _shifted(pad_ref, start, n, mp, mn):
    cur = pad_ref[pl.ds(start + PAD, n), :]
    prev = pad_ref[pl.ds(start + PAD - 1, n), :]
    nxt = pad_ref[pl.ds(start + PAD + 1, n), :]
    return cur * (1.0 - mp - mn) + mp * prev + mn * nxt


def _chunk_cumsum(x, rev):
    n = x.shape[0]
    sub = 8
    pos = _iota((n, LANES), 0) % CH
    s = 1
    while s < sub:
        if rev:
            x = x + jnp.where(pos < CH - s, pltpu.roll(x, n - s, 0), 0.0)
        else:
            x = x + jnp.where(pos >= s, pltpu.roll(x, s, 0), 0.0)
        s *= 2
    g = CH // sub
    x = x.reshape(n // CH, g, sub, LANES)
    s = 1
    while s < g:
        z = jnp.zeros((n // CH, s, sub, LANES), F32)
        if rev:
            x = x + jnp.concatenate([x[:, s:], z], axis=1)
        else:
            x = x + jnp.concatenate([z, x[:, :-s]], axis=1)
        s *= 2
    return x.reshape(n, LANES)


(V_MPR, V_MPK, V_MPV, V_MNR, V_MNK, V_MNV, V_DBF, V_DBB, V_IBF, V_IBB,
 V_KK, V_KA, V_RK, V_LG, V_LB, V_VB) = range(16)
W_DEC, W_ICL, W_VR = range(3)
I_RT, I_AT, I_KT, I_BT = range(4)
I_NAMES = dict(rt=I_RT, at=I_AT, kt=I_KT, bt=I_BT)


def _wkv_kernel(*refs, seq, first):
    if first:
        (r_ref, k_ref, v_ref, ga_ref, dec_ref, icl_ref, vec_ref, mul_ref, wl_ref,
         ya_ref, vf_out_ref, lhs_s, add_s, yf_s, yb_s, bon_s, pad_s, it_s, v_s, wt_s) = refs
        vres_ref = vf_ref = None
    else:
        (r_ref, k_ref, v_ref, ga_ref, dec_ref, icl_ref, vres_ref, vf_ref, vec_ref, mul_ref, wl_ref,
         ya_ref, lhs_s, add_s, yf_s, yb_s, bon_s, pad_s, it_s, v_s, wt_s) = refs
        vf_out_ref = None
    ns = seq // SLAB
    per = SLAB // CH
    cst = _wkv_consts()
    ones_bd = jnp.where(cst["bd"], 1.0, 0.0).astype(BF16)

    def vec(i):
        return vec_ref[i:i + 1, :]

    shift_in = (r_ref, k_ref, v_ref, dec_ref, icl_ref)
    for a in range(len(shift_in)):
        pad_s[a, 0:PAD, :] = jnp.zeros((PAD, LANES), F32)
        pad_s[a, seq + PAD:seq + 2 * PAD, :] = jnp.zeros((PAD, LANES), F32)

    def stage(sb, _):
        rows = pl.ds(pl.multiple_of(sb * SLAB, SLAB), SLAB)
        dst = pl.ds(pl.multiple_of(sb * SLAB, SLAB) + PAD, SLAB)
        for a, ref in enumerate(shift_in):
            pad_s[a, dst, :] = ref[rows, :]
        return 0

    lax.fori_loop(0, ns, stage, 0)

    def prologue(sb):
        start = sb * SLAB
        rows = slice(start, start + SLAB)
        r = _shifted(pad_s.at[0], start, SLAB, vec(V_MPR), vec(V_MNR))
        yield
        k = _shifted(pad_s.at[1], start, SLAB, vec(V_MPK), vec(V_MNK))
        yield
        v = _shifted(pad_s.at[2], start, SLAB, vec(V_MPV), vec(V_MNV))
        yield
        dec = _shifted(pad_s.at[3], start, SLAB, mul_ref[0:1, :], mul_ref[2:3, :])
        icl = _shifted(pad_s.at[4], start, SLAB, mul_ref[1:2, :], mul_ref[3:4, :])
        yield
        if first:
            vf_out_ref[rows, :] = v
        else:
            mix = _sigmoid(vec(V_VB) + _mm(vres_ref[rows, :], wl_ref[W_VR], prec=P_LORA)[:, :LANES])
            v = v + (vf_ref[rows, :] - v) * mix
        v_s[rows, :] = v
        yield
        zd = _mm(jnp.tanh(dec), wl_ref[W_DEC], prec=P_LORA)
        zi = _mm(icl, wl_ref[W_ICL], prec=P_LORA)
        kk = k * vec(V_KK)
        yield
        kds, ags, lws = [], [], []
        kka = k * vec(V_KA)
        kbase = k - kka
        for d in range(2):
            z = vec(V_DBF + d) + zd[:, d * LANES:(d + 1) * LANES]
            lws.append(-EXP_M_HALF * _sigmoid(z))
            ag = _sigmoid(vec(V_IBF + d) + zi[:, d * LANES:(d + 1) * LANES])
            kds.append(kbase + kka * ag)
            ags.append(ag)
            yield
        ssq, bsum = _headsum([kk * kk, r * (kds[0] + kds[1]) * vec(V_RK)], ones_bd)
        kk = kk * lax.rsqrt(ssq + 1e-12)
        bon_s[rows, :] = bsum * v
        yield
        for d in range(2):
            cum = _chunk_cumsum(lws[d], d == 1)
            yield
            e_inv = jnp.exp(-cum)
            it_s[d, I_RT, rows, :] = r * jnp.exp(cum)
            it_s[d, I_AT, rows, :] = -kk * jnp.exp(cum - lws[d])
            yield
            it_s[d, I_KT, rows, :] = kds[d] * e_inv
            it_s[d, I_BT, rows, :] = (kk * ags[d]) * e_inv
            for i in range(per):
                c = sb * per + i
                row = i * CH if d == 1 else i * CH + CH - 1
                wt_s[d, c * 8:(c + 1) * 8, :] = jnp.broadcast_to(jnp.exp(cum[row:row + 1, :]), (8, LANES))
            yield

    def chunk_items(step):
        buf = step % 2
        items = []
        for d in range(2):
            slab = step if d == 0 else ns - 1 - step
            for i in range(per):
                c = slab * per + i
                crow = slice(c * CH, (c + 1) * CH)

                def load(name, d=d, crow=crow):
                    if name == "v":
                        return v_s[crow, :]
                    return it_s[d, I_NAMES[name], crow, :]

                def w_tot(d=d, c=c):
                    return wt_s[d, c * 8:c * 8 + 1, :]

                def store(rh, y0, m, g, d=d, i=i):
                    lhs_s[buf, d, i, :CH, :] = rh
                    lhs_s[buf, d, i, CH:, :] = m
                    add_s[buf, d, i, :CH, :] = y0
                    add_s[buf, d, i, CH:, :] = g

                items.append(dict(rev=d == 1, load=load, w_tot=w_tot, store=store))
        return items

    state = [jnp.zeros((LANES, LANES), F32), jnp.zeros((LANES, LANES), F32)]

    def scan(step):
        buf = step % 2
        for j in range(per):
            for d in range(2):
                i = j if d == 0 else per - 1 - j
                slab = step if d == 0 else ns - 1 - step
                o = _mm(lhs_s[buf, d, i], state[d], prec=P_WKV) + add_s[buf, d, i]
                rows = slice((slab * per + i) * CH, (slab * per + i + 1) * CH)
                (yf_s if d == 0 else yb_s)[rows, :] = o[:CH]
                state[d] = o[CH:]
            yield

    def post(sb):
        rows = slice(sb * SLAB, (sb + 1) * SLAB)
        y = yf_s[rows, :] + yb_s[rows, :]
        mu = _headsum([y], ones_bd)[0] * (1.0 / HEAD)
        yield
        dv = y - mu
        var = _headsum([dv * dv], ones_bd)[0] * (1.0 / HEAD)
        yield
        out = dv * lax.rsqrt(var + GN_EPS) * vec(V_LG) + vec(V_LB) + bon_s[rows, :]
        ga = ga_ref[rows, :]
        ya_ref[rows, :] = (out * (ga * _sigmoid(ga))).astype(BF16)
        yield

    ends = sorted({0, ns - 1})
    inner = [sb for sb in range(ns) if sb not in ends]
    for sb in ends:
        _run(prologue(sb))
    for step in range(ns):
        side = _chain([prologue(sb) for sb in inner]) if step == 0 else scan(step - 1)
        _interleave(_wkv_mxu_steps(chunk_items(step), cst), side, 3 if step == 0 else 1)
    _interleave(scan(ns - 1), _chain([post(sb) for sb in inner]), 1)
    for sb in ends:
        _run(post(sb))


def _wkv_call(proj3, vfirst, vecs, mul, wl, *, first):
    bsz, seq, _ = proj3.shape
    nc = seq // CH
    per = SLAB // CH

    def col(base):
        blk = base // LANES
        return pl.BlockSpec((None, seq, LANES), lambda b, p, blk=blk: (b, 0, blk + p))

    def fixed(base):
        blk = base // LANES
        return pl.BlockSpec((None, seq, LANES), lambda b, p, blk=blk: (b, 0, blk))

    in_specs = [col(C_R), col(C_K), col(C_V), col(C_GA), fixed(C_DEC), fixed(C_ICL)]
    args = [proj3] * 6
    if not first:
        in_specs += [fixed(C_VRES), pl.BlockSpec((None, seq, LANES), lambda b, p: (b, 0, p))]
        args += [proj3, vfirst]
    in_specs += [pl.BlockSpec((16, LANES), lambda b, p: (0, p)),
                 pl.BlockSpec((8, LANES), lambda b, p: (0, 0)),
                 pl.BlockSpec((None, 3, LANES, 2 * LANES), lambda b, p: (p, 0, 0, 0))]
    args += [vecs, mul, wl]
    out_block = pl.BlockSpec((None, seq, LANES), lambda b, p: (b, 0, p))
    out_sds = jax.ShapeDtypeStruct((bsz, seq, D), BF16)
    out_shape = (out_sds, jax.ShapeDtypeStruct((bsz, seq, D), F32)) if first else out_sds
    out_specs = (out_block, out_block) if first else out_block
    scratch = [pltpu.VMEM((2, 2, per, CH + LANES, LANES), F32),
               pltpu.VMEM((2, 2, per, CH + LANES, LANES), F32),
               pltpu.VMEM((seq, LANES), F32), pltpu.VMEM((seq, LANES), F32), pltpu.VMEM((seq, LANES), F32),
               pltpu.VMEM((5, seq + 2 * PAD, LANES), F32),
               pltpu.VMEM((2, 4, seq, LANES), F32),
               pltpu.VMEM((seq, LANES), F32),
               pltpu.VMEM((2, nc * 8, LANES), F32)]
    return pl.pallas_call(
        functools.partial(_wkv_kernel, seq=seq, first=first),
        grid=(bsz, D // LANES),
        in_specs=in_specs, out_specs=out_specs, out_shape=out_shape,
        scratch_shapes=scratch,
        compiler_params=pltpu.CompilerParams(dimension_semantics=("arbitrary", "arbitrary"),
                                             vmem_limit_bytes=VMEM_LIMIT),
        name="wkv7",
    )(*args)


def _proj_kernel(x_ref, g_ref, w_ref, o_ref, xn_ref):
    @pl.when(pl.program_id(1) == 0)
    def _():
        x = x_ref[...]
        ms = jnp.mean(x * x, axis=-1, keepdims=True)
        xn_ref[...] = (x * lax.rsqrt(ms + NORM_EPS) * g_ref[...]).astype(BF16)

    o_ref[...] = jnp.dot(xn_ref[...], w_ref[...], preferred_element_type=F32)


def _proj_call(x2, gain, w):
    t = x2.shape[0]
    tm = min(1024, t)
    tn = 1920
    return pl.pallas_call(
        _proj_kernel,
        grid=(t // tm, NPAD // tn),
        in_specs=[pl.BlockSpec((tm, D), lambda i, j: (i, 0)),
                  pl.BlockSpec((1, D), lambda i, j: (0, 0)),
                  pl.BlockSpec((D, tn), lambda i, j: (0, j))],
        out_specs=pl.BlockSpec((tm, tn), lambda i, j: (i, j)),
        out_shape=jax.ShapeDtypeStruct((t, NPAD), F32),
        scratch_shapes=[pltpu.VMEM((tm, D), BF16)],
        compiler_params=pltpu.CompilerParams(dimension_semantics=("arbitrary", "arbitrary"),
                                             vmem_limit_bytes=VMEM_LIMIT),
        name="proj",
    )(x2, gain, w)


def _ret_kernel(q_ref, k_ref, v_ref, gb_ref, cos_ref, sin_ref, lg_ref, gain_ref, o_ref, qr_s, kr_s, y_s, *, seq):
    nc = seq // RET_C
    p = pl.program_id(1)
    scale = HEAD ** -0.5

    def rot(c, _):
        sl = pl.ds(pl.multiple_of(c * RET_C, RET_C), RET_C)
        cs, sn = cos_ref[sl, :], sin_ref[sl, :]
        q = q_ref[sl, :]
        k = k_ref[sl, :]
        qr_s[sl, :] = (q * cs + pltpu.roll(q, 64, 1) * sn) * scale
        kr_s[sl, :] = k * cs + pltpu.roll(k, 64, 1) * sn
        return 0

    lax.fori_loop(0, nc, rot, 0)

    lane = _iota((RET_C, LANES), 1)
    rowi = _iota((RET_C, LANES), 0).astype(F32)
    absd = jnp.abs(_iota((RET_C, RET_C), 0) - _iota((RET_C, RET_C), 1)).astype(F32)
    for j in range(2):
        hmask = ((lane // 32) % 2) == j
        lg = lg_ref[pl.ds(2 * p + j, 1), :]
        lg128 = lg[:, :LANES]
        dmat = jnp.exp(absd * lg)
        xi_f = jnp.exp((rowi + 1.0) * lg128)
        ze_f = jnp.exp((RET_C - 1.0 - rowi) * lg128)
        xi_b = jnp.exp((RET_C - rowi) * lg128)
        ze_b = jnp.exp(rowi * lg128)
        cdec = jnp.exp(RET_C * lg128)
        vs = slice(j * LANES, (j + 1) * LANES)
        fst = jnp.zeros((LANES, LANES), F32)
        for i in range(nc):
            sl = slice(i * RET_C, (i + 1) * RET_C)
            qh = jnp.where(hmask, qr_s[sl, :], 0.0)
            kh = kr_s[sl, :]
            vh = v_ref[sl, vs]
            sc = _mm(qh, kh, _NT) * dmat
            y = _mm(sc, vh)
            if i > 0:
                y = y + _mm(qh * xi_f, fst)
            if i < nc - 1:
                fst = fst * cdec + _mm(jnp.where(hmask, kh, 0.0) * ze_f, vh, _TN)
            y_s[sl, vs] = y
        gst = jnp.zeros((LANES, LANES), F32)
        for i in range(nc - 1, -1, -1):
            sl = slice(i * RET_C, (i + 1) * RET_C)
            y = y_s[sl, vs]
            if i < nc - 1:
                qh = jnp.where(hmask, qr_s[sl, :], 0.0)
                y = y + _mm(qh * xi_b, gst)
            if i > 0:
                kh = jnp.where(hmask, kr_s[sl, :], 0.0)
                gst = gst * cdec + _mm(kh * ze_b, v_ref[sl, vs], _TN)
            mu = jnp.mean(y, axis=-1, keepdims=True)
            dv = y - mu
            var = jnp.mean(dv * dv, axis=-1, keepdims=True)
            out = dv * lax.rsqrt(var + NORM_EPS) * gain_ref[:, vs]
            o_ref[sl, vs] = (out * jax.nn.silu(gb_ref[sl, vs])).astype(BF16)


def _ret_call(proj3, cos_t, sin_t, lg_t, gain):
    bsz, seq, _ = proj3.shape
    w2 = 2 * LANES

    def spec(base, width):
        blk = base // width
        return pl.BlockSpec((None, seq, width), lambda b, p, blk=blk: (b, 0, blk + p))

    return pl.pallas_call(
        functools.partial(_ret_kernel, seq=seq),
        grid=(bsz, RET_HEADS // 2),
        in_specs=[spec(C_QB, LANES), spec(C_KB, LANES), spec(C_VB, w2), spec(C_GB, w2),
                  pl.BlockSpec((seq, LANES), lambda b, p: (0, 0)),
                  pl.BlockSpec((seq, LANES), lambda b, p: (0, 0)),
                  pl.BlockSpec((RET_HEADS, w2), lambda b, p: (0, 0)),
                  pl.BlockSpec((1, w2), lambda b, p: (0, p))],
        out_specs=pl.BlockSpec((None, seq, w2), lambda b, p: (b, 0, p)),
        out_shape=jax.ShapeDtypeStruct((bsz, seq, D), BF16),
        scratch_shapes=[pltpu.VMEM((seq, LANES), F32), pltpu.VMEM((seq, LANES), F32), pltpu.VMEM((seq, w2), F32)],
        compiler_params=pltpu.CompilerParams(dimension_semantics=("arbitrary", "arbitrary"),
                                             vmem_limit_bytes=VMEM_LIMIT),
        name="retention",
    )(proj3, proj3, proj3, proj3, cos_t, sin_t, lg_t, gain)


def _merge_kernel(x_ref, ya_ref, yb_ref, ma_ref, mb_ref, wa_ref, wb_ref, wo_ref, o_ref):
    za = jnp.dot(ya_ref[...], wa_ref[...], preferred_element_type=F32)
    zb = jnp.dot(yb_ref[...], wb_ref[...], preferred_element_type=F32)
    merged = jax.nn.sigmoid(ma_ref[...]) * za + jax.nn.sigmoid(mb_ref[...]) * zb
    o_ref[...] = x_ref[...] + jnp.dot(merged.astype(BF16), wo_ref[...], preferred_element_type=F32)


def _merge_call(x2, ya2, yb2, proj2, wa, wb, wo):
    t = x2.shape[0]
    tm = min(512, t)
    row = pl.BlockSpec((tm, D), lambda i: (i, 0))
    wspec = pl.BlockSpec((D, D), lambda i: (0, 0))
    return pl.pallas_call(
        _merge_kernel,
        grid=(t // tm,),
        in_specs=[row, row, row,
                  pl.BlockSpec((tm, D), lambda i: (i, C_MA // D)),
                  pl.BlockSpec((tm, D), lambda i: (i, C_MB // D)),
                  wspec, wspec, wspec],
        out_specs=row,
        out_shape=jax.ShapeDtypeStruct((t, D), F32),
        compiler_params=pltpu.CompilerParams(dimension_semantics=("arbitrary",),
                                             vmem_limit_bytes=VMEM_LIMIT),
        name="merge",
    )(x2, ya2, yb2, proj2, proj2, wa, wb, wo)


def _final_kernel(x_ref, g_ref, o_ref):
    x = x_ref[...]
    ms = jnp.mean(x * x, axis=-1, keepdims=True)
    o_ref[...] = x * lax.rsqrt(ms + NORM_EPS) * g_ref[...]


def _final_call(x2, gain):
    t = x2.shape[0]
    tm = min(1024, t)
    return pl.pallas_call(
        _final_kernel,
        grid=(t // tm,),
        in_specs=[pl.BlockSpec((tm, D), lambda i: (i, 0)), pl.BlockSpec((1, D), lambda i: (0, 0))],
        out_specs=pl.BlockSpec((tm, D), lambda i: (i, 0)),
        out_shape=jax.ShapeDtypeStruct((t, D), F32),
        compiler_params=pltpu.CompilerParams(dimension_semantics=("arbitrary",)),
        name="final_norm",
    )(x2, gain)


def _qk_perm():
    idx = []
    for pair in range(RET_HEADS // 2):
        for lane in range(LANES):
            seg, i = lane // 32, lane % 32
            head, half = 2 * pair + seg % 2, seg // 2
            idx.append(head * HEAD + half * 32 + i)
    return np.asarray(idx, np.int32)


def _pack_w_in(w_in, w_vres_down):
    depth = w_in.shape[0]
    perm = _qk_perm()
    vres = jnp.concatenate([jnp.zeros((1, D, 32), F32), w_vres_down], axis=0)
    vres = jnp.pad(vres, ((0, 0), (0, 0), (0, LANES - 32)))
    parts = [w_in[:, :, 0:3072], w_in[:, :, 3328:4352], w_in[:, :, 5376:6400], w_in[:, :, 6400:7424],
             w_in[:, :, 7424:8448], w_in[:, :, 8448:9472],
             w_in[:, :, 4352:4864][:, :, perm], w_in[:, :, 4864:5376][:, :, perm],
             w_in[:, :, 3072:3328], vres]
    w = jnp.concatenate(parts, axis=2)
    assert w.shape == (depth, D, NPAD)
    return w.astype(BF16)


def _pair_cat(wf, wb):
    pairs = D // LANES
    cat = jnp.concatenate([wf.reshape(LANES, pairs, LANES), wb.reshape(LANES, pairs, LANES)], axis=2)
    return cat.transpose(1, 0, 2)


def _rope_tables(seq):
    half = 32
    freqs = jnp.power(ROPE_BASE, -jnp.arange(half, dtype=F32) / half)
    ang = jnp.arange(seq, dtype=F32)[:, None] * freqs[None, :]
    cos = jnp.tile(jnp.cos(ang), (1, 4))
    sin = jnp.tile(jnp.sin(ang), (1, 4))
    sign = jnp.where(jnp.arange(LANES) < 64, -1.0, 1.0).astype(F32)
    return cos, sin * sign[None, :]


def kernel(x, norm_gain, w_in, w_vres_down, shift_prev, shift_next, w_decay_up, decay_bias, w_iclr_up, iclr_bias,
           w_vres_up, vres_bias, k_k, k_a, r_k, lnx_gain, lnx_bias, w_branch_a, ret_norm_gain, w_branch_b, w_out,
           final_gain):
    bsz, seq, _ = x.shape
    depth = w_in.shape[0]
    assert seq % RET_C == 0 and seq % SLAB == 0 and SLAB % CH == 0
    t = bsz * seq

    w_all = _pack_w_in(w_in, w_vres_down)
    wa_all, wb_all, wo_all = (w.astype(BF16) for w in (w_branch_a, w_branch_b, w_out))
    cos_t, sin_t = _rope_tables(seq)
    log_g = jnp.log1p(-jnp.exp2(-5.0 - jnp.arange(RET_HEADS, dtype=F32)))
    lg_t = jnp.broadcast_to(log_g[:, None], (RET_HEADS, 2 * LANES))
    zpad = jnp.zeros((64, D), F32)
    zmat = jnp.zeros((LANES, D), F32)

    x2 = x.reshape(t, D)
    vfirst = None
    for l in range(depth):
        proj2 = _proj_call(x2, norm_gain[l][None, :], w_all[l])
        proj3 = proj2.reshape(bsz, seq, NPAD)

        vb = vres_bias[l - 1] if l > 0 else jnp.zeros((D,), F32)
        vecs = jnp.stack([shift_prev[l, 0:1024], shift_prev[l, 1024:2048], shift_prev[l, 2048:3072],
                          shift_next[l, 0:1024], shift_next[l, 1024:2048], shift_next[l, 2048:3072],
                          decay_bias[l, 0], decay_bias[l, 1], iclr_bias[l, 0], iclr_bias[l, 1],
                          k_k[l], k_a[l], r_k[l].reshape(D), lnx_gain[l], lnx_bias[l], vb], axis=0)
        mul = jnp.concatenate([jnp.stack([shift_prev[l, 3072:3200], shift_prev[l, 3200:3328],
                                          shift_next[l, 3072:3200], shift_next[l, 3200:3328]], axis=0),
                               jnp.zeros((4, LANES), F32)], axis=0)
        wvr = jnp.pad(w_vres_up[l - 1], ((0, LANES - 32), (0, 0))) if l > 0 else zmat
        wl = jnp.stack([_pair_cat(jnp.concatenate([w_decay_up[l, 0], zpad], axis=0),
                                  jnp.concatenate([zpad, w_decay_up[l, 1]], axis=0)),
                        _pair_cat(jnp.concatenate([w_iclr_up[l, 0], zpad], axis=0),
                                  jnp.concatenate([zpad, w_iclr_up[l, 1]], axis=0)),
                        _pair_cat(wvr, zmat)], axis=1)

        if l == 0:
            ya, vfirst = _wkv_call(proj3, None, vecs, mul, wl, first=True)
        else:
            ya = _wkv_call(proj3, vfirst, vecs, mul, wl, first=False)
        yb = _ret_call(proj3, cos_t, sin_t, lg_t, ret_norm_gain[l][None, :])
        x2 = _merge_call(x2, ya.reshape(t, D), yb.reshape(t, D), proj2, wa_all[l], wb_all[l], wo_all[l])

    return _final_call(x2, final_gain[None, :]).reshape(bsz, seq, D)
```

```python
import functools

import numpy as np
import jax
import jax.numpy as jnp
from jax import lax
from jax.experimental import pallas as pl
from jax.experimental.pallas import tpu as pltpu

F32 = jnp.float32
BF16 = jnp.bfloat16

D = 1024
HEAD = 64
LANES = 128
CH = 64
SLAB = 512
RET_HEADS = 8
RET_C = 256
NORM_EPS = 1e-6
GN_EPS = 64e-5
EXP_M_HALF = 0.6065306597126334
ROPE_BASE = 10000.0
VMEM_LIMIT = 56 * 1024 * 1024

C_R, C_K, C_V, C_GA, C_VB, C_GB, C_MA, C_MB = (i * 1024 for i in range(8))
C_QB, C_KB, C_DEC, C_ICL, C_VRES, NPAD = 8192, 8704, 9216, 9344, 9472, 9600

P_LORA = 1
P_WKV = 1

_NN = (((1,), (0,)), ((), ()))
_NT = (((1,), (1,)), ((), ()))
_TN = (((0,), (0,)), ((), ()))


def _dg(a, b, dims):
    return lax.dot_general(a, b, dims, preferred_element_type=F32)


def _split2(x):
    hi = x.astype(BF16)
    lo = (x - hi.astype(F32)).astype(BF16)
    return hi, lo


def _mm(a, b, dims=_NN, prec=1):
    if prec == 1:
        return _dg(a.astype(BF16), b.astype(BF16), dims)
    ah, al = _split2(a)
    bh, bl = _split2(b)
    return _dg(ah, bh, dims) + (_dg(ah, bl, dims) + _dg(al, bh, dims))


def _headsum(xs, ones_bd):
    return [_dg(x.astype(BF16), ones_bd, _NN) for x in xs]


def _iota(shape, axis):
    return lax.broadcasted_iota(jnp.int32, shape, axis)


def _pair_bd(x, bdmask):
    return jnp.where(bdmask, jnp.concatenate([x, x], axis=0), 0.0)


def _wkv_consts():
    row = _iota((CH, LANES), 0)
    col = _iota((CH, LANES), 1) % CH
    r128 = _iota((LANES, LANES), 0)
    c128 = _iota((LANES, LANES), 1)

    def masks(lower):
        late, early = (row, col) if lower else (col, row)
        return dict(
            strict=early < late, incl=early <= late,
            m_d=(early < late) & ((row // 16) == (col // 16)),
            m_o1=((row // 32) == (col // 32)) & ((late // 16) % 2 == 1) & ((early // 16) % 2 == 0),
            m_o2=((late // 32) == 1) & ((early // 32) == 0))

    return dict(bd=(r128 // CH) == (c128 // CH), eye128=r128 == c128,
                eye_cat=jnp.where(row == col, 1.0, 0.0).astype(F32),
                fwd=masks(True), bwd=masks(False))


def _wkv_mxu_steps(items, cst):
    bdmask = cst["bd"]

    def bd(x):
        return _pair_bd(x, bdmask)

    def mm(a, b, dims=_NN):
        return _mm(a, b, dims, prec=P_WKV)

    st = [dict(it=it, mk=cst["bwd"] if it["rev"] else cst["fwd"]) for it in items]
    for s in st:
        ld = s["it"]["load"]
        amat = mm(jnp.concatenate([ld("at"), ld("rt")], axis=0),
                  jnp.concatenate([bd(ld("bt")), bd(ld("kt"))], axis=0), _NT)
        mk = s["mk"]
        s["n"] = amat[:CH, :LANES]
        s["akrk"] = jnp.concatenate([jnp.where(mk["strict"], amat[:CH, LANES:], 0.0),
                                     jnp.where(mk["incl"], amat[CH:, LANES:], 0.0)], axis=0)
        s["rb"] = jnp.where(mk["incl"], amat[CH:, :LANES], 0.0)
    yield
    for s in st:
        xv = mm(s["akrk"], bd(s["it"]["load"]("v")))
        s["x1"], s["y0v"] = xv[:CH], xv[CH:]
    yield
    for s in st:
        s["p"] = jnp.where(s["mk"]["m_d"], s["n"], 0.0)
        s["t"] = cst["eye_cat"] + s["p"]
        s["q"] = mm(s["p"], bd(s["p"]))
    yield
    for _ in range(2):
        for s in st:
            qt = mm(jnp.concatenate([s["q"], s["t"]], axis=0), bd(s["q"]))
            s["t"] = s["t"] + qt[CH:]
            s["q"] = qt[:CH]
        yield
    for s in st:
        s["t"] = s["t"] + mm(s["t"], bd(s["q"]))
    yield
    for key in ("m_o1", "m_o2"):
        for s in st:
            s["x"] = mm(s["t"], bd(jnp.where(s["mk"][key], s["n"], 0.0)))
        yield
        for s in st:
            s["t"] = s["t"] + mm(s["x"], bd(s["t"]))
        yield
    for s in st:
        au = mm(s["t"], jnp.concatenate([bd(s["it"]["load"]("at")), bd(s["x1"])], axis=1))
        s["ah"], s["u0"] = au[:, :LANES], au[:, LANES:]
    yield
    for s in st:
        ld = s["it"]["load"]
        w_tot = s["it"]["w_tot"]()
        ry = mm(s["rb"], jnp.concatenate([bd(s["ah"]), bd(s["u0"])], axis=1))
        rhs2 = jnp.concatenate([jnp.concatenate([s["ah"], s["u0"]], axis=1),
                                jnp.concatenate([jnp.zeros((CH, LANES), F32), ld("v")], axis=1)], axis=0)
        kb_end = jnp.concatenate([ld("bt") * w_tot, ld("kt") * w_tot], axis=0)
        mg = mm(kb_end, rhs2, _TN)
        m = jnp.where(bdmask, mg[:, :LANES], 0.0) + jnp.where(cst["eye128"], w_tot, 0.0)
        g = jnp.where(bdmask, mg[:, LANES:], 0.0)
        s["it"]["store"](ld("rt") + ry[:, :LANES], ry[:, LANES:] + s["y0v"], m, g)
    yield


def _run(gen):
    for _ in gen:
        pass


def _interleave(main, side, ratio):
    for _ in main:
        for _ in range(ratio):
            next(side, None)
    _run(side)


def _chain(gens):
    for g in gens:
        yield from g


def _sigmoid(x):
    return 0.5 * jnp.tanh(0.5 * x) + 0.5


PAD = 8


def _shifted(pad_ref, start, n, mp, mn):
    cur = pad_ref[pl.ds(start + PAD, n), :]
    prev = pad_ref[pl.ds(start + PAD - 1, n), :]
    nxt = pad_ref[pl.ds(start + PAD + 1, n), :]
    return cur * (1.0 - mp - mn) + mp * prev + mn * nxt


def _chunk_cumsum(x, rev):
    n = x.shape[0]
    sub = 8
    pos = _iota((n, LANES), 0) % CH
    s = 1
    while s < sub:
        if rev:
            x = x + jnp.where(pos < CH - s, pltpu.roll(x, n - s, 0), 0.0)
        else:
            x = x + jnp.where(pos >= s, pltpu.roll(x, s, 0), 0.0)
        s *= 2
    g = CH // sub
    x = x.reshape(n // CH, g, sub, LANES)
    s = 1
    while s < g:
        z = jnp.zeros((n // CH, s, sub, LANES), F32)
        if rev:
            x = x + jnp.concatenate([x[:, s:], z], axis=1)
        else:
            x = x + jnp.concatenate([z, x[:, :-s]], axis=1)
        s *= 2
    return x.reshape(n, LANES)


(V_MPR, V_MPK, V_MPV, V_MNR, V_MNK, V_MNV, V_DBF, V_DBB, V_IBF, V_IBB,
 V_KK, V_KA, V_RK, V_LG, V_LB, V_VB) = range(16)
W_DEC, W_ICL, W_VR = range(3)
I_RT, I_AT, I_KT, I_BT = range(4)
I_NAMES = dict(rt=I_RT, at=I_AT, kt=I_KT, bt=I_BT)


def _wkv_kernel(*refs, seq, first):
    if first:
        (r_ref, k_ref, v_ref, ga_ref, dec_ref, icl_ref, vec_ref, mul_ref, wl_ref,
         ya_ref, vf_out_ref, lhs_s, add_s, yf_s, yb_s, bon_s, pad_s, it_s, v_s, wt_s) = refs
        vres_ref = vf_ref = None
    else:
        (r_ref, k_ref, v_ref, ga_ref, dec_ref, icl_ref, vres_ref, vf_ref, vec_ref, mul_ref, wl_ref,
         ya_ref, lhs_s, add_s, yf_s, yb_s, bon_s, pad_s, it_s, v_s, wt_s) = refs
        vf_out_ref = None
    ns = seq // SLAB
    per = SLAB // CH
    cst = _wkv_consts()
    ones_bd = jnp.where(cst["bd"], 1.0, 0.0).astype(BF16)

    def vec(i):
        return vec_ref[i:i + 1, :]

    shift_in = (r_ref, k_ref, v_ref, dec_ref, icl_ref)
    for a in range(len(shift_in)):
        pad_s[a, 0:PAD, :] = jnp.zeros((PAD, LANES), F32)
        pad_s[a, seq + PAD:seq + 2 * PAD, :] = jnp.zeros((PAD, LANES), F32)

    def stage(sb, _):
        rows = pl.ds(pl.multiple_of(sb * SLAB, SLAB), SLAB)
        dst = pl.ds(pl.multiple_of(sb * SLAB, SLAB) + PAD, SLAB)
        for a, ref in enumerate(shift_in):
            pad_s[a, dst, :] = ref[rows, :]
        return 0

    lax.fori_loop(0, ns, stage, 0)

    def prologue(sb):
        start = sb * SLAB
        rows = slice(start, start + SLAB)
        r = _shifted(pad_s.at[0], start, SLAB, vec(V_MPR), vec(V_MNR))
        yield
        k = _shifted(pad_s.at[1], start, SLAB, vec(V_MPK), vec(V_MNK))
        yield
        v = _shifted(pad_s.at[2], start, SLAB, vec(V_MPV), vec(V_MNV))
        yield
        dec = _shifted(pad_s.at[3], start, SLAB, mul_ref[0:1, :], mul_ref[2:3, :])
        icl = _shifted(pad_s.at[4], start, SLAB, mul_ref[1:2, :], mul_ref[3:4, :])
        yield
        if first:
            vf_out_ref[rows, :] = v
        else:
            mix = _sigmoid(vec(V_VB) + _mm(vres_ref[rows, :], wl_ref[W_VR], prec=P_LORA)[:, :LANES])
            v = v + (vf_ref[rows, :] - v) * mix
        v_s[rows, :] = v
        yield
        zd = _mm(jnp.tanh(dec), wl_ref[W_DEC], prec=P_LORA)
        zi = _mm(icl, wl_ref[W_ICL], prec=P_LORA)
        kk = k * vec(V_KK)
        yield
        kds, ags, lws = [], [], []
        kka = k * vec(V_KA)
        kbase = k - kka
        for d in range(2):
            z = vec(V_DBF + d) + zd[:, d * LANES:(d + 1) * LANES]
            lws.append(-EXP_M_HALF * _sigmoid(z))
            ag = _sigmoid(vec(V_IBF + d) + zi[:, d * LANES:(d + 1) * LANES])
            kds.append(kbase + kka * ag)
            ags.append(ag)
            yield
        ssq, bsum = _headsum([kk * kk, r * (kds[0] + kds[1]) * vec(V_RK)], ones_bd)
        kk = kk * lax.rsqrt(ssq + 1e-12)
        bon_s[rows, :] = bsum * v
        yield
        for d in range(2):
            cum = _chunk_cumsum(lws[d], d == 1)
            yield
            e_inv = jnp.exp(-cum)
            it_s[d, I_RT, rows, :] = r * jnp.exp(cum)
            it_s[d, I_AT, rows, :] = -kk * jnp.exp(cum - lws[d])
            yield
            it_s[d, I_KT, rows, :] = kds[d] * e_inv
            it_s[d, I_BT, rows, :] = (kk * ags[d]) * e_inv
            for i in range(per):
                c = sb * per + i
                row = i * CH if d == 1 else i * CH + CH - 1
                wt_s[d, c * 8:(c + 1) * 8, :] = jnp.broadcast_to(jnp.exp(cum[row:row + 1, :]), (8, LANES))
            yield

    def chunk_items(step):
        buf = step % 2
        items = []
        for d in range(2):
            slab = step if d == 0 else ns - 1 - step
            for i in range(per):
                c = slab * per + i
                crow = slice(c * CH, (c + 1) * CH)

                def load(name, d=d, crow=crow):
                    if name == "v":
                        return v_s[crow, :]
                    return it_s[d, I_NAMES[name], crow, :]

                def w_tot(d=d, c=c):
                    return wt_s[d, c * 8:c * 8 + 1, :]

                def store(rh, y0, m, g, d=d, i=i):
                    lhs_s[buf, d, i, :CH, :] = rh
                    lhs_s[buf, d, i, CH:, :] = m
                    add_s[buf, d, i, :CH, :] = y0
                    add_s[buf, d, i, CH:, :] = g

                items.append(dict(rev=d == 1, load=load, w_tot=w_tot, store=store))
        return items

    state = [jnp.zeros((LANES, LANES), F32), jnp.zeros((LANES, LANES), F32)]

    def scan(step):
        buf = step % 2
        for j in range(per):
            for d in range(2):
                i = j if d == 0 else per - 1 - j
                slab = step if d == 0 else ns - 1 - step
                o = _mm(lhs_s[buf, d, i], state[d], prec=P_WKV) + add_s[buf, d, i]
                rows = slice((slab * per + i) * CH, (slab * per + i + 1) * CH)
                (yf_s if d == 0 else yb_s)[rows, :] = o[:CH]
                state[d] = o[CH:]
            yield

    def post(sb):
        rows = slice(sb * SLAB, (sb + 1) * SLAB)
        y = yf_s[rows, :] + yb_s[rows, :]
        mu = _headsum([y], ones_bd)[0] * (1.0 / HEAD)
        yield
        dv = y - mu
        var = _headsum([dv * dv], ones_bd)[0] * (1.0 / HEAD)
        yield
        out = dv * lax.rsqrt(var + GN_EPS) * vec(V_LG) + vec(V_LB) + bon_s[rows, :]
        ga = ga_ref[rows, :]
        ya_ref[rows, :] = (out * (ga * _sigmoid(ga))).astype(BF16)
        yield

    ends = sorted({0, ns - 1})
    inner = [sb for sb in range(ns) if sb not in ends]
    for sb in ends:
        _run(prologue(sb))
    for step in range(ns):
        side = _chain([prologue(sb) for sb in inner]) if step == 0 else scan(step - 1)
        _interleave(_wkv_mxu_steps(chunk_items(step), cst), side, 3 if step == 0 else 1)
    _interleave(scan(ns - 1), _chain([post(sb) for sb in inner]), 1)
    for sb in ends:
        _run(post(sb))


def _wkv_call(proj3, vfirst, vecs, mul, wl, *, first):
    bsz, seq, _ = proj3.shape
    nc = seq // CH
    per = SLAB // CH

    def col(base):
        blk = base // LANES
        return pl.BlockSpec((None, seq, LANES), lambda b, p, blk=blk: (b, 0, blk + p))

    def fixed(base):
        blk = base // LANES
        return pl.BlockSpec((None, seq, LANES), lambda b, p, blk=blk: (b, 0, blk))

    in_specs = [col(C_R), col(C_K), col(C_V), col(C_GA), fixed(C_DEC), fixed(C_ICL)]
    args = [proj3] * 6
    if not first:
        in_specs += [fixed(C_VRES), pl.BlockSpec((None, seq, LANES), lambda b, p: (b, 0, p))]
        args += [proj3, vfirst]
    in_specs += [pl.BlockSpec((16, LANES), lambda b, p: (0, p)),
                 pl.BlockSpec((8, LANES), lambda b, p: (0, 0)),
                 pl.BlockSpec((None, 3, LANES, 2 * LANES), lambda b, p: (p, 0, 0, 0))]
    args += [vecs, mul, wl]
    out_block = pl.BlockSpec((None, seq, LANES), lambda b, p: (b, 0, p))
    out_sds = jax.ShapeDtypeStruct((bsz, seq, D), BF16)
    out_shape = (out_sds, jax.ShapeDtypeStruct((bsz, seq, D), F32)) if first else out_sds
    out_specs = (out_block, out_block) if first else out_block
    scratch = [pltpu.VMEM((2, 2, per, CH + LANES, LANES), F32),
               pltpu.VMEM((2, 2, per, CH + LANES, LANES), F32),
               pltpu.VMEM((seq, LANES), F32), pltpu.VMEM((seq, LANES), F32), pltpu.VMEM((seq, LANES), F32),
               pltpu.VMEM((5, seq + 2 * PAD, LANES), F32),
               pltpu.VMEM((2, 4, seq, LANES), F32),
               pltpu.VMEM((seq, LANES), F32),
               pltpu.VMEM((2, nc * 8, LANES), F32)]
    return pl.pallas_call(
        functools.partial(_wkv_kernel, seq=seq, first=first),
        grid=(bsz, D // LANES),
        in_specs=in_specs, out_specs=out_specs, out_shape=out_shape,
        scratch_shapes=scratch,
        compiler_params=pltpu.CompilerParams(dimension_semantics=("arbitrary", "arbitrary"),
                                             vmem_limit_bytes=VMEM_LIMIT),
        name="wkv7",
    )(*args)


def _rms(x, g):
    ms = jnp.mean(x * x, axis=-1, keepdims=True)
    return x * lax.rsqrt(ms + NORM_EPS) * g


def _proj_kernel(xn_ref, w_ref, o_ref):
    o_ref[...] = jnp.dot(xn_ref[...], w_ref[...], preferred_element_type=F32)


def _proj_call(xn2, w):
    t = xn2.shape[0]
    tm = min(1024, t)
    tn = 1920
    return pl.pallas_call(
        _proj_kernel,
        grid=(t // tm, NPAD // tn),
        in_specs=[pl.BlockSpec((tm, D), lambda i, j: (i, 0)),
                  pl.BlockSpec((D, tn), lambda i, j: (0, j))],
        out_specs=pl.BlockSpec((tm, tn), lambda i, j: (i, j)),
        out_shape=jax.ShapeDtypeStruct((t, NPAD), F32),
        compiler_params=pltpu.CompilerParams(dimension_semantics=("arbitrary", "arbitrary"),
                                             vmem_limit_bytes=VMEM_LIMIT),
        name="proj",
    )(xn2, w)


def _ret_kernel(q_ref, k_ref, v_ref, gb_ref, cos_ref, sin_ref, lg_ref, gain_ref, o_ref, qr_s, kr_s, y_s, *, seq):
    nc = seq // RET_C
    p = pl.program_id(1)
    scale = HEAD ** -0.5

    def rot(c, _):
        sl = pl.ds(pl.multiple_of(c * RET_C, RET_C), RET_C)
        cs, sn = cos_ref[sl, :], sin_ref[sl, :]
        q = q_ref[sl, :]
        k = k_ref[sl, :]
        qr_s[sl, :] = (q * cs + pltpu.roll(q, 64, 1) * sn) * scale
        kr_s[sl, :] = k * cs + pltpu.roll(k, 64, 1) * sn
        return 0

    lax.fori_loop(0, nc, rot, 0)

    lane = _iota((RET_C, LANES), 1)
    rowi = _iota((RET_C, LANES), 0).astype(F32)
    absd = jnp.abs(_iota((RET_C, RET_C), 0) - _iota((RET_C, RET_C), 1)).astype(F32)
    for j in range(2):
        hmask = ((lane // 32) % 2) == j
        lg = lg_ref[pl.ds(2 * p + j, 1), :]
        lg128 = lg[:, :LANES]
        dmat = jnp.exp(absd * lg)
        xi_f = jnp.exp((rowi + 1.0) * lg128)
        ze_f = jnp.exp((RET_C - 1.0 - rowi) * lg128)
        xi_b = jnp.exp((RET_C - rowi) * lg128)
        ze_b = jnp.exp(rowi * lg128)
        cdec = jnp.exp(RET_C * lg128)
        vs = slice(j * LANES, (j + 1) * LANES)
        fst = jnp.zeros((LANES, LANES), F32)
        for i in range(nc):
            sl = slice(i * RET_C, (i + 1) * RET_C)
            qh = jnp.where(hmask, qr_s[sl, :], 0.0)
            kh = kr_s[sl, :]
            vh = v_ref[sl, vs]
            sc = _mm(qh, kh, _NT) * dmat
            y = _mm(sc, vh)
            if i > 0:
                y = y + _mm(qh * xi_f, fst)
            if i < nc - 1:
                fst = fst * cdec + _mm(jnp.where(hmask, kh, 0.0) * ze_f, vh, _TN)
            y_s[sl, vs] = y
        gst = jnp.zeros((LANES, LANES), F32)
        for i in range(nc - 1, -1, -1):
            sl = slice(i * RET_C, (i + 1) * RET_C)
            y = y_s[sl, vs]
            if i < nc - 1:
                qh = jnp.where(hmask, qr_s[sl, :], 0.0)
                y = y + _mm(qh * xi_b, gst)
            if i > 0:
                kh = jnp.where(hmask, kr_s[sl, :], 0.0)
                gst = gst * cdec + _mm(kh * ze_b, v_ref[sl, vs], _TN)
            mu = jnp.mean(y, axis=-1, keepdims=True)
            dv = y - mu
            var = jnp.mean(dv * dv, axis=-1, keepdims=True)
            out = dv * lax.rsqrt(var + NORM_EPS) * gain_ref[:, vs]
            o_ref[sl, vs] = (out * jax.nn.silu(gb_ref[sl, vs])).astype(BF16)


def _ret_call(proj3, cos_t, sin_t, lg_t, gain):
    bsz, seq, _ = proj3.shape
    w2 = 2 * LANES

    def spec(base, width):
        blk = base // width
        return pl.BlockSpec((None, seq, width), lambda b, p, blk=blk: (b, 0, blk + p))

    return pl.pallas_call(
        functools.partial(_ret_kernel, seq=seq),
        grid=(bsz, RET_HEADS // 2),
        in_specs=[spec(C_QB, LANES), spec(C_KB, LANES), spec(C_VB, w2), spec(C_GB, w2),
                  pl.BlockSpec((seq, LANES), lambda b, p: (0, 0)),
                  pl.BlockSpec((seq, LANES), lambda b, p: (0, 0)),
                  pl.BlockSpec((RET_HEADS, w2), lambda b, p: (0, 0)),
                  pl.BlockSpec((1, w2), lambda b, p: (0, p))],
        out_specs=pl.BlockSpec((None, seq, w2), lambda b, p: (b, 0, p)),
        out_shape=jax.ShapeDtypeStruct((bsz, seq, D), BF16),
        scratch_shapes=[pltpu.VMEM((seq, LANES), F32), pltpu.VMEM((seq, LANES), F32), pltpu.VMEM((seq, w2), F32)],
        compiler_params=pltpu.CompilerParams(dimension_semantics=("arbitrary", "arbitrary"),
                                             vmem_limit_bytes=VMEM_LIMIT),
        name="retention",
    )(proj3, proj3, proj3, proj3, cos_t, sin_t, lg_t, gain)


def _merge_kernel(x_ref, ya_ref, yb_ref, ma_ref, mb_ref, wa_ref, wb_ref, wo_ref, g_ref, *o_refs, last):
    za = jnp.dot(ya_ref[...], wa_ref[...], preferred_element_type=F32)
    zb = jnp.dot(yb_ref[...], wb_ref[...], preferred_element_type=F32)
    merged = jax.nn.sigmoid(ma_ref[...]) * za + jax.nn.sigmoid(mb_ref[...]) * zb
    x = x_ref[...] + jnp.dot(merged.astype(BF16), wo_ref[...], preferred_element_type=F32)
    if last:
        o_refs[0][...] = _rms(x, g_ref[...])
    else:
        o_refs[0][...] = x
        o_refs[1][...] = _rms(x, g_ref[...]).astype(BF16)


def _merge_call(x2, ya2, yb2, proj2, wa, wb, wo, gain, *, last):
    t = x2.shape[0]
    tm = min(512, t)
    row = pl.BlockSpec((tm, D), lambda i: (i, 0))
    wspec = pl.BlockSpec((D, D), lambda i: (0, 0))
    x_sds = jax.ShapeDtypeStruct((t, D), F32)
    return pl.pallas_call(
        functools.partial(_merge_kernel, last=last),
        grid=(t // tm,),
        in_specs=[row, row, row,
                  pl.BlockSpec((tm, D), lambda i: (i, C_MA // D)),
                  pl.BlockSpec((tm, D), lambda i: (i, C_MB // D)),
                  wspec, wspec, wspec, pl.BlockSpec((1, D), lambda i: (0, 0))],
        out_specs=row if last else (row, row),
        out_shape=x_sds if last else (x_sds, jax.ShapeDtypeStruct((t, D), BF16)),
        compiler_params=pltpu.CompilerParams(dimension_semantics=("arbitrary",),
                                             vmem_limit_bytes=VMEM_LIMIT),
        name="merge",
    )(x2, ya2, yb2, proj2, proj2, wa, wb, wo, gain)


def _norm_kernel(x_ref, g_ref, o_ref):
    o_ref[...] = _rms(x_ref[...], g_ref[...]).astype(BF16)


def _norm_call(x2, gain):
    t = x2.shape[0]
    tm = min(1024, t)
    return pl.pallas_call(
        _norm_kernel,
        grid=(t // tm,),
        in_specs=[pl.BlockSpec((tm, D), lambda i: (i, 0)), pl.BlockSpec((1, D), lambda i: (0, 0))],
        out_specs=pl.BlockSpec((tm, D), lambda i: (i, 0)),
        out_shape=jax.ShapeDtypeStruct((t, D), BF16),
        compiler_params=pltpu.CompilerParams(dimension_semantics=("arbitrary",)),
        name="first_norm",
    )(x2, gain)


def _qk_perm():
    idx = []
    for pair in range(RET_HEADS // 2):
        for lane in range(LANES):
            seg, i = lane // 32, lane % 32
            head, half = 2 * pair + seg % 2, seg // 2
            idx.append(head * HEAD + half * 32 + i)
    return np.asarray(idx, np.int32)


def _pack_w_in(w_in, w_vres_down):
    depth = w_in.shape[0]
    perm = _qk_perm()
    vres = jnp.concatenate([jnp.zeros((1, D, 32), F32), w_vres_down], axis=0)
    vres = jnp.pad(vres, ((0, 0), (0, 0), (0, LANES - 32)))
    parts = [w_in[:, :, 0:3072], w_in[:, :, 3328:4352], w_in[:, :, 5376:6400], w_in[:, :, 6400:7424],
             w_in[:, :, 7424:8448], w_in[:, :, 8448:9472],
             w_in[:, :, 4352:4864][:, :, perm], w_in[:, :, 4864:5376][:, :, perm],
             w_in[:, :, 3072:3328], vres]
    w = jnp.concatenate(parts, axis=2)
    assert w.shape == (depth, D, NPAD)
    return w.astype(BF16)


def _pair_cat(wf, wb):
    pairs = D // LANES
    cat = jnp.concatenate([wf.reshape(LANES, pairs, LANES), wb.reshape(LANES, pairs, LANES)], axis=2)
    return cat.transpose(1, 0, 2)


def _rope_tables(seq):
    half = 32
    freqs = jnp.power(ROPE_BASE, -jnp.arange(half, dtype=F32) / half)
    ang = jnp.arange(seq, dtype=F32)[:, None] * freqs[None, :]
    cos = jnp.tile(jnp.cos(ang), (1, 4))
    sin = jnp.tile(jnp.sin(ang), (1, 4))
    sign = jnp.where(jnp.arange(LANES) < 64, -1.0, 1.0).astype(F32)
    return cos, sin * sign[None, :]


def kernel(x, norm_gain, w_in, w_vres_down, shift_prev, shift_next, w_decay_up, decay_bias, w_iclr_up, iclr_bias,
           w_vres_up, vres_bias, k_k, k_a, r_k, lnx_gain, lnx_bias, w_branch_a, ret_norm_gain, w_branch_b, w_out,
           final_gain):
    bsz, seq, _ = x.shape
    depth = w_in.shape[0]
    assert seq % RET_C == 0 and seq % SLAB == 0 and SLAB % CH == 0
    t = bsz * seq

    w_all = _pack_w_in(w_in, w_vres_down)
    wa_all, wb_all, wo_all = (w.astype(BF16) for w in (w_branch_a, w_branch_b, w_out))
    cos_t, sin_t = _rope_tables(seq)
    log_g = jnp.log1p(-jnp.exp2(-5.0 - jnp.arange(RET_HEADS, dtype=F32)))
    lg_t = jnp.broadcast_to(log_g[:, None], (RET_HEADS, 2 * LANES))
    zpad = jnp.zeros((64, D), F32)
    zmat = jnp.zeros((LANES, D), F32)

    x2 = x.reshape(t, D)
    xn2 = _norm_call(x2, norm_gain[0][None, :])
    vfirst = None
    for l in range(depth):
        proj2 = _proj_call(xn2, w_all[l])
        proj3 = proj2.reshape(bsz, seq, NPAD)

        vb = vres_bias[l - 1] if l > 0 else jnp.zeros((D,), F32)
        vecs = jnp.stack([shift_prev[l, 0:1024], shift_prev[l, 1024:2048], shift_prev[l, 2048:3072],
                          shift_next[l, 0:1024], shift_next[l, 1024:2048], shift_next[l, 2048:3072],
                          decay_bias[l, 0], decay_bias[l, 1], iclr_bias[l, 0], iclr_bias[l, 1],
                          k_k[l], k_a[l], r_k[l].reshape(D), lnx_gain[l], lnx_bias[l], vb], axis=0)
        mul = jnp.concatenate([jnp.stack([shift_prev[l, 3072:3200], shift_prev[l, 3200:3328],
                                          shift_next[l, 3072:3200], shift_next[l, 3200:3328]], axis=0),
                               jnp.zeros((4, LANES), F32)], axis=0)
        wvr = jnp.pad(w_vres_up[l - 1], ((0, LANES - 32), (0, 0))) if l > 0 else zmat
        wl = jnp.stack([_pair_cat(jnp.concatenate([w_decay_up[l, 0], zpad], axis=0),
                                  jnp.concatenate([zpad, w_decay_up[l, 1]], axis=0)),
                        _pair_cat(jnp.concatenate([w_iclr_up[l, 0], zpad], axis=0),
                                  jnp.concatenate([zpad, w_iclr_up[l, 1]], axis=0)),
                        _pair_cat(wvr, zmat)], axis=1)

        if l == 0:
            ya, vfirst = _wkv_call(proj3, None, vecs, mul, wl, first=True)
        else:
            ya = _wkv_call(proj3, vfirst, vecs, mul, wl, first=False)
        yb = _ret_call(proj3, cos_t, sin_t, lg_t, ret_norm_gain[l][None, :])
        last = l == depth - 1
        gain = final_gain if last else norm_gain[l + 1]
        res = _merge_call(x2, ya.reshape(t, D), yb.reshape(t, D), proj2, wa_all[l], wb_all[l], wo_all[l],
                          gain[None, :], last=last)
        if not last:
            x2, xn2 = res

    return res.reshape(bsz, seq, D)
```

```python
import functools

import numpy as np
import jax
import jax.numpy as jnp
from jax import lax
from jax.experimental import pallas as pl
from jax.experimental.pallas import tpu as pltpu

F32 = jnp.float32
BF16 = jnp.bfloat16

D = 1024
HEAD = 64
LANES = 128
CH = 64
SLAB = 512
RET_HEADS = 8
RET_C = 256
NORM_EPS = 1e-6
GN_EPS = 64e-5
EXP_M_HALF = 0.6065306597126334
ROPE_BASE = 10000.0
VMEM_LIMIT = 56 * 1024 * 1024

C_R, C_K, C_V, C_GA, C_VB, C_GB, C_MA, C_MB = (i * 1024 for i in range(8))
C_QB, C_KB, C_DEC, C_ICL, C_VRES, NPAD = 8192, 8704, 9216, 9344, 9472, 9600

P_LORA = 1
P_WKV = 1

_NN = (((1,), (0,)), ((), ()))
_NT = (((1,), (1,)), ((), ()))
_TN = (((0,), (0,)), ((), ()))


def _dg(a, b, dims):
    return lax.dot_general(a, b, dims, preferred_element_type=F32)


def _split2(x):
    hi = x.astype(BF16)
    lo = (x - hi.astype(F32)).astype(BF16)
    return hi, lo


def _mm(a, b, dims=_NN, prec=1):
    if prec == 1:
        return _dg(a.astype(BF16), b.astype(BF16), dims)
    ah, al = _split2(a)
    bh, bl = _split2(b)
    return _dg(ah, bh, dims) + (_dg(ah, bl, dims) + _dg(al, bh, dims))


def _headsum(xs, ones_bd):
    return [_dg(x.astype(BF16), ones_bd, _NN) for x in xs]


def _iota(shape, axis):
    return lax.broadcasted_iota(jnp.int32, shape, axis)


def _pair_bd(x, bdmask):
    return jnp.where(bdmask, jnp.concatenate([x, x], axis=0), 0.0)


def _wkv_consts():
    row = _iota((CH, LANES), 0)
    col = _iota((CH, LANES), 1) % CH
    r128 = _iota((LANES, LANES), 0)
    c128 = _iota((LANES, LANES), 1)

    def masks(lower):
        late, early = (row, col) if lower else (col, row)
        return dict(
            strict=early < late, incl=early <= late,
            m_d=(early < late) & ((row // 16) == (col // 16)),
            m_o1=((row // 32) == (col // 32)) & ((late // 16) % 2 == 1) & ((early // 16) % 2 == 0),
            m_o2=((late // 32) == 1) & ((early // 32) == 0))

    return dict(bd=(r128 // CH) == (c128 // CH), eye128=r128 == c128,
                eye_cat=jnp.where(row == col, 1.0, 0.0).astype(F32),
                fwd=masks(True), bwd=masks(False))


def _wkv_mxu_steps(items, cst):
    bdmask = cst["bd"]

    def bd(x):
        return _pair_bd(x, bdmask)

    def mm(a, b, dims=_NN):
        return _mm(a, b, dims, prec=P_WKV)

    st = [dict(it=it, mk=cst["bwd"] if it["rev"] else cst["fwd"]) for it in items]
    for s in st:
        ld = s["it"]["load"]
        amat = mm(jnp.concatenate([ld("at"), ld("rt")], axis=0),
                  jnp.concatenate([bd(ld("bt")), bd(ld("kt"))], axis=0), _NT)
        mk = s["mk"]
        s["n"] = amat[:CH, :LANES]
        s["akrk"] = jnp.concatenate([jnp.where(mk["strict"], amat[:CH, LANES:], 0.0),
                                     jnp.where(mk["incl"], amat[CH:, LANES:], 0.0)], axis=0)
        s["rb"] = jnp.where(mk["incl"], amat[CH:, :LANES], 0.0)
    yield
    for s in st:
        xv = mm(s["akrk"], bd(s["it"]["load"]("v")))
        s["x1"], s["y0v"] = xv[:CH], xv[CH:]
    yield
    for s in st:
        s["p"] = jnp.where(s["mk"]["m_d"], s["n"], 0.0)
        s["t"] = cst["eye_cat"] + s["p"]
        s["q"] = mm(s["p"], bd(s["p"]))
    yield
    for _ in range(2):
        for s in st:
            qt = mm(jnp.concatenate([s["q"], s["t"]], axis=0), bd(s["q"]))
            s["t"] = s["t"] + qt[CH:]
            s["q"] = qt[:CH]
        yield
    for s in st:
        s["t"] = s["t"] + mm(s["t"], bd(s["q"]))
    yield
    for key in ("m_o1", "m_o2"):
        for s in st:
            s["x"] = mm(s["t"], bd(jnp.where(s["mk"][key], s["n"], 0.0)))
        yield
        for s in st:
            s["t"] = s["t"] + mm(s["x"], bd(s["t"]))
        yield
    for s in st:
        au = mm(s["t"], jnp.concatenate([bd(s["it"]["load"]("at")), bd(s["x1"])], axis=1))
        s["ah"], s["u0"] = au[:, :LANES], au[:, LANES:]
    yield
    for s in st:
        ld = s["it"]["load"]
        w_tot = s["it"]["w_tot"]()
        ry = mm(s["rb"], jnp.concatenate([bd(s["ah"]), bd(s["u0"])], axis=1))
        rhs2 = jnp.concatenate([jnp.concatenate([s["ah"], s["u0"]], axis=1),
                                jnp.concatenate([jnp.zeros((CH, LANES), F32), ld("v")], axis=1)], axis=0)
        kb_end = jnp.concatenate([ld("bt") * w_tot, ld("kt") * w_tot], axis=0)
        mg = mm(kb_end, rhs2, _TN)
        m = jnp.where(bdmask, mg[:, :LANES], 0.0) + jnp.where(cst["eye128"], w_tot, 0.0)
        g = jnp.where(bdmask, mg[:, LANES:], 0.0)
        s["it"]["store"](ld("rt") + ry[:, :LANES], ry[:, LANES:] + s["y0v"], m, g)
    yield


def _run(gen):
    for _ in gen:
        pass


def _interleave(main, side, ratio):
    for _ in main:
        for _ in range(ratio):
            next(side, None)
    _run(side)


def _chain(gens):
    for g in gens:
        yield from g


def _sigmoid(x):
    return 0.5 * jnp.tanh(0.5 * x) + 0.5


def _shifted(ref, start, n, mp, mn, seq):
    sub = 8
    edge = _iota((sub, LANES), 0)
    cur = ref[start:start + n, :]
    if start > 0:
        prev = ref[start - 1:start - 1 + n, :]
    else:
        head = jnp.where(edge == 0, 0.0, pltpu.roll(ref[0:sub, :], 1, 0))
        prev = jnp.concatenate([head, ref[sub - 1:n - 1, :]], axis=0)
    if start + n < seq:
        nxt = ref[start + 1:start + 1 + n, :]
    else:
        tail = jnp.where(edge == sub - 1, 0.0, pltpu.roll(ref[seq - sub:seq, :], sub - 1, 0))
        nxt = jnp.concatenate([ref[start + 1:seq - sub + 1, :], tail], axis=0)
    return cur * (1.0 - mp - mn) + mp * prev + mn * nxt


def _chunk_cumsum(x, rev):
    n = x.shape[0]
    sub = 8
    pos = _iota((n, LANES), 0) % CH
    s = 1
    while s < sub:
        if rev:
            x = x + jnp.where(pos < CH - s, pltpu.roll(x, n - s, 0), 0.0)
        else:
            x = x + jnp.where(pos >= s, pltpu.roll(x, s, 0), 0.0)
        s *= 2
    g = CH // sub
    x = x.reshape(n // CH, g, sub, LANES)
    s = 1
    while s < g:
        z = jnp.zeros((n // CH, s, sub, LANES), F32)
        if rev:
            x = x + jnp.concatenate([x[:, s:], z], axis=1)
        else:
            x = x + jnp.concatenate([z, x[:, :-s]], axis=1)
        s *= 2
    return x.reshape(n, LANES)


(V_MPR, V_MPK, V_MPV, V_MNR, V_MNK, V_MNV, V_DBF, V_DBB, V_IBF, V_IBB,
 V_KK, V_KA, V_RK, V_LG, V_LB, V_VB) = range(16)
W_DEC, W_ICL, W_VR = range(3)
I_RT, I_AT, I_KT, I_BT = range(4)
I_NAMES = dict(rt=I_RT, at=I_AT, kt=I_KT, bt=I_BT)


def _wkv_kernel(*refs, seq, first):
    if first:
        (r_ref, k_ref, v_ref, ga_ref, dec_ref, icl_ref, vec_ref, mul_ref, wl_ref,
         ya_ref, vf_out_ref, lhs_s, add_s, yf_s, yb_s, bon_s, it_s, v_s, wt_s) = refs
        vres_ref = vf_ref = None
    else:
        (r_ref, k_ref, v_ref, ga_ref, dec_ref, icl_ref, vres_ref, vf_ref, vec_ref, mul_ref, wl_ref,
         ya_ref, lhs_s, add_s, yf_s, yb_s, bon_s, it_s, v_s, wt_s) = refs
        vf_out_ref = None
    ns = seq // SLAB
    per = SLAB // CH
    cst = _wkv_consts()
    ones_bd = jnp.where(cst["bd"], 1.0, 0.0).astype(BF16)

    def vec(i):
        return vec_ref[i:i + 1, :]

    def prologue(sb):
        start = sb * SLAB
        rows = slice(start, start + SLAB)
        r = _shifted(r_ref, start, SLAB, vec(V_MPR), vec(V_MNR), seq)
        yield
        k = _shifted(k_ref, start, SLAB, vec(V_MPK), vec(V_MNK), seq)
        yield
        v = _shifted(v_ref, start, SLAB, vec(V_MPV), vec(V_MNV), seq)
        yield
        dec = _shifted(dec_ref, start, SLAB, mul_ref[0:1, :], mul_ref[2:3, :], seq)
        icl = _shifted(icl_ref, start, SLAB, mul_ref[1:2, :], mul_ref[3:4, :], seq)
        yield
        if first:
            vf_out_ref[rows, :] = v
        else:
            mix = _sigmoid(vec(V_VB) + _mm(vres_ref[rows, :], wl_ref[W_VR], prec=P_LORA)[:, :LANES])
            v = v + (vf_ref[rows, :] - v) * mix
        v_s[rows, :] = v
        yield
        zd = _mm(jnp.tanh(dec), wl_ref[W_DEC], prec=P_LORA)
        zi = _mm(icl, wl_ref[W_ICL], prec=P_LORA)
        kk = k * vec(V_KK)
        yield
        kds, ags, lws = [], [], []
        kka = k * vec(V_KA)
        kbase = k - kka
        for d in range(2):
            z = vec(V_DBF + d) + zd[:, d * LANES:(d + 1) * LANES]
            lws.append(-EXP_M_HALF * _sigmoid(z))
            ag = _sigmoid(vec(V_IBF + d) + zi[:, d * LANES:(d + 1) * LANES])
            kds.append(kbase + kka * ag)
            ags.append(ag)
            yield
        ssq, bsum = _headsum([kk * kk, r * (kds[0] + kds[1]) * vec(V_RK)], ones_bd)
        kk = kk * lax.rsqrt(ssq + 1e-12)
        bon_s[rows, :] = bsum * v
        yield
        for d in range(2):
            cum = _chunk_cumsum(lws[d], d == 1)
            yield
            e_inv = jnp.exp(-cum)
            it_s[d, I_RT, rows, :] = r * jnp.exp(cum)
            it_s[d, I_AT, rows, :] = -kk * jnp.exp(cum - lws[d])
            yield
            it_s[d, I_KT, rows, :] = kds[d] * e_inv
            it_s[d, I_BT, rows, :] = (kk * ags[d]) * e_inv
            for i in range(per):
                c = sb * per + i
                row = i * CH if d == 1 else i * CH + CH - 1
                wt_s[d, c * 8:(c + 1) * 8, :] = jnp.broadcast_to(jnp.exp(cum[row:row + 1, :]), (8, LANES))
            yield

    def chunk_items(step):
        buf = step % 2
        items = []
        for d in range(2):
            slab = step if d == 0 else ns - 1 - step
            for i in range(per):
                c = slab * per + i
                crow = slice(c * CH, (c + 1) * CH)

                def load(name, d=d, crow=crow):
                    if name == "v":
                        return v_s[crow, :]
                    return it_s[d, I_NAMES[name], crow, :]

                def w_tot(d=d, c=c):
                    return wt_s[d, c * 8:c * 8 + 1, :]

                def store(rh, y0, m, g, d=d, i=i):
                    lhs_s[buf, d, i, :CH, :] = rh
                    lhs_s[buf, d, i, CH:, :] = m
                    add_s[buf, d, i, :CH, :] = y0
                    add_s[buf, d, i, CH:, :] = g

                items.append(dict(rev=d == 1, load=load, w_tot=w_tot, store=store))
        return items

    state = [jnp.zeros((LANES, LANES), F32), jnp.zeros((LANES, LANES), F32)]

    def scan(step):
        buf = step % 2
        for j in range(per):
            for d in range(2):
                i = j if d == 0 else per - 1 - j
                slab = step if d == 0 else ns - 1 - step
                o = _mm(lhs_s[buf, d, i], state[d], prec=P_WKV) + add_s[buf, d, i]
                rows = slice((slab * per + i) * CH, (slab * per + i + 1) * CH)
                (yf_s if d == 0 else yb_s)[rows, :] = o[:CH]
                state[d] = o[CH:]
            yield

    def post(start, n):
        rows = slice(start, start + n)
        y = yf_s[rows, :] + yb_s[rows, :]
        mu = _headsum([y], ones_bd)[0] * (1.0 / HEAD)
        yield
        dv = y - mu
        var = _headsum([dv * dv], ones_bd)[0] * (1.0 / HEAD)
        yield
        out = dv * lax.rsqrt(var + GN_EPS) * vec(V_LG) + vec(V_LB) + bon_s[rows, :]
        ga = ga_ref[rows, :]
        ya_ref[rows, :] = (out * (ga * _sigmoid(ga))).astype(BF16)
        yield

    ends = sorted({0, ns - 1})
    inner = [sb for sb in range(ns) if sb not in ends]
    for sb in ends:
        _run(prologue(sb))
    for step in range(ns):
        side = _chain([prologue(sb) for sb in inner]) if step == 0 else scan(step - 1)
        _interleave(_wkv_mxu_steps(chunk_items(step), cst), side, 3 if step == 0 else 1)
    _interleave(scan(ns - 1), _chain([post(sb * SLAB, SLAB) for sb in inner]), 1)
    for sb in ends:
        _run(post(sb * SLAB, SLAB))


def _wkv_call(proj3, vfirst, vecs, mul, wl, *, first):
    bsz, seq, _ = proj3.shape
    nc = seq // CH
    per = SLAB // CH

    def col(base):
        blk = base // LANES
        return pl.BlockSpec((None, seq, LANES), lambda b, p, blk=blk: (b, 0, blk + p))

    def fixed(base):
        blk = base // LANES
        return pl.BlockSpec((None, seq, LANES), lambda b, p, blk=blk: (b, 0, blk))

    in_specs = [col(C_R), col(C_K), col(C_V), col(C_GA), fixed(C_DEC), fixed(C_ICL)]
    args = [proj3] * 6
    if not first:
        in_specs += [fixed(C_VRES), pl.BlockSpec((None, seq, LANES), lambda b, p: (b, 0, p))]
        args += [proj3, vfirst]
    in_specs += [pl.BlockSpec((16, LANES), lambda b, p: (0, p)),
                 pl.BlockSpec((8, LANES), lambda b, p: (0, 0)),
                 pl.BlockSpec((None, 3, LANES, 2 * LANES), lambda b, p: (p, 0, 0, 0))]
    args += [vecs, mul, wl]
    out_block = pl.BlockSpec((None, seq, LANES), lambda b, p: (b, 0, p))
    out_sds = jax.ShapeDtypeStruct((bsz, seq, D), BF16)
    out_shape = (out_sds, jax.ShapeDtypeStruct((bsz, seq, D), F32)) if first else out_sds
    out_specs = (out_block, out_block) if first else out_block
    scratch = [pltpu.VMEM((2, 2, per, CH + LANES, LANES), F32),
               pltpu.VMEM((2, 2, per, CH + LANES, LANES), F32),
               pltpu.VMEM((seq, LANES), F32), pltpu.VMEM((seq, LANES), F32), pltpu.VMEM((seq, LANES), F32),
               pltpu.VMEM((2, 4, seq, LANES), F32),
               pltpu.VMEM((seq, LANES), F32),
               pltpu.VMEM((2, nc * 8, LANES), F32)]
    return pl.pallas_call(
        functools.partial(_wkv_kernel, seq=seq, first=first),
        grid=(bsz, D // LANES),
        in_specs=in_specs, out_specs=out_specs, out_shape=out_shape,
        scratch_shapes=scratch,
        compiler_params=pltpu.CompilerParams(dimension_semantics=("arbitrary", "arbitrary"),
                                             vmem_limit_bytes=VMEM_LIMIT),
        name="wkv7",
    )(*args)


def _rms(x, g):
    ms = jnp.mean(x * x, axis=-1, keepdims=True)
    return x * lax.rsqrt(ms + NORM_EPS) * g


def _proj_kernel(xn_ref, w_ref, o_ref):
    o_ref[...] = jnp.dot(xn_ref[...], w_ref[...], preferred_element_type=F32)


def _proj_call(xn2, w):
    t = xn2.shape[0]
    tm = min(1024, t)
    tn = 1920
    return pl.pallas_call(
        _proj_kernel,
        grid=(t // tm, NPAD // tn),
        in_specs=[pl.BlockSpec((tm, D), lambda i, j: (i, 0)),
                  pl.BlockSpec((D, tn), lambda i, j: (0, j))],
        out_specs=pl.BlockSpec((tm, tn), lambda i, j: (i, j)),
        out_shape=jax.ShapeDtypeStruct((t, NPAD), F32),
        compiler_params=pltpu.CompilerParams(dimension_semantics=("arbitrary", "arbitrary"),
                                             vmem_limit_bytes=VMEM_LIMIT),
        name="proj",
    )(xn2, w)


def _ret_kernel(q_ref, k_ref, v_ref, gb_ref, cos_ref, sin_ref, lg_ref, gain_ref, o_ref, qr_s, kr_s, y_s, *, seq):
    nc = seq // RET_C
    p = pl.program_id(1)
    scale = HEAD ** -0.5

    def rot(c, _):
        sl = pl.ds(pl.multiple_of(c * RET_C, RET_C), RET_C)
        cs, sn = cos_ref[sl, :], sin_ref[sl, :]
        q = q_ref[sl, :]
        k = k_ref[sl, :]
        qr_s[sl, :] = (q * cs + pltpu.roll(q, 64, 1) * sn) * scale
        kr_s[sl, :] = k * cs + pltpu.roll(k, 64, 1) * sn
        return 0

    lax.fori_loop(0, nc, rot, 0)

    lane = _iota((RET_C, LANES), 1)
    rowi = _iota((RET_C, LANES), 0).astype(F32)
    absd = jnp.abs(_iota((RET_C, RET_C), 0) - _iota((RET_C, RET_C), 1)).astype(F32)
    for j in range(2):
        hmask = ((lane // 32) % 2) == j
        lg = lg_ref[pl.ds(2 * p + j, 1), :]
        lg128 = lg[:, :LANES]
        dmat = jnp.exp(absd * lg)
        xi_f = jnp.exp((rowi + 1.0) * lg128)
        ze_f = jnp.exp((RET_C - 1.0 - rowi) * lg128)
        xi_b = jnp.exp((RET_C - rowi) * lg128)
        ze_b = jnp.exp(rowi * lg128)
        cdec = jnp.exp(RET_C * lg128)
        vs = slice(j * LANES, (j + 1) * LANES)
        fst = jnp.zeros((LANES, LANES), F32)
        for i in range(nc):
            sl = slice(i * RET_C, (i + 1) * RET_C)
            qh = jnp.where(hmask, qr_s[sl, :], 0.0)
            kh = kr_s[sl, :]
            vh = v_ref[sl, vs]
            sc = _mm(qh, kh, _NT) * dmat
            y = _mm(sc, vh)
            if i > 0:
                y = y + _mm(qh * xi_f, fst)
            if i < nc - 1:
                fst = fst * cdec + _mm(jnp.where(hmask, kh, 0.0) * ze_f, vh, _TN)
            y_s[sl, vs] = y
        gst = jnp.zeros((LANES, LANES), F32)
        for i in range(nc - 1, -1, -1):
            sl = slice(i * RET_C, (i + 1) * RET_C)
            y = y_s[sl, vs]
            if i < nc - 1:
                qh = jnp.where(hmask, qr_s[sl, :], 0.0)
                y = y + _mm(qh * xi_b, gst)
            if i > 0:
                kh = jnp.where(hmask, kr_s[sl, :], 0.0)
                gst = gst * cdec + _mm(kh * ze_b, v_ref[sl, vs], _TN)
            mu = jnp.mean(y, axis=-1, keepdims=True)
            dv = y - mu
            var = jnp.mean(dv * dv, axis=-1, keepdims=True)
            out = dv * lax.rsqrt(var + NORM_EPS) * gain_ref[:, vs]
            o_ref[sl, vs] = (out * jax.nn.silu(gb_ref[sl, vs])).astype(BF16)


def _ret_call(proj3, cos_t, sin_t, lg_t, gain):
    bsz, seq, _ = proj3.shape
    w2 = 2 * LANES

    def spec(base, width):
        blk = base // width
        return pl.BlockSpec((None, seq, width), lambda b, p, blk=blk: (b, 0, blk + p))

    return pl.pallas_call(
        functools.partial(_ret_kernel, seq=seq),
        grid=(bsz, RET_HEADS // 2),
        in_specs=[spec(C_QB, LANES), spec(C_KB, LANES), spec(C_VB, w2), spec(C_GB, w2),
                  pl.BlockSpec((seq, LANES), lambda b, p: (0, 0)),
                  pl.BlockSpec((seq, LANES), lambda b, p: (0, 0)),
                  pl.BlockSpec((RET_HEADS, w2), lambda b, p: (0, 0)),
                  pl.BlockSpec((1, w2), lambda b, p: (0, p))],
        out_specs=pl.BlockSpec((None, seq, w2), lambda b, p: (b, 0, p)),
        out_shape=jax.ShapeDtypeStruct((bsz, seq, D), BF16),
        scratch_shapes=[pltpu.VMEM((seq, LANES), F32), pltpu.VMEM((seq, LANES), F32), pltpu.VMEM((seq, w2), F32)],
        compiler_params=pltpu.CompilerParams(dimension_semantics=("arbitrary", "arbitrary"),
                                             vmem_limit_bytes=VMEM_LIMIT),
        name="retention",
    )(proj3, proj3, proj3, proj3, cos_t, sin_t, lg_t, gain)


def _merge_kernel(x_ref, ya_ref, yb_ref, ma_ref, mb_ref, wa_ref, wb_ref, wo_ref, g_ref, *o_refs, last):
    za = jnp.dot(ya_ref[...], wa_ref[...], preferred_element_type=F32)
    zb = jnp.dot(yb_ref[...], wb_ref[...], preferred_element_type=F32)
    merged = jax.nn.sigmoid(ma_ref[...]) * za + jax.nn.sigmoid(mb_ref[...]) * zb
    x = x_ref[...] + jnp.dot(merged.astype(BF16), wo_ref[...], preferred_element_type=F32)
    if last:
        o_refs[0][...] = _rms(x, g_ref[...])
    else:
        o_refs[0][...] = x
        o_refs[1][...] = _rms(x, g_ref[...]).astype(BF16)


def _merge_call(x2, ya2, yb2, proj2, wa, wb, wo, gain, *, last):
    t = x2.shape[0]
    tm = min(512, t)
    row = pl.BlockSpec((tm, D), lambda i: (i, 0))
    wspec = pl.BlockSpec((D, D), lambda i: (0, 0))
    x_sds = jax.ShapeDtypeStruct((t, D), F32)
    return pl.pallas_call(
        functools.partial(_merge_kernel, last=last),
        grid=(t // tm,),
        in_specs=[row, row, row,
                  pl.BlockSpec((tm, D), lambda i: (i, C_MA // D)),
                  pl.BlockSpec((tm, D), lambda i: (i, C_MB // D)),
                  wspec, wspec, wspec, pl.BlockSpec((1, D), lambda i: (0, 0))],
        out_specs=row if last else (row, row),
        out_shape=x_sds if last else (x_sds, jax.ShapeDtypeStruct((t, D), BF16)),
        compiler_params=pltpu.CompilerParams(dimension_semantics=("arbitrary",),
                                             vmem_limit_bytes=VMEM_LIMIT),
        name="merge",
    )(x2, ya2, yb2, proj2, proj2, wa, wb, wo, gain)


def _norm_kernel(x_ref, g_ref, o_ref):
    o_ref[...] = _rms(x_ref[...], g_ref[...]).astype(BF16)


def _norm_call(x2, gain):
    t = x2.shape[0]
    tm = min(1024, t)
    return pl.pallas_call(
        _norm_kernel,
        grid=(t // tm,),
        in_specs=[pl.BlockSpec((tm, D), lambda i: (i, 0)), pl.BlockSpec((1, D), lambda i: (0, 0))],
        out_specs=pl.BlockSpec((tm, D), lambda i: (i, 0)),
        out_shape=jax.ShapeDtypeStruct((t, D), BF16),
        compiler_params=pltpu.CompilerParams(dimension_semantics=("arbitrary",)),
        name="first_norm",
    )(x2, gain)


def _qk_perm():
    idx = []
    for pair in range(RET_HEADS // 2):
        for lane in range(LANES):
            seg, i = lane // 32, lane % 32
            head, half = 2 * pair + seg % 2, seg // 2
            idx.append(head * HEAD + half * 32 + i)
    return np.asarray(idx, np.int32)


def _pack_w_in(w_in, w_vres_down):
    depth = w_in.shape[0]
    perm = _qk_perm()
    vres = jnp.concatenate([jnp.zeros((1, D, 32), F32), w_vres_down], axis=0)
    vres = jnp.pad(vres, ((0, 0), (0, 0), (0, LANES - 32)))
    parts = [w_in[:, :, 0:3072], w_in[:, :, 3328:4352], w_in[:, :, 5376:6400], w_in[:, :, 6400:7424],
             w_in[:, :, 7424:8448], w_in[:, :, 8448:9472],
             w_in[:, :, 4352:4864][:, :, perm], w_in[:, :, 4864:5376][:, :, perm],
             w_in[:, :, 3072:3328], vres]
    w = jnp.concatenate(parts, axis=2)
    assert w.shape == (depth, D, NPAD)
    return w.astype(BF16)


def _pair_cat(wf, wb):
    pairs = D // LANES
    cat = jnp.concatenate([wf.reshape(LANES, pairs, LANES), wb.reshape(LANES, pairs, LANES)], axis=2)
    return cat.transpose(1, 0, 2)


def _rope_tables(seq):
    half = 32
    freqs = jnp.power(ROPE_BASE, -jnp.arange(half, dtype=F32) / half)
    ang = jnp.arange(seq, dtype=F32)[:, None] * freqs[None, :]
    cos = jnp.tile(jnp.cos(ang), (1, 4))
    sin = jnp.tile(jnp.sin(ang), (1, 4))
    sign = jnp.where(jnp.arange(LANES) < 64, -1.0, 1.0).astype(F32)
    return cos, sin * sign[None, :]


def kernel(x, norm_gain, w_in, w_vres_down, shift_prev, shift_next, w_decay_up, decay_bias, w_iclr_up, iclr_bias,
           w_vres_up, vres_bias, k_k, k_a, r_k, lnx_gain, lnx_bias, w_branch_a, ret_norm_gain, w_branch_b, w_out,
           final_gain):
    bsz, seq, _ = x.shape
    depth = w_in.shape[0]
    assert seq % RET_C == 0 and seq % SLAB == 0 and SLAB % CH == 0
    t = bsz * seq

    w_all = _pack_w_in(w_in, w_vres_down)
    wa_all, wb_all, wo_all = (w.astype(BF16) for w in (w_branch_a, w_branch_b, w_out))
    cos_t, sin_t = _rope_tables(seq)
    log_g = jnp.log1p(-jnp.exp2(-5.0 - jnp.arange(RET_HEADS, dtype=F32)))
    lg_t = jnp.broadcast_to(log_g[:, None], (RET_HEADS, 2 * LANES))
    zpad = jnp.zeros((64, D), F32)
    zmat = jnp.zeros((LANES, D), F32)

    x2 = x.reshape(t, D)
    xn2 = _norm_call(x2, norm_gain[0][None, :])
    vfirst = None
    for l in range(depth):
        proj2 = _proj_call(xn2, w_all[l])
        proj3 = proj2.reshape(bsz, seq, NPAD)

        vb = vres_bias[l - 1] if l > 0 else jnp.zeros((D,), F32)
        vecs = jnp.stack([shift_prev[l, 0:1024], shift_prev[l, 1024:2048], shift_prev[l, 2048:3072],
                          shift_next[l, 0:1024], shift_next[l, 1024:2048], shift_next[l, 2048:3072],
                          decay_bias[l, 0], decay_bias[l, 1], iclr_bias[l, 0], iclr_bias[l, 1],
                          k_k[l], k_a[l], r_k[l].reshape(D), lnx_gain[l], lnx_bias[l], vb], axis=0)
        mul = jnp.concatenate([jnp.stack([shift_prev[l, 3072:3200], shift_prev[l, 3200:3328],
                                          shift_next[l, 3072:3200], shift_next[l, 3200:3328]], axis=0),
                               jnp.zeros((4, LANES), F32)], axis=0)
        wvr = jnp.pad(w_vres_up[l - 1], ((0, LANES - 32), (0, 0))) if l > 0 else zmat
        wl = jnp.stack([_pair_cat(jnp.concatenate([w_decay_up[l, 0], zpad], axis=0),
                                  jnp.concatenate([zpad, w_decay_up[l, 1]], axis=0)),
                        _pair_cat(jnp.concatenate([w_iclr_up[l, 0], zpad], axis=0),
                                  jnp.concatenate([zpad, w_iclr_up[l, 1]], axis=0)),
                        _pair_cat(wvr, zmat)], axis=1)

        if l == 0:
            ya, vfirst = _wkv_call(proj3, None, vecs, mul, wl, first=True)
        else:
            ya = _wkv_call(proj3, vfirst, vecs, mul, wl, first=False)
        yb = _ret_call(proj3, cos_t, sin_t, lg_t, ret_norm_gain[l][None, :])
        last = l == depth - 1
        gain = final_gain if last else norm_gain[l + 1]
        res = _merge_call(x2, ya.reshape(t, D), yb.reshape(t, D), proj2, wa_all[l], wb_all[l], wo_all[l],
                          gain[None, :], last=last)
        if not last:
            x2, xn2 = res

    return res.reshape(bsz, seq, D)
```

```python
import functools

import numpy as np
import jax
import jax.numpy as jnp
from jax import lax
from jax.experimental import pallas as pl
from jax.experimental.pallas import tpu as pltpu

F32 = jnp.float32
BF16 = jnp.bfloat16

D = 1024
HEAD = 64
LANES = 128
CH = 64
SLAB = 512
RET_HEADS = 8
RET_C = 256
NORM_EPS = 1e-6
GN_EPS = 64e-5
EXP_M_HALF = 0.6065306597126334
ROPE_BASE = 10000.0
VMEM_LIMIT = 56 * 1024 * 1024

C_R, C_K, C_V, C_GA, C_VB, C_GB, C_MA, C_MB = (i * 1024 for i in range(8))
C_QB, C_KB, C_DEC, C_ICL, C_VRES, NPAD = 8192, 8704, 9216, 9344, 9472, 9600

P_LORA = 1
P_WKV = 1

_NN = (((1,), (0,)), ((), ()))
_NT = (((1,), (1,)), ((), ()))
_TN = (((0,), (0,)), ((), ()))


def _dg(a, b, dims):
    return lax.dot_general(a, b, dims, preferred_element_type=F32)


def _split2(x):
    hi = x.astype(BF16)
    lo = (x - hi.astype(F32)).astype(BF16)
    return hi, lo


def _mm(a, b, dims=_NN, prec=1):
    if prec == 1:
        return _dg(a.astype(BF16), b.astype(BF16), dims)
    ah, al = _split2(a)
    bh, bl = _split2(b)
    return _dg(ah, bh, dims) + (_dg(ah, bl, dims) + _dg(al, bh, dims))


def _headsum(xs, ones_bd):
    return [_dg(x.astype(BF16), ones_bd, _NN) for x in xs]


def _iota(shape, axis):
    return lax.broadcasted_iota(jnp.int32, shape, axis)


def _pair_bd(x, bdmask):
    return jnp.where(bdmask, jnp.concatenate([x, x], axis=0), 0.0)


def _wkv_consts():
    row = _iota((CH, LANES), 0)
    col = _iota((CH, LANES), 1) % CH
    r128 = _iota((LANES, LANES), 0)
    c128 = _iota((LANES, LANES), 1)

    def masks(lower):
        late, early = (row, col) if lower else (col, row)
        return dict(
            strict=early < late, incl=early <= late,
            m_d=(early < late) & ((row // 16) == (col // 16)),
            m_o1=((row // 32) == (col // 32)) & ((late // 16) % 2 == 1) & ((early // 16) % 2 == 0),
            m_o2=((late // 32) == 1) & ((early // 32) == 0))

    return dict(bd=(r128 // CH) == (c128 // CH), eye128=r128 == c128,
                eye_cat=jnp.where(row == col, 1.0, 0.0).astype(F32),
                fwd=masks(True), bwd=masks(False))


def _wkv_mxu_steps(items, cst):
    bdmask = cst["bd"]

    def bd(x):
        return _pair_bd(x, bdmask)

    def mm(a, b, dims=_NN):
        return _mm(a, b, dims, prec=P_WKV)

    st = [dict(it=it, mk=cst["bwd"] if it["rev"] else cst["fwd"]) for it in items]
    for s in st:
        ld = s["it"]["load"]
        amat = mm(jnp.concatenate([ld("at"), ld("rt")], axis=0),
                  jnp.concatenate([bd(ld("bt")), bd(ld("kt"))], axis=0), _NT)
        mk = s["mk"]
        s["n"] = amat[:CH, :LANES]
        s["akrk"] = jnp.concatenate([jnp.where(mk["strict"], amat[:CH, LANES:], 0.0),
                                     jnp.where(mk["incl"], amat[CH:, LANES:], 0.0)], axis=0)
        s["rb"] = jnp.where(mk["incl"], amat[CH:, :LANES], 0.0)
    yield
    for s in st:
        xv = mm(s["akrk"], bd(s["it"]["load"]("v")))
        s["x1"], s["y0v"] = xv[:CH], xv[CH:]
    yield
    for s in st:
        s["p"] = jnp.where(s["mk"]["m_d"], s["n"], 0.0)
        s["t"] = cst["eye_cat"] + s["p"]
        s["q"] = mm(s["p"], bd(s["p"]))
    yield
    for _ in range(2):
        for s in st:
            qt = mm(jnp.concatenate([s["q"], s["t"]], axis=0), bd(s["q"]))
            s["t"] = s["t"] + qt[CH:]
            s["q"] = qt[:CH]
        yield
    for s in st:
        s["t"] = s["t"] + mm(s["t"], bd(s["q"]))
    yield
    for key in ("m_o1", "m_o2"):
        for s in st:
            s["x"] = mm(s["t"], bd(jnp.where(s["mk"][key], s["n"], 0.0)))
        yield
        for s in st:
            s["t"] = s["t"] + mm(s["x"], bd(s["t"]))
        yield
    for s in st:
        au = mm(s["t"], jnp.concatenate([bd(s["it"]["load"]("at")), bd(s["x1"])], axis=1))
        s["ah"], s["u0"] = au[:, :LANES], au[:, LANES:]
    yield
    for s in st:
        ld = s["it"]["load"]
        w_tot = s["it"]["w_tot"]()
        ry = mm(s["rb"], jnp.concatenate([bd(s["ah"]), bd(s["u0"])], axis=1))
        rhs2 = jnp.concatenate([jnp.concatenate([s["ah"], s["u0"]], axis=1),
                                jnp.concatenate([jnp.zeros((CH, LANES), F32), ld("v")], axis=1)], axis=0)
        kb_end = jnp.concatenate([ld("bt") * w_tot, ld("kt") * w_tot], axis=0)
        mg = mm(kb_end, rhs2, _TN)
        m = jnp.where(bdmask, mg[:, :LANES], 0.0) + jnp.where(cst["eye128"], w_tot, 0.0)
        g = jnp.where(bdmask, mg[:, LANES:], 0.0)
        s["it"]["store"](ld("rt") + ry[:, :LANES], ry[:, LANES:] + s["y0v"], m, g)
    yield


def _run(gen):
    for _ in gen:
        pass


def _interleave(main, side, ratio):
    for _ in main:
        for _ in range(ratio):
            next(side, None)
    _run(side)


def _chain(gens):
    for g in gens:
        yield from g


def _sigmoid(x):
    return 0.5 * jnp.tanh(0.5 * x) + 0.5


def _shifted(ref, start, n, mp, mn, seq):
    sub = 8
    edge = _iota((sub, LANES), 0)
    cur = ref[start:start + n, :]
    if start > 0:
        prev = ref[start - 1:start - 1 + n, :]
    else:
        head = jnp.where(edge == 0, 0.0, pltpu.roll(ref[0:sub, :], 1, 0))
        prev = jnp.concatenate([head, ref[sub - 1:n - 1, :]], axis=0)
    if start + n < seq:
        nxt = ref[start + 1:start + 1 + n, :]
    else:
        tail = jnp.where(edge == sub - 1, 0.0, pltpu.roll(ref[seq - sub:seq, :], sub - 1, 0))
        nxt = jnp.concatenate([ref[start + 1:seq - sub + 1, :], tail], axis=0)
    return cur * (1.0 - mp - mn) + mp * prev + mn * nxt


def _chunk_cumsum(x, rev):
    n = x.shape[0]
    sub = 8
    pos = _iota((n, LANES), 0) % CH
    s = 1
    while s < sub:
        if rev:
            x = x + jnp.where(pos < CH - s, pltpu.roll(x, n - s, 0), 0.0)
        else:
            x = x + jnp.where(pos >= s, pltpu.roll(x, s, 0), 0.0)
        s *= 2
    g = CH // sub
    x = x.reshape(n // CH, g, sub, LANES)
    s = 1
    while s < g:
        z = jnp.zeros((n // CH, s, sub, LANES), F32)
        if rev:
            x = x + jnp.concatenate([x[:, s:], z], axis=1)
        else:
            x = x + jnp.concatenate([z, x[:, :-s]], axis=1)
        s *= 2
    return x.reshape(n, LANES)


(V_MPR, V_MPK, V_MPV, V_MNR, V_MNK, V_MNV, V_DBF, V_DBB, V_IBF, V_IBB,
 V_KK, V_KA, V_RK, V_LG, V_LB, V_VB) = range(16)
W_DEC, W_ICL, W_VR = range(3)
I_RT, I_AT, I_KT, I_BT = range(4)
I_NAMES = dict(rt=I_RT, at=I_AT, kt=I_KT, bt=I_BT)


def _wkv_kernel(*refs, seq, first):
    if first:
        (r_ref, k_ref, v_ref, ga_ref, dec_ref, icl_ref, vec_ref, mul_ref, wl_ref,
         ya_ref, vf_out_ref, lhs_s, add_s, yf_s, yb_s, bon_s, it_s, v_s, wt_s) = refs
        vres_ref = vf_ref = None
    else:
        (r_ref, k_ref, v_ref, ga_ref, dec_ref, icl_ref, vres_ref, vf_ref, vec_ref, mul_ref, wl_ref,
         ya_ref, lhs_s, add_s, yf_s, yb_s, bon_s, it_s, v_s, wt_s) = refs
        vf_out_ref = None
    ns = seq // SLAB
    per = SLAB // CH
    cst = _wkv_consts()
    ones_bd = jnp.where(cst["bd"], 1.0, 0.0).astype(BF16)

    def vec(i):
        return vec_ref[i:i + 1, :]

    def prologue(sb):
        start = sb * SLAB
        rows = slice(start, start + SLAB)
        r = _shifted(r_ref, start, SLAB, vec(V_MPR), vec(V_MNR), seq)
        yield
        k = _shifted(k_ref, start, SLAB, vec(V_MPK), vec(V_MNK), seq)
        yield
        v = _shifted(v_ref, start, SLAB, vec(V_MPV), vec(V_MNV), seq)
        yield
        dec = _shifted(dec_ref, start, SLAB, mul_ref[0:1, :], mul_ref[2:3, :], seq)
        icl = _shifted(icl_ref, start, SLAB, mul_ref[1:2, :], mul_ref[3:4, :], seq)
        yield
        if first:
            vf_out_ref[rows, :] = v
        else:
            mix = _sigmoid(vec(V_VB) + _mm(vres_ref[rows, :], wl_ref[W_VR], prec=P_LORA)[:, :LANES])
            v = v + (vf_ref[rows, :] - v) * mix
        v_s[rows, :] = v
        yield
        zd = _mm(jnp.tanh(dec), wl_ref[W_DEC], prec=P_LORA)
        zi = _mm(icl, wl_ref[W_ICL], prec=P_LORA)
        kk = k * vec(V_KK)
        yield
        kds, ags, lws = [], [], []
        kka = k * vec(V_KA)
        kbase = k - kka
        for d in range(2):
            z = vec(V_DBF + d) + zd[:, d * LANES:(d + 1) * LANES]
            lws.append(-EXP_M_HALF * _sigmoid(z))
            ag = _sigmoid(vec(V_IBF + d) + zi[:, d * LANES:(d + 1) * LANES])
            kds.append(kbase + kka * ag)
            ags.append(ag)
            yield
        ssq, bsum = _headsum([kk * kk, r * (kds[0] + kds[1]) * vec(V_RK)], ones_bd)
        kk = kk * lax.rsqrt(ssq + 1e-12)
        bon_s[rows, :] = bsum * v
        yield
        for d in range(2):
            cum = _chunk_cumsum(lws[d], d == 1)
            yield
            e_inv = jnp.exp(-cum)
            it_s[d, I_RT, rows, :] = r * jnp.exp(cum)
            it_s[d, I_AT, rows, :] = -kk * jnp.exp(cum - lws[d])
            yield
            it_s[d, I_KT, rows, :] = kds[d] * e_inv
            it_s[d, I_BT, rows, :] = (kk * ags[d]) * e_inv
            for i in range(per):
                c = sb * per + i
                row = i * CH if d == 1 else i * CH + CH - 1
                wt_s[d, c * 8:(c + 1) * 8, :] = jnp.broadcast_to(jnp.exp(cum[row:row + 1, :]), (8, LANES))
            yield

    def chunk_items(step):
        buf = step % 2
        items = []
        for d in range(2):
            slab = step if d == 0 else ns - 1 - step
            for i in range(per):
                c = slab * per + i
                crow = slice(c * CH, (c + 1) * CH)

                def load(name, d=d, crow=crow):
                    if name == "v":
                        return v_s[crow, :]
                    return it_s[d, I_NAMES[name], crow, :]

                def w_tot(d=d, c=c):
                    return wt_s[d, c * 8:c * 8 + 1, :]

                def store(rh, y0, m, g, d=d, i=i):
                    lhs_s[buf, d, i, :CH, :] = rh
                    lhs_s[buf, d, i, CH:, :] = m
                    add_s[buf, d, i, :CH, :] = y0
                    add_s[buf, d, i, CH:, :] = g

                items.append(dict(rev=d == 1, load=load, w_tot=w_tot, store=store))
        return items

    state = [jnp.zeros((LANES, LANES), F32), jnp.zeros((LANES, LANES), F32)]

    def scan(step):
        buf = step % 2
        for j in range(per):
            for d in range(2):
                i = j if d == 0 else per - 1 - j
                slab = step if d == 0 else ns - 1 - step
                o = _mm(lhs_s[buf, d, i], state[d], prec=P_WKV) + add_s[buf, d, i]
                rows = slice((slab * per + i) * CH, (slab * per + i + 1) * CH)
                (yf_s if d == 0 else yb_s)[rows, :] = o[:CH]
                state[d] = o[CH:]
            yield

    def post(start, n):
        rows = slice(start, start + n)
        y = yf_s[rows, :] + yb_s[rows, :]
        mu = _headsum([y], ones_bd)[0] * (1.0 / HEAD)
        yield
        dv = y - mu
        var = _headsum([dv * dv], ones_bd)[0] * (1.0 / HEAD)
        yield
        out = dv * lax.rsqrt(var + GN_EPS) * vec(V_LG) + vec(V_LB) + bon_s[rows, :]
        ga = ga_ref[rows, :]
        ya_ref[rows, :] = (out * (ga * _sigmoid(ga))).astype(BF16)
        yield

    ends = sorted({0, ns - 1})
    inner = [sb for sb in range(ns) if sb not in ends]
    for sb in ends:
        _run(prologue(sb))
    for step in range(ns):
        side = _chain([prologue(sb) for sb in inner]) if step == 0 else scan(step - 1)
        _interleave(_wkv_mxu_steps(chunk_items(step), cst), side, 3 if step == 0 else 1)
    _interleave(scan(ns - 1), _chain([post(sb * SLAB, SLAB) for sb in inner]), 1)
    for sb in ends:
        _run(post(sb * SLAB, SLAB))


def _wkv_call(proj3, vfirst, vecs, mul, wl, *, first):
    bsz, seq, _ = proj3.shape
    nc = seq // CH
    per = SLAB // CH

    def col(base):
        blk = base // LANES
        return pl.BlockSpec((None, seq, LANES), lambda b, p, blk=blk: (b, 0, blk + p))

    def fixed(base):
        blk = base // LANES
        return pl.BlockSpec((None, seq, LANES), lambda b, p, blk=blk: (b, 0, blk))

    in_specs = [col(C_R), col(C_K), col(C_V), col(C_GA), fixed(C_DEC), fixed(C_ICL)]
    args = [proj3] * 6
    if not first:
        in_specs += [fixed(C_VRES), pl.BlockSpec((None, seq, LANES), lambda b, p: (b, 0, p))]
        args += [proj3, vfirst]
    in_specs += [pl.BlockSpec((16, LANES), lambda b, p: (0, p)),
                 pl.BlockSpec((8, LANES), lambda b, p: (0, 0)),
                 pl.BlockSpec((None, 3, LANES, 2 * LANES), lambda b, p: (p, 0, 0, 0))]
    args += [vecs, mul, wl]
    out_block = pl.BlockSpec((None, seq, LANES), lambda b, p: (b, 0, p))
    out_sds = jax.ShapeDtypeStruct((bsz, seq, D), BF16)
    out_shape = (out_sds, jax.ShapeDtypeStruct((bsz, seq, D), F32)) if first else out_sds
    out_specs = (out_block, out_block) if first else out_block
    scratch = [pltpu.VMEM((2, 2, per, CH + LANES, LANES), F32),
               pltpu.VMEM((2, 2, per, CH + LANES, LANES), F32),
               pltpu.VMEM((seq, LANES), F32), pltpu.VMEM((seq, LANES), F32), pltpu.VMEM((seq, LANES), F32),
               pltpu.VMEM((2, 4, seq, LANES), F32),
               pltpu.VMEM((seq, LANES), F32),
               pltpu.VMEM((2, nc * 8, LANES), F32)]
    return pl.pallas_call(
        functools.partial(_wkv_kernel, seq=seq, first=first),
        grid=(bsz, D // LANES),
        in_specs=in_specs, out_specs=out_specs, out_shape=out_shape,
        scratch_shapes=scratch,
        compiler_params=pltpu.CompilerParams(dimension_semantics=("arbitrary", "arbitrary"),
                                             vmem_limit_bytes=VMEM_LIMIT),
        name="wkv7",
    )(*args)


def _rms(x, g):
    ms = jnp.mean(x * x, axis=-1, keepdims=True)
    return x * lax.rsqrt(ms + NORM_EPS) * g


def _proj_kernel(xn_ref, w_ref, o_ref):
    o_ref[...] = jnp.dot(xn_ref[...], w_ref[...], preferred_element_type=F32)


def _proj_call(xn2, w):
    t = xn2.shape[0]
    tm = min(1024, t)
    tn = 1920
    return pl.pallas_call(
        _proj_kernel,
        grid=(NPAD // tn, t // tm),
        in_specs=[pl.BlockSpec((tm, D), lambda j, i: (i, 0)),
                  pl.BlockSpec((D, tn), lambda j, i: (0, j))],
        out_specs=pl.BlockSpec((tm, tn), lambda j, i: (i, j)),
        out_shape=jax.ShapeDtypeStruct((t, NPAD), F32),
        compiler_params=pltpu.CompilerParams(dimension_semantics=("arbitrary", "arbitrary"),
                                             vmem_limit_bytes=VMEM_LIMIT),
        name="proj",
    )(xn2, w)


def _ret_kernel(q_ref, k_ref, v_ref, gb_ref, cos_ref, sin_ref, lg_ref, gain_ref, o_ref, qr_s, kr_s, y_s, *, seq):
    nc = seq // RET_C
    p = pl.program_id(1)
    scale = HEAD ** -0.5

    def rot(c, _):
        sl = pl.ds(pl.multiple_of(c * RET_C, RET_C), RET_C)
        cs, sn = cos_ref[sl, :], sin_ref[sl, :]
        q = q_ref[sl, :]
        k = k_ref[sl, :]
        qr_s[sl, :] = (q * cs + pltpu.roll(q, 64, 1) * sn) * scale
        kr_s[sl, :] = k * cs + pltpu.roll(k, 64, 1) * sn
        return 0

    lax.fori_loop(0, nc, rot, 0)

    lane = _iota((RET_C, LANES), 1)
    rowi = _iota((RET_C, LANES), 0).astype(F32)
    absd = jnp.abs(_iota((RET_C, RET_C), 0) - _iota((RET_C, RET_C), 1)).astype(F32)
    for j in range(2):
        hmask = ((lane // 32) % 2) == j
        lg = lg_ref[pl.ds(2 * p + j, 1), :]
        lg128 = lg[:, :LANES]
        dmat = jnp.exp(absd * lg)
        xi_f = jnp.exp((rowi + 1.0) * lg128)
        ze_f = jnp.exp((RET_C - 1.0 - rowi) * lg128)
        xi_b = jnp.exp((RET_C - rowi) * lg128)
        ze_b = jnp.exp(rowi * lg128)
        cdec = jnp.exp(RET_C * lg128)
        vs = slice(j * LANES, (j + 1) * LANES)
        fst = jnp.zeros((LANES, LANES), F32)
        for i in range(nc):
            sl = slice(i * RET_C, (i + 1) * RET_C)
            qh = jnp.where(hmask, qr_s[sl, :], 0.0)
            kh = kr_s[sl, :]
            vh = v_ref[sl, vs]
            sc = _mm(qh, kh, _NT) * dmat
            y = _mm(sc, vh)
            if i > 0:
                y = y + _mm(qh * xi_f, fst)
            if i < nc - 1:
                fst = fst * cdec + _mm(jnp.where(hmask, kh, 0.0) * ze_f, vh, _TN)
            y_s[sl, vs] = y
        gst = jnp.zeros((LANES, LANES), F32)
        for i in range(nc - 1, -1, -1):
            sl = slice(i * RET_C, (i + 1) * RET_C)
            y = y_s[sl, vs]
            if i < nc - 1:
                qh = jnp.where(hmask, qr_s[sl, :], 0.0)
                y = y + _mm(qh * xi_b, gst)
            if i > 0:
                kh = jnp.where(hmask, kr_s[sl, :], 0.0)
                gst = gst * cdec + _mm(kh * ze_b, v_ref[sl, vs], _TN)
            mu = jnp.mean(y, axis=-1, keepdims=True)
            dv = y - mu
            var = jnp.mean(dv * dv, axis=-1, keepdims=True)
            out = dv * lax.rsqrt(var + NORM_EPS) * gain_ref[:, vs]
            o_ref[sl, vs] = (out * jax.nn.silu(gb_ref[sl, vs])).astype(BF16)


def _ret_call(proj3, cos_t, sin_t, lg_t, gain):
    bsz, seq, _ = proj3.shape
    w2 = 2 * LANES

    def spec(base, width):
        blk = base // width
        return pl.BlockSpec((None, seq, width), lambda b, p, blk=blk: (b, 0, blk + p))

    return pl.pallas_call(
        functools.partial(_ret_kernel, seq=seq),
        grid=(bsz, RET_HEADS // 2),
        in_specs=[spec(C_QB, LANES), spec(C_KB, LANES), spec(C_VB, w2), spec(C_GB, w2),
                  pl.BlockSpec((seq, LANES), lambda b, p: (0, 0)),
                  pl.BlockSpec((seq, LANES), lambda b, p: (0, 0)),
                  pl.BlockSpec((RET_HEADS, w2), lambda b, p: (0, 0)),
                  pl.BlockSpec((1, w2), lambda b, p: (0, p))],
        out_specs=pl.BlockSpec((None, seq, w2), lambda b, p: (b, 0, p)),
        out_shape=jax.ShapeDtypeStruct((bsz, seq, D), BF16),
        scratch_shapes=[pltpu.VMEM((seq, LANES), F32), pltpu.VMEM((seq, LANES), F32), pltpu.VMEM((seq, w2), F32)],
        compiler_params=pltpu.CompilerParams(dimension_semantics=("arbitrary", "arbitrary"),
                                             vmem_limit_bytes=VMEM_LIMIT),
        name="retention",
    )(proj3, proj3, proj3, proj3, cos_t, sin_t, lg_t, gain)


def _merge_kernel(x_ref, ya_ref, yb_ref, ma_ref, mb_ref, wa_ref, wb_ref, wo_ref, g_ref, *o_refs, last):
    za = jnp.dot(ya_ref[...], wa_ref[...], preferred_element_type=F32)
    zb = jnp.dot(yb_ref[...], wb_ref[...], preferred_element_type=F32)
    merged = jax.nn.sigmoid(ma_ref[...]) * za + jax.nn.sigmoid(mb_ref[...]) * zb
    x = x_ref[...] + jnp.dot(merged.astype(BF16), wo_ref[...], preferred_element_type=F32)
    if last:
        o_refs[0][...] = _rms(x, g_ref[...])
    else:
        o_refs[0][...] = x
        o_refs[1][...] = _rms(x, g_ref[...]).astype(BF16)


def _merge_call(x2, ya2, yb2, proj2, wa, wb, wo, gain, *, last):
    t = x2.shape[0]
    tm = min(512, t)
    row = pl.BlockSpec((tm, D), lambda i: (i, 0))
    wspec = pl.BlockSpec((D, D), lambda i: (0, 0))
    x_sds = jax.ShapeDtypeStruct((t, D), F32)
    return pl.pallas_call(
        functools.partial(_merge_kernel, last=last),
        grid=(t // tm,),
        in_specs=[row, row, row,
                  pl.BlockSpec((tm, D), lambda i: (i, C_MA // D)),
                  pl.BlockSpec((tm, D), lambda i: (i, C_MB // D)),
                  wspec, wspec, wspec, pl.BlockSpec((1, D), lambda i: (0, 0))],
        out_specs=row if last else (row, row),
        out_shape=x_sds if last else (x_sds, jax.ShapeDtypeStruct((t, D), BF16)),
        compiler_params=pltpu.CompilerParams(dimension_semantics=("arbitrary",),
                                             vmem_limit_bytes=VMEM_LIMIT),
        name="merge",
    )(x2, ya2, yb2, proj2, proj2, wa, wb, wo, gain)


def _norm_kernel(x_ref, g_ref, o_ref):
    o_ref[...] = _rms(x_ref[...], g_ref[...]).astype(BF16)


def _norm_call(x2, gain):
    t = x2.shape[0]
    tm = min(1024, t)
    return pl.pallas_call(
        _norm_kernel,
        grid=(t // tm,),
        in_specs=[pl.BlockSpec((tm, D), lambda i: (i, 0)), pl.BlockSpec((1, D), lambda i: (0, 0))],
        out_specs=pl.BlockSpec((tm, D), lambda i: (i, 0)),
        out_shape=jax.ShapeDtypeStruct((t, D), BF16),
        compiler_params=pltpu.CompilerParams(dimension_semantics=("arbitrary",)),
        name="first_norm",
    )(x2, gain)


def _qk_perm():
    idx = []
    for pair in range(RET_HEADS // 2):
        for lane in range(LANES):
            seg, i = lane // 32, lane % 32
            head, half = 2 * pair + seg % 2, seg // 2
            idx.append(head * HEAD + half * 32 + i)
    return np.asarray(idx, np.int32)


def _pack_w_in(w_in, w_vres_down):
    depth = w_in.shape[0]
    perm = _qk_perm()
    vres = jnp.concatenate([jnp.zeros((1, D, 32), F32), w_vres_down], axis=0)
    vres = jnp.pad(vres, ((0, 0), (0, 0), (0, LANES - 32)))
    parts = [w_in[:, :, 0:3072], w_in[:, :, 3328:4352], w_in[:, :, 5376:6400], w_in[:, :, 6400:7424],
             w_in[:, :, 7424:8448], w_in[:, :, 8448:9472],
             w_in[:, :, 4352:4864][:, :, perm], w_in[:, :, 4864:5376][:, :, perm],
             w_in[:, :, 3072:3328], vres]
    w = jnp.concatenate(parts, axis=2)
    assert w.shape == (depth, D, NPAD)
    return w.astype(BF16)


def _pair_cat(wf, wb):
    pairs = D // LANES
    cat = jnp.concatenate([wf.reshape(LANES, pairs, LANES), wb.reshape(LANES, pairs, LANES)], axis=2)
    return cat.transpose(1, 0, 2)


def _rope_tables(seq):
    half = 32
    freqs = jnp.power(ROPE_BASE, -jnp.arange(half, dtype=F32) / half)
    ang = jnp.arange(seq, dtype=F32)[:, None] * freqs[None, :]
    cos = jnp.tile(jnp.cos(ang), (1, 4))
    sin = jnp.tile(jnp.sin(ang), (1, 4))
    sign = jnp.where(jnp.arange(LANES) < 64, -1.0, 1.0).astype(F32)
    return cos, sin * sign[None, :]


def kernel(x, norm_gain, w_in, w_vres_down, shift_prev, shift_next, w_decay_up, decay_bias, w_iclr_up, iclr_bias,
           w_vres_up, vres_bias, k_k, k_a, r_k, lnx_gain, lnx_bias, w_branch_a, ret_norm_gain, w_branch_b, w_out,
           final_gain):
    bsz, seq, _ = x.shape
    depth = w_in.shape[0]
    assert seq % RET_C == 0 and seq % SLAB == 0 and SLAB % CH == 0
    t = bsz * seq

    w_all = _pack_w_in(w_in, w_vres_down)
    wa_all, wb_all, wo_all = (w.astype(BF16) for w in (w_branch_a, w_branch_b, w_out))
    cos_t, sin_t = _rope_tables(seq)
    log_g = jnp.log1p(-jnp.exp2(-5.0 - jnp.arange(RET_HEADS, dtype=F32)))
    lg_t = jnp.broadcast_to(log_g[:, None], (RET_HEADS, 2 * LANES))
    zpad = jnp.zeros((64, D), F32)
    zmat = jnp.zeros((LANES, D), F32)

    x2 = x.reshape(t, D)
    xn2 = _norm_call(x2, norm_gain[0][None, :])
    vfirst = None
    for l in range(depth):
        proj2 = _proj_call(xn2, w_all[l])
        proj3 = proj2.reshape(bsz, seq, NPAD)

        vb = vres_bias[l - 1] if l > 0 else jnp.zeros((D,), F32)
        vecs = jnp.stack([shift_prev[l, 0:1024], shift_prev[l, 1024:2048], shift_prev[l, 2048:3072],
                          shift_next[l, 0:1024], shift_next[l, 1024:2048], shift_next[l, 2048:3072],
                          decay_bias[l, 0], decay_bias[l, 1], iclr_bias[l, 0], iclr_bias[l, 1],
                          k_k[l], k_a[l], r_k[l].reshape(D), lnx_gain[l], lnx_bias[l], vb], axis=0)
        mul = jnp.concatenate([jnp.stack([shift_prev[l, 3072:3200], shift_prev[l, 3200:3328],
                                          shift_next[l, 3072:3200], shift_next[l, 3200:3328]], axis=0),
                               jnp.zeros((4, LANES), F32)], axis=0)
        wvr = jnp.pad(w_vres_up[l - 1], ((0, LANES - 32), (0, 0))) if l > 0 else zmat
        wl = jnp.stack([_pair_cat(jnp.concatenate([w_decay_up[l, 0], zpad], axis=0),
                                  jnp.concatenate([zpad, w_decay_up[l, 1]], axis=0)),
                        _pair_cat(jnp.concatenate([w_iclr_up[l, 0], zpad], axis=0),
                                  jnp.concatenate([zpad, w_iclr_up[l, 1]], axis=0)),
                        _pair_cat(wvr, zmat)], axis=1)

        if l == 0:
            ya, vfirst = _wkv_call(proj3, None, vecs, mul, wl, first=True)
        else:
            ya = _wkv_call(proj3, vfirst, vecs, mul, wl, first=False)
        yb = _ret_call(proj3, cos_t, sin_t, lg_t, ret_norm_gain[l][None, :])
        last = l == depth - 1
        gain = final_gain if last else norm_gain[l + 1]
        res = _merge_call(x2, ya.reshape(t, D), yb.reshape(t, D), proj2, wa_all[l], wb_all[l], wo_all[l],
                          gain[None, :], last=last)
        if not last:
            x2, xn2 = res

    return res.reshape(bsz, seq, D)
```

```python
import functools

import numpy as np
import jax
import jax.numpy as jnp
from jax import lax
from jax.experimental import pallas as pl
from jax.experimental.pallas import tpu as pltpu

F32 = jnp.float32
BF16 = jnp.bfloat16

D = 1024
HEAD = 64
LANES = 128
CH = 64
SLAB = 512
RET_HEADS = 8
RET_C = 256
NORM_EPS = 1e-6
GN_EPS = 64e-5
EXP_M_HALF = 0.6065306597126334
ROPE_BASE = 10000.0
VMEM_LIMIT = 56 * 1024 * 1024

C_R, C_K, C_V, C_GA, C_DEC, C_ICL, C_VRES, N_WKV = 0, 1024, 2048, 3072, 4096, 4224, 4352, 4480
C_VB, C_GB, C_MA, C_MB, C_QB, C_KB, N_REST = 0, 1024, 2048, 3072, 4096, 4608, 5120

P_LORA = 1
P_WKV = 1

_NN = (((1,), (0,)), ((), ()))
_NT = (((1,), (1,)), ((), ()))
_TN = (((0,), (0,)), ((), ()))


def _dg(a, b, dims):
    return lax.dot_general(a, b, dims, preferred_element_type=F32)


def _split2(x):
    hi = x.astype(BF16)
    lo = (x - hi.astype(F32)).astype(BF16)
    return hi, lo


def _mm(a, b, dims=_NN, prec=1):
    if prec == 1:
        return _dg(a.astype(BF16), b.astype(BF16), dims)
    ah, al = _split2(a)
    bh, bl = _split2(b)
    return _dg(ah, bh, dims) + (_dg(ah, bl, dims) + _dg(al, bh, dims))


def _headsum(xs, ones_bd):
    return [_dg(x.astype(BF16), ones_bd, _NN) for x in xs]


def _iota(shape, axis):
    return lax.broadcasted_iota(jnp.int32, shape, axis)


def _pair_bd(x, bdmask):
    return jnp.where(bdmask, jnp.concatenate([x, x], axis=0), 0.0)


def _wkv_consts():
    row = _iota((CH, LANES), 0)
    col = _iota((CH, LANES), 1) % CH
    r128 = _iota((LANES, LANES), 0)
    c128 = _iota((LANES, LANES), 1)

    def masks(lower):
        late, early = (row, col) if lower else (col, row)
        return dict(
            strict=early < late, incl=early <= late,
            m_d=(early < late) & ((row // 16) == (col // 16)),
            m_o1=((row // 32) == (col // 32)) & ((late // 16) % 2 == 1) & ((early // 16) % 2 == 0),
            m_o2=((late // 32) == 1) & ((early // 32) == 0))

    return dict(bd=(r128 // CH) == (c128 // CH), eye128=r128 == c128,
                eye_cat=jnp.where(row == col, 1.0, 0.0).astype(F32),
                fwd=masks(True), bwd=masks(False))


def _wkv_mxu_steps(items, cst):
    bdmask = cst["bd"]

    def bd(x):
        return _pair_bd(x, bdmask)

    def mm(a, b, dims=_NN):
        return _mm(a, b, dims, prec=P_WKV)

    st = [dict(it=it, mk=cst["bwd"] if it["rev"] else cst["fwd"]) for it in items]
    for s in st:
        ld = s["it"]["load"]
        amat = mm(jnp.concatenate([ld("at"), ld("rt")], axis=0),
                  jnp.concatenate([bd(ld("bt")), bd(ld("kt"))], axis=0), _NT)
        mk = s["mk"]
        s["n"] = amat[:CH, :LANES]
        s["akrk"] = jnp.concatenate([jnp.where(mk["strict"], amat[:CH, LANES:], 0.0),
                                     jnp.where(mk["incl"], amat[CH:, LANES:], 0.0)], axis=0)
        s["rb"] = jnp.where(mk["incl"], amat[CH:, :LANES], 0.0)
    yield
    for s in st:
        xv = mm(s["akrk"], bd(s["it"]["load"]("v")))
        s["x1"], s["y0v"] = xv[:CH], xv[CH:]
    yield
    for s in st:
        s["p"] = jnp.where(s["mk"]["m_d"], s["n"], 0.0)
        s["t"] = cst["eye_cat"] + s["p"]
        s["q"] = mm(s["p"], bd(s["p"]))
    yield
    for _ in range(2):
        for s in st:
            qt = mm(jnp.concatenate([s["q"], s["t"]], axis=0), bd(s["q"]))
            s["t"] = s["t"] + qt[CH:]
            s["q"] = qt[:CH]
        yield
    for s in st:
        s["t"] = s["t"] + mm(s["t"], bd(s["q"]))
    yield
    for key in ("m_o1", "m_o2"):
        for s in st:
            s["x"] = mm(s["t"], bd(jnp.where(s["mk"][key], s["n"], 0.0)))
        yield
        for s in st:
            s["t"] = s["t"] + mm(s["x"], bd(s["t"]))
        yield
    for s in st:
        au = mm(s["t"], jnp.concatenate([bd(s["it"]["load"]("at")), bd(s["x1"])], axis=1))
        s["ah"], s["u0"] = au[:, :LANES], au[:, LANES:]
    yield
    for s in st:
        ld = s["it"]["load"]
        w_tot = s["it"]["w_tot"]()
        ry = mm(s["rb"], jnp.concatenate([bd(s["ah"]), bd(s["u0"])], axis=1))
        rhs2 = jnp.concatenate([jnp.concatenate([s["ah"], s["u0"]], axis=1),
                                jnp.concatenate([jnp.zeros((CH, LANES), F32), ld("v")], axis=1)], axis=0)
        kb_end = jnp.concatenate([ld("bt") * w_tot, ld("kt") * w_tot], axis=0)
        mg = mm(kb_end, rhs2, _TN)
        m = jnp.where(bdmask, mg[:, :LANES], 0.0) + jnp.where(cst["eye128"], w_tot, 0.0)
        g = jnp.where(bdmask, mg[:, LANES:], 0.0)
        s["it"]["store"](ld("rt") + ry[:, :LANES], ry[:, LANES:] + s["y0v"], m, g)
    yield


def _run(gen):
    for _ in gen:
        pass


def _interleave(main, side, ratio):
    for _ in main:
        for _ in range(ratio):
            next(side, None)
    _run(side)


def _chain(gens):
    for g in gens:
        yield from g


def _sigmoid(x):
    return 0.5 * jnp.tanh(0.5 * x) + 0.5


def _shifted(ref, start, n, mp, mn, seq):
    sub = 8
    edge = _iota((sub, LANES), 0)
    cur = ref[start:start + n, :]
    if start > 0:
        prev = ref[start - 1:start - 1 + n, :]
    else:
        head = jnp.where(edge == 0, 0.0, pltpu.roll(ref[0:sub, :], 1, 0))
        prev = jnp.concatenate([head, ref[sub - 1:n - 1, :]], axis=0)
    if start + n < seq:
        nxt = ref[start + 1:start + 1 + n, :]
    else:
        tail = jnp.where(edge == sub - 1, 0.0, pltpu.roll(ref[seq - sub:seq, :], sub - 1, 0))
        nxt = jnp.concatenate([ref[start + 1:seq - sub + 1, :], tail], axis=0)
    return cur * (1.0 - mp - mn) + mp * prev + mn * nxt


def _chunk_cumsum(x, rev):
    n = x.shape[0]
    sub = 8
    pos = _iota((n, LANES), 0) % CH
    s = 1
    while s < sub:
        if rev:
            x = x + jnp.where(pos < CH - s, pltpu.roll(x, n - s, 0), 0.0)
        else:
            x = x + jnp.where(pos >= s, pltpu.roll(x, s, 0), 0.0)
        s *= 2
    g = CH // sub
    x = x.reshape(n // CH, g, sub, LANES)
    s = 1
    while s < g:
        z = jnp.zeros((n // CH, s, sub, LANES), F32)
        if rev:
            x = x + jnp.concatenate([x[:, s:], z], axis=1)
        else:
            x = x + jnp.concatenate([z, x[:, :-s]], axis=1)
        s *= 2
    return x.reshape(n, LANES)


(V_MPR, V_MPK, V_MPV, V_MNR, V_MNK, V_MNV, V_DBF, V_DBB, V_IBF, V_IBB,
 V_KK, V_KA, V_RK, V_LG, V_LB, V_VB) = range(16)
W_DEC, W_ICL, W_VR = range(3)
I_RT, I_AT, I_KT, I_BT = range(4)
I_NAMES = dict(rt=I_RT, at=I_AT, kt=I_KT, bt=I_BT)


def _wkv_kernel(*refs, seq, first):
    if first:
        (r_ref, k_ref, v_ref, ga_ref, dec_ref, icl_ref, vec_ref, mul_ref, wl_ref,
         ya_ref, vf_out_ref, lhs_s, add_s, yf_s, yb_s, bon_s, it_s, v_s, wt_s) = refs
        vres_ref = vf_ref = None
    else:
        (r_ref, k_ref, v_ref, ga_ref, dec_ref, icl_ref, vres_ref, vf_ref, vec_ref, mul_ref, wl_ref,
         ya_ref, lhs_s, add_s, yf_s, yb_s, bon_s, it_s, v_s, wt_s) = refs
        vf_out_ref = None
    ns = seq // SLAB
    per = SLAB // CH
    cst = _wkv_consts()
    ones_bd = jnp.where(cst["bd"], 1.0, 0.0).astype(BF16)

    def vec(i):
        return vec_ref[i:i + 1, :]

    def prologue(sb):
        start = sb * SLAB
        rows = slice(start, start + SLAB)
        r = _shifted(r_ref, start, SLAB, vec(V_MPR), vec(V_MNR), seq)
        yield
        k = _shifted(k_ref, start, SLAB, vec(V_MPK), vec(V_MNK), seq)
        yield
        v = _shifted(v_ref, start, SLAB, vec(V_MPV), vec(V_MNV), seq)
        yield
        dec = _shifted(dec_ref, start, SLAB, mul_ref[0:1, :], mul_ref[2:3, :], seq)
        icl = _shifted(icl_ref, start, SLAB, mul_ref[1:2, :], mul_ref[3:4, :], seq)
        yield
        if first:
            vf_out_ref[rows, :] = v
        else:
            mix = _sigmoid(vec(V_VB) + _mm(vres_ref[rows, :], wl_ref[W_VR], prec=P_LORA)[:, :LANES])
            v = v + (vf_ref[rows, :] - v) * mix
        v_s[rows, :] = v
        yield
        zd = _mm(jnp.tanh(dec), wl_ref[W_DEC], prec=P_LORA)
        zi = _mm(icl, wl_ref[W_ICL], prec=P_LORA)
        kk = k * vec(V_KK)
        yield
        kds, ags, lws = [], [], []
        kka = k * vec(V_KA)
        kbase = k - kka
        for d in range(2):
            z = vec(V_DBF + d) + zd[:, d * LANES:(d + 1) * LANES]
            lws.append(-EXP_M_HALF * _sigmoid(z))
            ag = _sigmoid(vec(V_IBF + d) + zi[:, d * LANES:(d + 1) * LANES])
            kds.append(kbase + kka * ag)
            ags.append(ag)
            yield
        ssq, bsum = _headsum([kk * kk, r * (kds[0] + kds[1]) * vec(V_RK)], ones_bd)
        kk = kk * lax.rsqrt(ssq + 1e-12)
        bon_s[rows, :] = bsum * v
        yield
        for d in range(2):
            cum = _chunk_cumsum(lws[d], d == 1)
            yield
            e_inv = jnp.exp(-cum)
            it_s[d, I_RT, rows, :] = r * jnp.exp(cum)
            it_s[d, I_AT, rows, :] = -kk * jnp.exp(cum - lws[d])
            yield
            it_s[d, I_KT, rows, :] = kds[d] * e_inv
            it_s[d, I_BT, rows, :] = (kk * ags[d]) * e_inv
            for i in range(per):
                c = sb * per + i
                row = i * CH if d == 1 else i * CH + CH - 1
                wt_s[d, c * 8:(c + 1) * 8, :] = jnp.broadcast_to(jnp.exp(cum[row:row + 1, :]), (8, LANES))
            yield

    def chunk_items(step):
        buf = step % 2
        items = []
        for d in range(2):
            slab = step if d == 0 else ns - 1 - step
            for i in range(per):
                c = slab * per + i
                crow = slice(c * CH, (c + 1) * CH)

                def load(name, d=d, crow=crow):
                    if name == "v":
                        return v_s[crow, :]
                    return it_s[d, I_NAMES[name], crow, :]

                def w_tot(d=d, c=c):
                    return wt_s[d, c * 8:c * 8 + 1, :]

                def store(rh, y0, m, g, d=d, i=i):
                    lhs_s[buf, d, i, :CH, :] = rh
                    lhs_s[buf, d, i, CH:, :] = m
                    add_s[buf, d, i, :CH, :] = y0
                    add_s[buf, d, i, CH:, :] = g

                items.append(dict(rev=d == 1, load=load, w_tot=w_tot, store=store))
        return items

    state = [jnp.zeros((LANES, LANES), F32), jnp.zeros((LANES, LANES), F32)]

    def scan(step):
        buf = step % 2
        for j in range(per):
            for d in range(2):
                i = j if d == 0 else per - 1 - j
                slab = step if d == 0 else ns - 1 - step
                o = _mm(lhs_s[buf, d, i], state[d], prec=P_WKV) + add_s[buf, d, i]
                rows = slice((slab * per + i) * CH, (slab * per + i + 1) * CH)
                (yf_s if d == 0 else yb_s)[rows, :] = o[:CH]
                state[d] = o[CH:]
            yield

    def post(start, n):
        rows = slice(start, start + n)
        y = yf_s[rows, :] + yb_s[rows, :]
        mu = _headsum([y], ones_bd)[0] * (1.0 / HEAD)
        yield
        dv = y - mu
        var = _headsum([dv * dv], ones_bd)[0] * (1.0 / HEAD)
        yield
        out = dv * lax.rsqrt(var + GN_EPS) * vec(V_LG) + vec(V_LB) + bon_s[rows, :]
        ga = ga_ref[rows, :]
        ya_ref[rows, :] = (out * (ga * _sigmoid(ga))).astype(BF16)
        yield

    ends = sorted({0, ns - 1})
    inner = [sb for sb in range(ns) if sb not in ends]
    for sb in ends:
        _run(prologue(sb))
    for step in range(ns):
        side = _chain([prologue(sb) for sb in inner]) if step == 0 else scan(step - 1)
        _interleave(_wkv_mxu_steps(chunk_items(step), cst), side, 3 if step == 0 else 1)
    _interleave(scan(ns - 1), _chain([post(sb * SLAB, SLAB) for sb in inner]), 1)
    for sb in ends:
        _run(post(sb * SLAB, SLAB))


def _wkv_call(proj3, vfirst, vecs, mul, wl, *, first):
    bsz, seq, _ = proj3.shape
    nc = seq // CH
    per = SLAB // CH

    def col(base):
        blk = base // LANES
        return pl.BlockSpec((None, seq, LANES), lambda b, p, blk=blk: (b, 0, blk + p))

    def fixed(base):
        blk = base // LANES
        return pl.BlockSpec((None, seq, LANES), lambda b, p, blk=blk: (b, 0, blk))

    in_specs = [col(C_R), col(C_K), col(C_V), col(C_GA), fixed(C_DEC), fixed(C_ICL)]
    args = [proj3] * 6
    if not first:
        in_specs += [fixed(C_VRES), pl.BlockSpec((None, seq, LANES), lambda b, p: (b, 0, p))]
        args += [proj3, vfirst]
    in_specs += [pl.BlockSpec((16, LANES), lambda b, p: (0, p)),
                 pl.BlockSpec((8, LANES), lambda b, p: (0, 0)),
                 pl.BlockSpec((None, 3, LANES, 2 * LANES), lambda b, p: (p, 0, 0, 0))]
    args += [vecs, mul, wl]
    out_block = pl.BlockSpec((None, seq, LANES), lambda b, p: (b, 0, p))
    out_sds = jax.ShapeDtypeStruct((bsz, seq, D), BF16)
    out_shape = (out_sds, jax.ShapeDtypeStruct((bsz, seq, D), F32)) if first else out_sds
    out_specs = (out_block, out_block) if first else out_block
    scratch = [pltpu.VMEM((2, 2, per, CH + LANES, LANES), F32),
               pltpu.VMEM((2, 2, per, CH + LANES, LANES), F32),
               pltpu.VMEM((seq, LANES), F32), pltpu.VMEM((seq, LANES), F32), pltpu.VMEM((seq, LANES), F32),
               pltpu.VMEM((2, 4, seq, LANES), F32),
               pltpu.VMEM((seq, LANES), F32),
               pltpu.VMEM((2, nc * 8, LANES), F32)]
    return pl.pallas_call(
        functools.partial(_wkv_kernel, seq=seq, first=first),
        grid=(bsz, D // LANES),
        in_specs=in_specs, out_specs=out_specs, out_shape=out_shape,
        scratch_shapes=scratch,
        compiler_params=pltpu.CompilerParams(dimension_semantics=("arbitrary", "arbitrary"),
                                             vmem_limit_bytes=VMEM_LIMIT),
        name="wkv7",
    )(*args)


def _rms(x, g):
    ms = jnp.mean(x * x, axis=-1, keepdims=True)
    return x * lax.rsqrt(ms + NORM_EPS) * g


def _proj_kernel(xn_ref, w_ref, o_ref):
    o_ref[...] = jnp.dot(xn_ref[...], w_ref[...], preferred_element_type=F32).astype(o_ref.dtype)


def _proj_call(xn2, w, out_dtype):
    t = xn2.shape[0]
    n = w.shape[1]
    tm = min(1024, t)
    tn = max(c for c in range(LANES, 2560 + 1, LANES) if n % c == 0)
    return pl.pallas_call(
        _proj_kernel,
        grid=(n // tn, t // tm),
        in_specs=[pl.BlockSpec((tm, D), lambda j, i: (i, 0)),
                  pl.BlockSpec((D, tn), lambda j, i: (0, j))],
        out_specs=pl.BlockSpec((tm, tn), lambda j, i: (i, j)),
        out_shape=jax.ShapeDtypeStruct((t, n), out_dtype),
        compiler_params=pltpu.CompilerParams(dimension_semantics=("arbitrary", "arbitrary"),
                                             vmem_limit_bytes=VMEM_LIMIT),
        name="proj",
    )(xn2, w)


def _ret_kernel(q_ref, k_ref, v_ref, gb_ref, cos_ref, sin_ref, lg_ref, gain_ref, o_ref, qr_s, kr_s, y_s, *, seq):
    nc = seq // RET_C
    p = pl.program_id(1)
    scale = HEAD ** -0.5

    def rot(c, _):
        sl = pl.ds(pl.multiple_of(c * RET_C, RET_C), RET_C)
        cs, sn = cos_ref[sl, :], sin_ref[sl, :]
        q = q_ref[sl, :].astype(F32)
        k = k_ref[sl, :].astype(F32)
        qr_s[sl, :] = (q * cs + pltpu.roll(q, 64, 1) * sn) * scale
        kr_s[sl, :] = k * cs + pltpu.roll(k, 64, 1) * sn
        return 0

    lax.fori_loop(0, nc, rot, 0)

    lane = _iota((RET_C, LANES), 1)
    rowi = _iota((RET_C, LANES), 0).astype(F32)
    absd = jnp.abs(_iota((RET_C, RET_C), 0) - _iota((RET_C, RET_C), 1)).astype(F32)
    for j in range(2):
        hmask = ((lane // 32) % 2) == j
        lg = lg_ref[pl.ds(2 * p + j, 1), :]
        lg128 = lg[:, :LANES]
        dmat = jnp.exp(absd * lg)
        xi_f = jnp.exp((rowi + 1.0) * lg128)
        ze_f = jnp.exp((RET_C - 1.0 - rowi) * lg128)
        xi_b = jnp.exp((RET_C - rowi) * lg128)
        ze_b = jnp.exp(rowi * lg128)
        cdec = jnp.exp(RET_C * lg128)
        vs = slice(j * LANES, (j + 1) * LANES)
        fst = jnp.zeros((LANES, LANES), F32)
        for i in range(nc):
            sl = slice(i * RET_C, (i + 1) * RET_C)
            qh = jnp.where(hmask, qr_s[sl, :], 0.0)
            kh = kr_s[sl, :]
            vh = v_ref[sl, vs]
            sc = _mm(qh, kh, _NT) * dmat
            y = _mm(sc, vh)
            if i > 0:
                y = y + _mm(qh * xi_f, fst)
            if i < nc - 1:
                fst = fst * cdec + _mm(jnp.where(hmask, kh, 0.0) * ze_f, vh, _TN)
            y_s[sl, vs] = y
        gst = jnp.zeros((LANES, LANES), F32)
        for i in range(nc - 1, -1, -1):
            sl = slice(i * RET_C, (i + 1) * RET_C)
            y = y_s[sl, vs]
            if i < nc - 1:
                qh = jnp.where(hmask, qr_s[sl, :], 0.0)
                y = y + _mm(qh * xi_b, gst)
            if i > 0:
                kh = jnp.where(hmask, kr_s[sl, :], 0.0)
                gst = gst * cdec + _mm(kh * ze_b, v_ref[sl, vs], _TN)
            mu = jnp.mean(y, axis=-1, keepdims=True)
            dv = y - mu
            var = jnp.mean(dv * dv, axis=-1, keepdims=True)
            out = dv * lax.rsqrt(var + NORM_EPS) * gain_ref[:, vs]
            o_ref[sl, vs] = (out * jax.nn.silu(gb_ref[sl, vs].astype(F32))).astype(BF16)


def _ret_call(proj3, cos_t, sin_t, lg_t, gain):
    bsz, seq, _ = proj3.shape
    w2 = 2 * LANES

    def spec(base, width):
        blk = base // width
        return pl.BlockSpec((None, seq, width), lambda b, p, blk=blk: (b, 0, blk + p))

    return pl.pallas_call(
        functools.partial(_ret_kernel, seq=seq),
        grid=(bsz, RET_HEADS // 2),
        in_specs=[spec(C_QB, LANES), spec(C_KB, LANES), spec(C_VB, w2), spec(C_GB, w2),
                  pl.BlockSpec((seq, LANES), lambda b, p: (0, 0)),
                  pl.BlockSpec((seq, LANES), lambda b, p: (0, 0)),
                  pl.BlockSpec((RET_HEADS, w2), lambda b, p: (0, 0)),
                  pl.BlockSpec((1, w2), lambda b, p: (0, p))],
        out_specs=pl.BlockSpec((None, seq, w2), lambda b, p: (b, 0, p)),
        out_shape=jax.ShapeDtypeStruct((bsz, seq, D), BF16),
        scratch_shapes=[pltpu.VMEM((seq, LANES), F32), pltpu.VMEM((seq, LANES), F32), pltpu.VMEM((seq, w2), F32)],
        compiler_params=pltpu.CompilerParams(dimension_semantics=("arbitrary", "arbitrary"),
                                             vmem_limit_bytes=VMEM_LIMIT),
        name="retention",
    )(proj3, proj3, proj3, proj3, cos_t, sin_t, lg_t, gain)


def _merge_kernel(x_ref, ya_ref, yb_ref, ma_ref, mb_ref, wa_ref, wb_ref, wo_ref, g_ref, *o_refs, last):
    za = jnp.dot(ya_ref[...], wa_ref[...], preferred_element_type=F32)
    zb = jnp.dot(yb_ref[...], wb_ref[...], preferred_element_type=F32)
    merged = jax.nn.sigmoid(ma_ref[...].astype(F32)) * za + jax.nn.sigmoid(mb_ref[...].astype(F32)) * zb
    x = x_ref[...] + jnp.dot(merged.astype(BF16), wo_ref[...], preferred_element_type=F32)
    if last:
        o_refs[0][...] = _rms(x, g_ref[...])
    else:
        o_refs[0][...] = x
        o_refs[1][...] = _rms(x, g_ref[...]).astype(BF16)


def _merge_call(x2, ya2, yb2, proj2, wa, wb, wo, gain, *, last):
    t = x2.shape[0]
    tm = min(512, t)
    row = pl.BlockSpec((tm, D), lambda i: (i, 0))
    wspec = pl.BlockSpec((D, D), lambda i: (0, 0))
    x_sds = jax.ShapeDtypeStruct((t, D), F32)
    return pl.pallas_call(
        functools.partial(_merge_kernel, last=last),
        grid=(t // tm,),
        in_specs=[row, row, row,
                  pl.BlockSpec((tm, D), lambda i: (i, C_MA // D)),
                  pl.BlockSpec((tm, D), lambda i: (i, C_MB // D)),
                  wspec, wspec, wspec, pl.BlockSpec((1, D), lambda i: (0, 0))],
        out_specs=row if last else (row, row),
        out_shape=x_sds if last else (x_sds, jax.ShapeDtypeStruct((t, D), BF16)),
        compiler_params=pltpu.CompilerParams(dimension_semantics=("arbitrary",),
                                             vmem_limit_bytes=VMEM_LIMIT),
        name="merge",
    )(x2, ya2, yb2, proj2, proj2, wa, wb, wo, gain)


def _norm_kernel(x_ref, g_ref, o_ref):
    o_ref[...] = _rms(x_ref[...], g_ref[...]).astype(BF16)


def _norm_call(x2, gain):
    t = x2.shape[0]
    tm = min(1024, t)
    return pl.pallas_call(
        _norm_kernel,
        grid=(t // tm,),
        in_specs=[pl.BlockSpec((tm, D), lambda i: (i, 0)), pl.BlockSpec((1, D), lambda i: (0, 0))],
        out_specs=pl.BlockSpec((tm, D), lambda i: (i, 0)),
        out_shape=jax.ShapeDtypeStruct((t, D), BF16),
        compiler_params=pltpu.CompilerParams(dimension_semantics=("arbitrary",)),
        name="first_norm",
    )(x2, gain)


def _qk_perm():
    idx = []
    for pair in range(RET_HEADS // 2):
        for lane in range(LANES):
            seg, i = lane // 32, lane % 32
            head, half = 2 * pair + seg % 2, seg // 2
            idx.append(head * HEAD + half * 32 + i)
    return np.asarray(idx, np.int32)


def _pack_w_in(w_in, w_vres_down):
    depth = w_in.shape[0]
    perm = _qk_perm()
    vres = jnp.concatenate([jnp.zeros((1, D, 32), F32), w_vres_down], axis=0)
    vres = jnp.pad(vres, ((0, 0), (0, 0), (0, LANES - 32)))
    w_wkv = jnp.concatenate([w_in[:, :, 0:3072], w_in[:, :, 3328:4352], w_in[:, :, 3072:3328], vres], axis=2)
    w_rest = jnp.concatenate([w_in[:, :, 5376:6400], w_in[:, :, 6400:7424], w_in[:, :, 7424:8448],
                              w_in[:, :, 8448:9472], w_in[:, :, 4352:4864][:, :, perm],
                              w_in[:, :, 4864:5376][:, :, perm]], axis=2)
    assert w_wkv.shape == (depth, D, N_WKV) and w_rest.shape == (depth, D, N_REST)
    return w_wkv.astype(BF16), w_rest.astype(BF16)


def _pair_cat(wf, wb):
    pairs = D // LANES
    cat = jnp.concatenate([wf.reshape(LANES, pairs, LANES), wb.reshape(LANES, pairs, LANES)], axis=2)
    return cat.transpose(1, 0, 2)


def _rope_tables(seq):
    half = 32
    freqs = jnp.power(ROPE_BASE, -jnp.arange(half, dtype=F32) / half)
    ang = jnp.arange(seq, dtype=F32)[:, None] * freqs[None, :]
    cos = jnp.tile(jnp.cos(ang), (1, 4))
    sin = jnp.tile(jnp.sin(ang), (1, 4))
    sign = jnp.where(jnp.arange(LANES) < 64, -1.0, 1.0).astype(F32)
    return cos, sin * sign[None, :]


def kernel(x, norm_gain, w_in, w_vres_down, shift_prev, shift_next, w_decay_up, decay_bias, w_iclr_up, iclr_bias,
           w_vres_up, vres_bias, k_k, k_a, r_k, lnx_gain, lnx_bias, w_branch_a, ret_norm_gain, w_branch_b, w_out,
           final_gain):
    bsz, seq, _ = x.shape
    depth = w_in.shape[0]
    assert seq % RET_C == 0 and seq % SLAB == 0 and SLAB % CH == 0
    t = bsz * seq

    w_wkv, w_rest = _pack_w_in(w_in, w_vres_down)
    wa_all, wb_all, wo_all = (w.astype(BF16) for w in (w_branch_a, w_branch_b, w_out))
    cos_t, sin_t = _rope_tables(seq)
    log_g = jnp.log1p(-jnp.exp2(-5.0 - jnp.arange(RET_HEADS, dtype=F32)))
    lg_t = jnp.broadcast_to(log_g[:, None], (RET_HEADS, 2 * LANES))
    zpad = jnp.zeros((64, D), F32)
    zmat = jnp.zeros((LANES, D), F32)

    x2 = x.reshape(t, D)
    xn2 = _norm_call(x2, norm_gain[0][None, :])
    vfirst = None
    for l in range(depth):
        proj3 = _proj_call(xn2, w_wkv[l], F32).reshape(bsz, seq, N_WKV)
        rest2 = _proj_call(xn2, w_rest[l], BF16)
        rest3 = rest2.reshape(bsz, seq, N_REST)

        vb = vres_bias[l - 1] if l > 0 else jnp.zeros((D,), F32)
        vecs = jnp.stack([shift_prev[l, 0:1024], shift_prev[l, 1024:2048], shift_prev[l, 2048:3072],
                          shift_next[l, 0:1024], shift_next[l, 1024:2048], shift_next[l, 2048:3072],
                          decay_bias[l, 0], decay_bias[l, 1], iclr_bias[l, 0], iclr_bias[l, 1],
                          k_k[l], k_a[l], r_k[l].reshape(D), lnx_gain[l], lnx_bias[l], vb], axis=0)
        mul = jnp.concatenate([jnp.stack([shift_prev[l, 3072:3200], shift_prev[l, 3200:3328],
                                          shift_next[l, 3072:3200], shift_next[l, 3200:3328]], axis=0),
                               jnp.zeros((4, LANES), F32)], axis=0)
        wvr = jnp.pad(w_vres_up[l - 1], ((0, LANES - 32), (0, 0))) if l > 0 else zmat
        wl = jnp.stack([_pair_cat(jnp.concatenate([w_decay_up[l, 0], zpad], axis=0),
                                  jnp.concatenate([zpad, w_decay_up[l, 1]], axis=0)),
                        _pair_cat(jnp.concatenate([w_iclr_up[l, 0], zpad], axis=0),
                                  jnp.concatenate([zpad, w_iclr_up[l, 1]], axis=0)),
                        _pair_cat(wvr, zmat)], axis=1)

        if l == 0:
            ya, vfirst = _wkv_call(proj3, None, vecs, mul, wl, first=True)
        else:
            ya = _wkv_call(proj3, vfirst, vecs, mul, wl, first=False)
        yb = _ret_call(rest3, cos_t, sin_t, lg_t, ret_norm_gain[l][None, :])
        last = l == depth - 1
        gain = final_gain if last else norm_gain[l + 1]
        res = _merge_call(x2, ya.reshape(t, D), yb.reshape(t, D), rest2, wa_all[l], wb_all[l], wo_all[l],
                          gain[None, :], last=last)
        if not last:
            x2, xn2 = res

    return res.reshape(bsz, seq, D)
```

```python
import functools

import numpy as np
import jax
import jax.numpy as jnp
from jax import lax
from jax.experimental import pallas as pl
from jax.experimental.pallas import tpu as pltpu

F32 = jnp.float32
BF16 = jnp.bfloat16

D = 1024
HEAD = 64
LANES = 128
CH = 64
SLAB = 512
RET_HEADS = 8
RET_C = 256
NORM_EPS = 1e-6
GN_EPS = 64e-5
EXP_M_HALF = 0.6065306597126334
ROPE_BASE = 10000.0
VMEM_LIMIT = 56 * 1024 * 1024

C_R, C_K, C_V, C_GA, C_DEC, C_ICL, C_VRES, N_WKV = 0, 1024, 2048, 3072, 4096, 4224, 4352, 4480
C_VB, C_GB, C_MA, C_MB, C_QB, C_KB, N_REST = 0, 1024, 2048, 3072, 4096, 4608, 5120

P_LORA = 1
P_WKV = 1

_NN = (((1,), (0,)), ((), ()))
_NT = (((1,), (1,)), ((), ()))
_TN = (((0,), (0,)), ((), ()))


def _dg(a, b, dims):
    return lax.dot_general(a, b, dims, preferred_element_type=F32)


def _split2(x):
    hi = x.astype(BF16)
    lo = (x - hi.astype(F32)).astype(BF16)
    return hi, lo


def _mm(a, b, dims=_NN, prec=1):
    if prec == 1:
        return _dg(a.astype(BF16), b.astype(BF16), dims)
    ah, al = _split2(a)
    bh, bl = _split2(b)
    return _dg(ah, bh, dims) + (_dg(ah, bl, dims) + _dg(al, bh, dims))


def _headsum(xs, ones_bd):
    return [_dg(x.astype(BF16), ones_bd, _NN) for x in xs]


def _iota(shape, axis):
    return lax.broadcasted_iota(jnp.int32, shape, axis)


def _pair_bd(x, bdmask):
    return jnp.where(bdmask, jnp.concatenate([x, x], axis=0), 0.0)


def _wkv_consts():
    row = _iota((CH, LANES), 0)
    col = _iota((CH, LANES), 1) % CH
    r128 = _iota((LANES, LANES), 0)
    c128 = _iota((LANES, LANES), 1)

    def masks(lower):
        late, early = (row, col) if lower else (col, row)
        return dict(
            strict=early < late, incl=early <= late,
            m_d=(early < late) & ((row // 16) == (col // 16)),
            m_o1=((row // 32) == (col // 32)) & ((late // 16) % 2 == 1) & ((early // 16) % 2 == 0),
            m_o2=((late // 32) == 1) & ((early // 32) == 0))

    return dict(bd=(r128 // CH) == (c128 // CH), eye128=r128 == c128,
                eye_cat=jnp.where(row == col, 1.0, 0.0).astype(F32),
                fwd=masks(True), bwd=masks(False))


def _wkv_mxu_steps(items, cst):
    bdmask = cst["bd"]

    def bd(x):
        return _pair_bd(x, bdmask)

    def mm(a, b, dims=_NN):
        return _mm(a, b, dims, prec=P_WKV)

    st = [dict(it=it, mk=cst["bwd"] if it["rev"] else cst["fwd"]) for it in items]
    for s in st:
        ld = s["it"]["load"]
        amat = mm(jnp.concatenate([ld("at"), ld("rt")], axis=0),
                  jnp.concatenate([bd(ld("bt")), bd(ld("kt"))], axis=0), _NT)
        mk = s["mk"]
        s["n"] = amat[:CH, :LANES]
        s["akrk"] = jnp.concatenate([jnp.where(mk["strict"], amat[:CH, LANES:], 0.0),
                                     jnp.where(mk["incl"], amat[CH:, LANES:], 0.0)], axis=0)
        s["rb"] = jnp.where(mk["incl"], amat[CH:, :LANES], 0.0)
    yield
    for s in st:
        xv = mm(s["akrk"], bd(s["it"]["load"]("v")))
        s["x1"], s["y0v"] = xv[:CH], xv[CH:]
    yield
    for s in st:
        s["p"] = jnp.where(s["mk"]["m_d"], s["n"], 0.0)
        s["t"] = cst["eye_cat"] + s["p"]
        s["q"] = mm(s["p"], bd(s["p"]))
    yield
    for _ in range(2):
        for s in st:
            qt = mm(jnp.concatenate([s["q"], s["t"]], axis=0), bd(s["q"]))
            s["t"] = s["t"] + qt[CH:]
            s["q"] = qt[:CH]
        yield
    for s in st:
        s["t"] = s["t"] + mm(s["t"], bd(s["q"]))
    yield
    for key in ("m_o1", "m_o2"):
        for s in st:
            s["x"] = mm(s["t"], bd(jnp.where(s["mk"][key], s["n"], 0.0)))
        yield
        for s in st:
            s["t"] = s["t"] + mm(s["x"], bd(s["t"]))
        yield
    for s in st:
        au = mm(s["t"], jnp.concatenate([bd(s["it"]["load"]("at")), bd(s["x1"])], axis=1))
        s["ah"], s["u0"] = au[:, :LANES], au[:, LANES:]
    yield
    for s in st:
        ld = s["it"]["load"]
        w_tot = s["it"]["w_tot"]()
        ry = mm(s["rb"], jnp.concatenate([bd(s["ah"]), bd(s["u0"])], axis=1))
        rhs2 = jnp.concatenate([jnp.concatenate([s["ah"], s["u0"]], axis=1),
                                jnp.concatenate([jnp.zeros((CH, LANES), F32), ld("v")], axis=1)], axis=0)
        kb_end = jnp.concatenate([ld("bt") * w_tot, ld("kt") * w_tot], axis=0)
        mg = mm(kb_end, rhs2, _TN)
        m = jnp.where(bdmask, mg[:, :LANES], 0.0) + jnp.where(cst["eye128"], w_tot, 0.0)
        g = jnp.where(bdmask, mg[:, LANES:], 0.0)
        s["it"]["store"](ld("rt") + ry[:, :LANES], ry[:, LANES:] + s["y0v"], m, g)
    yield


def _run(gen):
    for _ in gen:
        pass


def _interleave(main, side, ratio):
    for _ in main:
        for _ in range(ratio):
            next(side, None)
    _run(side)


def _chain(gens):
    for g in gens:
        yield from g


def _sigmoid(x):
    return 0.5 * jnp.tanh(0.5 * x) + 0.5


def _shifted(ref, start, n, mp, mn, seq):
    sub = 8
    edge = _iota((sub, LANES), 0)
    cur = ref[start:start + n, :]
    if start > 0:
        prev = ref[start - 1:start - 1 + n, :]
    else:
        head = jnp.where(edge == 0, 0.0, pltpu.roll(ref[0:sub, :], 1, 0))
        prev = jnp.concatenate([head, ref[sub - 1:n - 1, :]], axis=0)
    if start + n < seq:
        nxt = ref[start + 1:start + 1 + n, :]
    else:
        tail = jnp.where(edge == sub - 1, 0.0, pltpu.roll(ref[seq - sub:seq, :], sub - 1, 0))
        nxt = jnp.concatenate([ref[start + 1:seq - sub + 1, :], tail], axis=0)
    return cur * (1.0 - mp - mn) + mp * prev + mn * nxt


def _chunk_cumsum(x, rev):
    n = x.shape[0]
    sub = 8
    pos = _iota((n, LANES), 0) % CH
    s = 1
    while s < sub:
        if rev:
            x = x + jnp.where(pos < CH - s, pltpu.roll(x, n - s, 0), 0.0)
        else:
            x = x + jnp.where(pos >= s, pltpu.roll(x, s, 0), 0.0)
        s *= 2
    g = CH // sub
    x = x.reshape(n // CH, g, sub, LANES)
    s = 1
    while s < g:
        z = jnp.zeros((n // CH, s, sub, LANES), F32)
        if rev:
            x = x + jnp.concatenate([x[:, s:], z], axis=1)
        else:
            x = x + jnp.concatenate([z, x[:, :-s]], axis=1)
        s *= 2
    return x.reshape(n, LANES)


(V_MPR, V_MPK, V_MPV, V_MNR, V_MNK, V_MNV, V_DBF, V_DBB, V_IBF, V_IBB,
 V_KK, V_KA, V_RK, V_LG, V_LB, V_VB) = range(16)
W_DEC, W_ICL, W_VR = range(3)
I_RT, I_AT, I_KT, I_BT = range(4)
I_NAMES = dict(rt=I_RT, at=I_AT, kt=I_KT, bt=I_BT)


def _wkv_kernel(*refs, seq, first):
    if first:
        (r_ref, k_ref, v_ref, ga_ref, dec_ref, icl_ref, vec_ref, mul_ref, wl_ref,
         ya_ref, vf_out_ref, lhs_s, add_s, yf_s, yb_s, bon_s, it_s, v_s, wt_s) = refs
        vres_ref = vf_ref = None
    else:
        (r_ref, k_ref, v_ref, ga_ref, dec_ref, icl_ref, vres_ref, vf_ref, vec_ref, mul_ref, wl_ref,
         ya_ref, lhs_s, add_s, yf_s, yb_s, bon_s, it_s, v_s, wt_s) = refs
        vf_out_ref = None
    ns = seq // SLAB
    per = SLAB // CH
    cst = _wkv_consts()
    ones_bd = jnp.where(cst["bd"], 1.0, 0.0).astype(BF16)

    def vec(i):
        return vec_ref[i:i + 1, :]

    def prologue(sb):
        start = sb * SLAB
        rows = slice(start, start + SLAB)
        r = _shifted(r_ref, start, SLAB, vec(V_MPR), vec(V_MNR), seq)
        yield
        k = _shifted(k_ref, start, SLAB, vec(V_MPK), vec(V_MNK), seq)
        yield
        v = _shifted(v_ref, start, SLAB, vec(V_MPV), vec(V_MNV), seq)
        yield
        dec = _shifted(dec_ref, start, SLAB, mul_ref[0:1, :], mul_ref[2:3, :], seq)
        icl = _shifted(icl_ref, start, SLAB, mul_ref[1:2, :], mul_ref[3:4, :], seq)
        yield
        if first:
            vf_out_ref[rows, :] = v
        else:
            mix = _sigmoid(vec(V_VB) + _mm(vres_ref[rows, :], wl_ref[W_VR], prec=P_LORA)[:, :LANES])
            v = v + (vf_ref[rows, :] - v) * mix
        v_s[rows, :] = v
        yield
        zd = _mm(jnp.tanh(dec), wl_ref[W_DEC], prec=P_LORA)
        zi = _mm(icl, wl_ref[W_ICL], prec=P_LORA)
        kk = k * vec(V_KK)
        yield
        kds, ags, lws = [], [], []
        kka = k * vec(V_KA)
        kbase = k - kka
        for d in range(2):
            z = vec(V_DBF + d) + zd[:, d * LANES:(d + 1) * LANES]
            lws.append(-EXP_M_HALF * _sigmoid(z))
            ag = _sigmoid(vec(V_IBF + d) + zi[:, d * LANES:(d + 1) * LANES])
            kds.append(kbase + kka * ag)
            ags.append(ag)
            yield
        ssq, bsum = _headsum([kk * kk, r * (kds[0] + kds[1]) * vec(V_RK)], ones_bd)
        kk = kk * lax.rsqrt(ssq + 1e-12)
        bon_s[rows, :] = bsum * v
        yield
        for d in range(2):
            cum = _chunk_cumsum(lws[d], d == 1)
            yield
            e_inv = jnp.exp(-cum)
            it_s[d, I_RT, rows, :] = r * jnp.exp(cum)
            it_s[d, I_AT, rows, :] = -kk * jnp.exp(cum - lws[d])
            yield
            it_s[d, I_KT, rows, :] = kds[d] * e_inv
            it_s[d, I_BT, rows, :] = (kk * ags[d]) * e_inv
            for i in range(per):
                c = sb * per + i
                row = i * CH if d == 1 else i * CH + CH - 1
                wt_s[d, c * 8:(c + 1) * 8, :] = jnp.broadcast_to(jnp.exp(cum[row:row + 1, :]), (8, LANES))
            yield

    def chunk_items(step):
        buf = step % 2
        items = []
        for d in range(2):
            slab = step if d == 0 else ns - 1 - step
            for i in range(per):
                c = slab * per + i
                crow = slice(c * CH, (c + 1) * CH)

                def load(name, d=d, crow=crow):
                    if name == "v":
                        return v_s[crow, :]
                    return it_s[d, I_NAMES[name], crow, :]

                def w_tot(d=d, c=c):
                    return wt_s[d, c * 8:c * 8 + 1, :]

                def store(rh, y0, m, g, d=d, i=i):
                    lhs_s[buf, d, i, :CH, :] = rh
                    lhs_s[buf, d, i, CH:, :] = m
                    add_s[buf, d, i, :CH, :] = y0
                    add_s[buf, d, i, CH:, :] = g

                items.append(dict(rev=d == 1, load=load, w_tot=w_tot, store=store))
        return items

    state = [jnp.zeros((LANES, LANES), F32), jnp.zeros((LANES, LANES), F32)]

    def scan(step):
        buf = step % 2
        for j in range(per):
            for d in range(2):
                i = j if d == 0 else per - 1 - j
                slab = step if d == 0 else ns - 1 - step
                o = _mm(lhs_s[buf, d, i], state[d], prec=P_WKV) + add_s[buf, d, i]
                rows = slice((slab * per + i) * CH, (slab * per + i + 1) * CH)
                (yf_s if d == 0 else yb_s)[rows, :] = o[:CH]
                state[d] = o[CH:]
            yield

    def post(start, n):
        rows = slice(start, start + n)
        y = yf_s[rows, :] + yb_s[rows, :]
        mu = _headsum([y], ones_bd)[0] * (1.0 / HEAD)
        yield
        dv = y - mu
        var = _headsum([dv * dv], ones_bd)[0] * (1.0 / HEAD)
        yield
        out = dv * lax.rsqrt(var + GN_EPS) * vec(V_LG) + vec(V_LB) + bon_s[rows, :]
        ga = ga_ref[rows, :]
        ya_ref[rows, :] = (out * (ga * _sigmoid(ga))).astype(BF16)
        yield

    ends = sorted({0, ns - 1})
    inner = [sb for sb in range(ns) if sb not in ends]
    for sb in ends:
        _run(prologue(sb))
    for step in range(ns):
        side = _chain([prologue(sb) for sb in inner]) if step == 0 else scan(step - 1)
        _interleave(_wkv_mxu_steps(chunk_items(step), cst), side, 3 if step == 0 else 1)
    _interleave(scan(ns - 1), _chain([post(sb * SLAB, SLAB) for sb in inner]), 1)
    for sb in ends:
        _run(post(sb * SLAB, SLAB))


def _wkv_call(proj3, vfirst, vecs, mul, wl, *, first):
    bsz, seq, _ = proj3.shape
    nc = seq // CH
    per = SLAB // CH

    def col(base):
        blk = base // LANES
        return pl.BlockSpec((None, seq, LANES), lambda b, p, blk=blk: (b, 0, blk + p))

    def fixed(base):
        blk = base // LANES
        return pl.BlockSpec((None, seq, LANES), lambda b, p, blk=blk: (b, 0, blk))

    in_specs = [col(C_R), col(C_K), col(C_V), col(C_GA), fixed(C_DEC), fixed(C_ICL)]
    args = [proj3] * 6
    if not first:
        in_specs += [fixed(C_VRES), pl.BlockSpec((None, seq, LANES), lambda b, p: (b, 0, p))]
        args += [proj3, vfirst]
    in_specs += [pl.BlockSpec((16, LANES), lambda b, p: (0, p)),
                 pl.BlockSpec((8, LANES), lambda b, p: (0, 0)),
                 pl.BlockSpec((None, 3, LANES, 2 * LANES), lambda b, p: (p, 0, 0, 0))]
    args += [vecs, mul, wl]
    out_block = pl.BlockSpec((None, seq, LANES), lambda b, p: (b, 0, p))
    out_sds = jax.ShapeDtypeStruct((bsz, seq, D), BF16)
    out_shape = (out_sds, jax.ShapeDtypeStruct((bsz, seq, D), F32)) if first else out_sds
    out_specs = (out_block, out_block) if first else out_block
    scratch = [pltpu.VMEM((2, 2, per, CH + LANES, LANES), F32),
               pltpu.VMEM((2, 2, per, CH + LANES, LANES), F32),
               pltpu.VMEM((seq, LANES), F32), pltpu.VMEM((seq, LANES), F32), pltpu.VMEM((seq, LANES), F32),
               pltpu.VMEM((2, 4, seq, LANES), F32),
               pltpu.VMEM((seq, LANES), F32),
               pltpu.VMEM((2, nc * 8, LANES), F32)]
    return pl.pallas_call(
        functools.partial(_wkv_kernel, seq=seq, first=first),
        grid=(bsz, D // LANES),
        in_specs=in_specs, out_specs=out_specs, out_shape=out_shape,
        scratch_shapes=scratch,
        compiler_params=pltpu.CompilerParams(dimension_semantics=("arbitrary", "arbitrary"),
                                             vmem_limit_bytes=VMEM_LIMIT),
        name="wkv7",
    )(*args)


def _rms(x, g):
    ms = jnp.mean(x * x, axis=-1, keepdims=True)
    return x * lax.rsqrt(ms + NORM_EPS) * g


def _proj_kernel(xn_ref, w_ref, o_ref):
    o_ref[...] = jnp.dot(xn_ref[...], w_ref[...], preferred_element_type=F32).astype(o_ref.dtype)


PROJ_BLOCK_BUDGET = 40 * 1024 * 1024


def _proj_tiles(t, n, out_bytes):
    best = None
    for tm in (1024, 512, 256):
        if t % tm:
            continue
        for tn in range(LANES, n + 1, LANES):
            if n % tn == 0 and 2 * (tm * D * 2 + D * tn * 2 + tm * tn * out_bytes) <= PROJ_BLOCK_BUDGET:
                if best is None or tm * tn > best[0] * best[1]:
                    best = (tm, tn)
    return best


def _proj_call(xn2, w, out_dtype):
    t = xn2.shape[0]
    n = w.shape[1]
    tm, tn = _proj_tiles(t, n, jnp.dtype(out_dtype).itemsize)
    return pl.pallas_call(
        _proj_kernel,
        grid=(n // tn, t // tm),
        in_specs=[pl.BlockSpec((tm, D), lambda j, i: (i, 0)),
                  pl.BlockSpec((D, tn), lambda j, i: (0, j))],
        out_specs=pl.BlockSpec((tm, tn), lambda j, i: (i, j)),
        out_shape=jax.ShapeDtypeStruct((t, n), out_dtype),
        compiler_params=pltpu.CompilerParams(dimension_semantics=("arbitrary", "arbitrary"),
                                             vmem_limit_bytes=VMEM_LIMIT),
        name="proj",
    )(xn2, w)


def _ret_kernel(q_ref, k_ref, v_ref, gb_ref, cos_ref, sin_ref, lg_ref, gain_ref, o_ref, qr_s, kr_s, y_s, *, seq):
    nc = seq // RET_C
    p = pl.program_id(1)
    scale = HEAD ** -0.5

    def rot(c, _):
        sl = pl.ds(pl.multiple_of(c * RET_C, RET_C), RET_C)
        cs, sn = cos_ref[sl, :], sin_ref[sl, :]
        q = q_ref[sl, :].astype(F32)
        k = k_ref[sl, :].astype(F32)
        qr_s[sl, :] = (q * cs + pltpu.roll(q, 64, 1) * sn) * scale
        kr_s[sl, :] = k * cs + pltpu.roll(k, 64, 1) * sn
        return 0

    lax.fori_loop(0, nc, rot, 0)

    lane = _iota((RET_C, LANES), 1)
    rowi = _iota((RET_C, LANES), 0).astype(F32)
    absd = jnp.abs(_iota((RET_C, RET_C), 0) - _iota((RET_C, RET_C), 1)).astype(F32)
    for j in range(2):
        hmask = ((lane // 32) % 2) == j
        lg = lg_ref[pl.ds(2 * p + j, 1), :]
        lg128 = lg[:, :LANES]
        dmat = jnp.exp(absd * lg)
        xi_f = jnp.exp((rowi + 1.0) * lg128)
        ze_f = jnp.exp((RET_C - 1.0 - rowi) * lg128)
        xi_b = jnp.exp((RET_C - rowi) * lg128)
        ze_b = jnp.exp(rowi * lg128)
        cdec = jnp.exp(RET_C * lg128)
        vs = slice(j * LANES, (j + 1) * LANES)
        fst = jnp.zeros((LANES, LANES), F32)
        for i in range(nc):
            sl = slice(i * RET_C, (i + 1) * RET_C)
            qh = jnp.where(hmask, qr_s[sl, :], 0.0)
            kh = kr_s[sl, :]
            vh = v_ref[sl, vs]
            sc = _mm(qh, kh, _NT) * dmat
            y = _mm(sc, vh)
            if i > 0:
                y = y + _mm(qh * xi_f, fst)
            if i < nc - 1:
                fst = fst * cdec + _mm(jnp.where(hmask, kh, 0.0) * ze_f, vh, _TN)
            y_s[sl, vs] = y
        gst = jnp.zeros((LANES, LANES), F32)
        for i in range(nc - 1, -1, -1):
            sl = slice(i * RET_C, (i + 1) * RET_C)
            y = y_s[sl, vs]
            if i < nc - 1:
                qh = jnp.where(hmask, qr_s[sl, :], 0.0)
                y = y + _mm(qh * xi_b, gst)
            if i > 0:
                kh = jnp.where(hmask, kr_s[sl, :], 0.0)
                gst = gst * cdec + _mm(kh * ze_b, v_ref[sl, vs], _TN)
            mu = jnp.mean(y, axis=-1, keepdims=True)
            dv = y - mu
            var = jnp.mean(dv * dv, axis=-1, keepdims=True)
            out = dv * lax.rsqrt(var + NORM_EPS) * gain_ref[:, vs]
            o_ref[sl, vs] = (out * jax.nn.silu(gb_ref[sl, vs].astype(F32))).astype(BF16)


def _ret_call(proj3, cos_t, sin_t, lg_t, gain):
    bsz, seq, _ = proj3.shape
    w2 = 2 * LANES

    def spec(base, width):
        blk = base // width
        return pl.BlockSpec((None, seq, width), lambda b, p, blk=blk: (b, 0, blk + p))

    return pl.pallas_call(
        functools.partial(_ret_kernel, seq=seq),
        grid=(bsz, RET_HEADS // 2),
        in_specs=[spec(C_QB, LANES), spec(C_KB, LANES), spec(C_VB, w2), spec(C_GB, w2),
                  pl.BlockSpec((seq, LANES), lambda b, p: (0, 0)),
                  pl.BlockSpec((seq, LANES), lambda b, p: (0, 0)),
                  pl.BlockSpec((RET_HEADS, w2), lambda b, p: (0, 0)),
                  pl.BlockSpec((1, w2), lambda b, p: (0, p))],
        out_specs=pl.BlockSpec((None, seq, w2), lambda b, p: (b, 0, p)),
        out_shape=jax.ShapeDtypeStruct((bsz, seq, D), BF16),
        scratch_shapes=[pltpu.VMEM((seq, LANES), F32), pltpu.VMEM((seq, LANES), F32), pltpu.VMEM((seq, w2), F32)],
        compiler_params=pltpu.CompilerParams(dimension_semantics=("arbitrary", "arbitrary"),
                                             vmem_limit_bytes=VMEM_LIMIT),
        name="retention",
    )(proj3, proj3, proj3, proj3, cos_t, sin_t, lg_t, gain)


def _merge_kernel(x_ref, ya_ref, yb_ref, ma_ref, mb_ref, wa_ref, wb_ref, wo_ref, g_ref, *o_refs, last):
    za = jnp.dot(ya_ref[...], wa_ref[...], preferred_element_type=F32)
    zb = jnp.dot(yb_ref[...], wb_ref[...], preferred_element_type=F32)
    merged = jax.nn.sigmoid(ma_ref[...].astype(F32)) * za + jax.nn.sigmoid(mb_ref[...].astype(F32)) * zb
    x = x_ref[...] + jnp.dot(merged.astype(BF16), wo_ref[...], preferred_element_type=F32)
    if last:
        o_refs[0][...] = _rms(x, g_ref[...])
    else:
        o_refs[0][...] = x
        o_refs[1][...] = _rms(x, g_ref[...]).astype(BF16)


def _merge_call(x2, ya2, yb2, proj2, wa, wb, wo, gain, *, last):
    t = x2.shape[0]
    tm = min(512, t)
    row = pl.BlockSpec((tm, D), lambda i: (i, 0))
    wspec = pl.BlockSpec((D, D), lambda i: (0, 0))
    x_sds = jax.ShapeDtypeStruct((t, D), F32)
    return pl.pallas_call(
        functools.partial(_merge_kernel, last=last),
        grid=(t // tm,),
        in_specs=[row, row, row,
                  pl.BlockSpec((tm, D), lambda i: (i, C_MA // D)),
                  pl.BlockSpec((tm, D), lambda i: (i, C_MB // D)),
                  wspec, wspec, wspec, pl.BlockSpec((1, D), lambda i: (0, 0))],
        out_specs=row if last else (row, row),
        out_shape=x_sds if last else (x_sds, jax.ShapeDtypeStruct((t, D), BF16)),
        compiler_params=pltpu.CompilerParams(dimension_semantics=("arbitrary",),
                                             vmem_limit_bytes=VMEM_LIMIT),
        name="merge",
    )(x2, ya2, yb2, proj2, proj2, wa, wb, wo, gain)


def _norm_kernel(x_ref, g_ref, o_ref):
    o_ref[...] = _rms(x_ref[...], g_ref[...]).astype(BF16)


def _norm_call(x2, gain):
    t = x2.shape[0]
    tm = min(1024, t)
    return pl.pallas_call(
        _norm_kernel,
        grid=(t // tm,),
        in_specs=[pl.BlockSpec((tm, D), lambda i: (i, 0)), pl.BlockSpec((1, D), lambda i: (0, 0))],
        out_specs=pl.BlockSpec((tm, D), lambda i: (i, 0)),
        out_shape=jax.ShapeDtypeStruct((t, D), BF16),
        compiler_params=pltpu.CompilerParams(dimension_semantics=("arbitrary",)),
        name="first_norm",
    )(x2, gain)


def _qk_perm():
    idx = []
    for pair in range(RET_HEADS // 2):
        for lane in range(LANES):
            seg, i = lane // 32, lane % 32
            head, half = 2 * pair + seg % 2, seg // 2
            idx.append(head * HEAD + half * 32 + i)
    return np.asarray(idx, np.int32)


def _pack_w_in(w_in, w_vres_down):
    depth = w_in.shape[0]
    perm = _qk_perm()
    vres = jnp.concatenate([jnp.zeros((1, D, 32), F32), w_vres_down], axis=0)
    vres = jnp.pad(vres, ((0, 0), (0, 0), (0, LANES - 32)))
    w_wkv = jnp.concatenate([w_in[:, :, 0:3072], w_in[:, :, 3328:4352], w_in[:, :, 3072:3328], vres], axis=2)
    w_rest = jnp.concatenate([w_in[:, :, 5376:6400], w_in[:, :, 6400:7424], w_in[:, :, 7424:8448],
                              w_in[:, :, 8448:9472], w_in[:, :, 4352:4864][:, :, perm],
                              w_in[:, :, 4864:5376][:, :, perm]], axis=2)
    assert w_wkv.shape == (depth, D, N_WKV) and w_rest.shape == (depth, D, N_REST)
    return w_wkv.astype(BF16), w_rest.astype(BF16)


def _pair_cat(wf, wb):
    pairs = D // LANES
    cat = jnp.concatenate([wf.reshape(LANES, pairs, LANES), wb.reshape(LANES, pairs, LANES)], axis=2)
    return cat.transpose(1, 0, 2)


def _rope_tables(seq):
    half = 32
    freqs = jnp.power(ROPE_BASE, -jnp.arange(half, dtype=F32) / half)
    ang = jnp.arange(seq, dtype=F32)[:, None] * freqs[None, :]
    cos = jnp.tile(jnp.cos(ang), (1, 4))
    sin = jnp.tile(jnp.sin(ang), (1, 4))
    sign = jnp.where(jnp.arange(LANES) < 64, -1.0, 1.0).astype(F32)
    return cos, sin * sign[None, :]


def kernel(x, norm_gain, w_in, w_vres_down, shift_prev, shift_next, w_decay_up, decay_bias, w_iclr_up, iclr_bias,
           w_vres_up, vres_bias, k_k, k_a, r_k, lnx_gain, lnx_bias, w_branch_a, ret_norm_gain, w_branch_b, w_out,
           final_gain):
    bsz, seq, _ = x.shape
    depth = w_in.shape[0]
    assert seq % RET_C == 0 and seq % SLAB == 0 and SLAB % CH == 0
    t = bsz * seq

    w_wkv, w_rest = _pack_w_in(w_in, w_vres_down)
    wa_all, wb_all, wo_all = (w.astype(BF16) for w in (w_branch_a, w_branch_b, w_out))
    cos_t, sin_t = _rope_tables(seq)
    log_g = jnp.log1p(-jnp.exp2(-5.0 - jnp.arange(RET_HEADS, dtype=F32)))
    lg_t = jnp.broadcast_to(log_g[:, None], (RET_HEADS, 2 * LANES))
    zpad = jnp.zeros((64, D), F32)
    zmat = jnp.zeros((LANES, D), F32)

    x2 = x.reshape(t, D)
    xn2 = _norm_call(x2, norm_gain[0][None, :])
    vfirst = None
    for l in range(depth):
        proj3 = _proj_call(xn2, w_wkv[l], F32).reshape(bsz, seq, N_WKV)
        rest2 = _proj_call(xn2, w_rest[l], BF16)
        rest3 = rest2.reshape(bsz, seq, N_REST)

        vb = vres_bias[l - 1] if l > 0 else jnp.zeros((D,), F32)
        vecs = jnp.stack([shift_prev[l, 0:1024], shift_prev[l, 1024:2048], shift_prev[l, 2048:3072],
                          shift_next[l, 0:1024], shift_next[l, 1024:2048], shift_next[l, 2048:3072],
                          decay_bias[l, 0], decay_bias[l, 1], iclr_bias[l, 0], iclr_bias[l, 1],
                          k_k[l], k_a[l], r_k[l].reshape(D), lnx_gain[l], lnx_bias[l], vb], axis=0)
        mul = jnp.concatenate([jnp.stack([shift_prev[l, 3072:3200], shift_prev[l, 3200:3328],
                                          shift_next[l, 3072:3200], shift_next[l, 3200:3328]], axis=0),
                               jnp.zeros((4, LANES), F32)], axis=0)
        wvr = jnp.pad(w_vres_up[l - 1], ((0, LANES - 32), (0, 0))) if l > 0 else zmat
        wl = jnp.stack([_pair_cat(jnp.concatenate([w_decay_up[l, 0], zpad], axis=0),
                                  jnp.concatenate([zpad, w_decay_up[l, 1]], axis=0)),
                        _pair_cat(jnp.concatenate([w_iclr_up[l, 0], zpad], axis=0),
                                  jnp.concatenate([zpad, w_iclr_up[l, 1]], axis=0)),
                        _pair_cat(wvr, zmat)], axis=1)

        if l == 0:
            ya, vfirst = _wkv_call(proj3, None, vecs, mul, wl, first=True)
        else:
            ya = _wkv_call(proj3, vfirst, vecs, mul, wl, first=False)
        yb = _ret_call(rest3, cos_t, sin_t, lg_t, ret_norm_gain[l][None, :])
        last = l == depth - 1
        gain = final_gain if last else norm_gain[l + 1]
        res = _merge_call(x2, ya.reshape(t, D), yb.reshape(t, D), rest2, wa_all[l], wb_all[l], wo_all[l],
                          gain[None, :], last=last)
        if not last:
            x2, xn2 = res

    return res.reshape(bsz, seq, D)
```

```python
import functools

import numpy as np
import jax
import jax.numpy as jnp
from jax import lax
from jax.experimental import pallas as pl
from jax.experimental.pallas import tpu as pltpu

F32 = jnp.float32
BF16 = jnp.bfloat16

D = 1024
HEAD = 64
LANES = 128
CH = 64
SLAB = 512
RET_HEADS = 8
RET_C = 256
NORM_EPS = 1e-6
GN_EPS = 64e-5
EXP_M_HALF = 0.6065306597126334
ROPE_BASE = 10000.0
VMEM_LIMIT = 56 * 1024 * 1024

C_R, C_K, C_V, C_GA, C_DEC, C_ICL, C_VRES, N_WKV = 0, 1024, 2048, 3072, 4096, 4224, 4352, 4480
C_VB, C_GB, C_MA, C_MB, C_QB, C_KB, N_REST = 0, 1024, 2048, 3072, 4096, 4608, 5120

P_LORA = 1
P_WKV = 1

_NN = (((1,), (0,)), ((), ()))
_NT = (((1,), (1,)), ((), ()))
_TN = (((0,), (0,)), ((), ()))


def _dg(a, b, dims):
    return lax.dot_general(a, b, dims, preferred_element_type=F32)


def _split2(x):
    hi = x.astype(BF16)
    lo = (x - hi.astype(F32)).astype(BF16)
    return hi, lo


def _mm(a, b, dims=_NN, prec=1):
    if prec == 1:
        return _dg(a.astype(BF16), b.astype(BF16), dims)
    ah, al = _split2(a)
    bh, bl = _split2(b)
    return _dg(ah, bh, dims) + (_dg(ah, bl, dims) + _dg(al, bh, dims))


def _headsum(xs, ones_bd):
    return [_dg(x.astype(BF16), ones_bd, _NN) for x in xs]


def _iota(shape, axis):
    return lax.broadcasted_iota(jnp.int32, shape, axis)


def _pair_bd(x, bdmask):
    return jnp.where(bdmask, jnp.concatenate([x, x], axis=0), 0.0)


def _wkv_consts():
    row = _iota((CH, LANES), 0)
    col = _iota((CH, LANES), 1) % CH
    r128 = _iota((LANES, LANES), 0)
    c128 = _iota((LANES, LANES), 1)

    def masks(lower):
        late, early = (row, col) if lower else (col, row)
        return dict(
            strict=early < late, incl=early <= late,
            m_d=(early < late) & ((row // 16) == (col // 16)),
            m_o1=((row // 32) == (col // 32)) & ((late // 16) % 2 == 1) & ((early // 16) % 2 == 0),
            m_o2=((late // 32) == 1) & ((early // 32) == 0))

    return dict(bd=(r128 // CH) == (c128 // CH), eye128=r128 == c128,
                eye_cat=jnp.where(row == col, 1.0, 0.0).astype(F32),
                fwd=masks(True), bwd=masks(False))


def _wkv_mxu_steps(items, cst):
    bdmask = cst["bd"]

    def bd(x):
        return _pair_bd(x, bdmask)

    def mm(a, b, dims=_NN):
        return _mm(a, b, dims, prec=P_WKV)

    st = [dict(it=it, mk=cst["bwd"] if it["rev"] else cst["fwd"]) for it in items]
    for s in st:
        ld = s["it"]["load"]
        amat = mm(jnp.concatenate([ld("at"), ld("rt")], axis=0),
                  jnp.concatenate([bd(ld("bt")), bd(ld("kt"))], axis=0), _NT)
        mk = s["mk"]
        s["n"] = amat[:CH, :LANES]
        s["akrk"] = jnp.concatenate([jnp.where(mk["strict"], amat[:CH, LANES:], 0.0),
                                     jnp.where(mk["incl"], amat[CH:, LANES:], 0.0)], axis=0)
        s["rb"] = jnp.where(mk["incl"], amat[CH:, :LANES], 0.0)
    yield
    for s in st:
        xv = mm(s["akrk"], bd(s["it"]["load"]("v")))
        s["x1"], s["y0v"] = xv[:CH], xv[CH:]
    yield
    for s in st:
        s["p"] = jnp.where(s["mk"]["m_d"], s["n"], 0.0)
        s["t"] = cst["eye_cat"] + s["p"]
        s["q"] = mm(s["p"], bd(s["p"]))
    yield
    for _ in range(2):
        for s in st:
            qt = mm(jnp.concatenate([s["q"], s["t"]], axis=0), bd(s["q"]))
            s["t"] = s["t"] + qt[CH:]
            s["q"] = qt[:CH]
        yield
    for s in st:
        s["t"] = s["t"] + mm(s["t"], bd(s["q"]))
    yield
    for key in ("m_o1", "m_o2"):
        for s in st:
            s["x"] = mm(s["t"], bd(jnp.where(s["mk"][key], s["n"], 0.0)))
        yield
        for s in st:
            s["t"] = s["t"] + mm(s["x"], bd(s["t"]))
        yield
    for s in st:
        au = mm(s["t"], jnp.concatenate([bd(s["it"]["load"]("at")), bd(s["x1"])], axis=1))
        s["ah"], s["u0"] = au[:, :LANES], au[:, LANES:]
    yield
    for s in st:
        ld = s["it"]["load"]
        w_tot = s["it"]["w_tot"]()
        ry = mm(s["rb"], jnp.concatenate([bd(s["ah"]), bd(s["u0"])], axis=1))
        rhs2 = jnp.concatenate([jnp.concatenate([s["ah"], s["u0"]], axis=1),
                                jnp.concatenate([jnp.zeros((CH, LANES), F32), ld("v")], axis=1)], axis=0)
        kb_end = jnp.concatenate([ld("bt") * w_tot, ld("kt") * w_tot], axis=0)
        mg = mm(kb_end, rhs2, _TN)
        m = jnp.where(bdmask, mg[:, :LANES], 0.0) + jnp.where(cst["eye128"], w_tot, 0.0)
        g = jnp.where(bdmask, mg[:, LANES:], 0.0)
        s["it"]["store"](ld("rt") + ry[:, :LANES], ry[:, LANES:] + s["y0v"], m, g)
    yield


def _run(gen):
    for _ in gen:
        pass


def _interleave(main, side, ratio):
    for _ in main:
        for _ in range(ratio):
            next(side, None)
    _run(side)


def _chain(gens):
    for g in gens:
        yield from g


def _sigmoid(x):
    return 0.5 * jnp.tanh(0.5 * x) + 0.5


def _shifted(ref, start, n, mp, mn, seq):
    sub = 8
    edge = _iota((sub, LANES), 0)
    cur = ref[start:start + n, :]
    if start > 0:
        prev = ref[start - 1:start - 1 + n, :]
    else:
        head = jnp.where(edge == 0, 0.0, pltpu.roll(ref[0:sub, :], 1, 0))
        prev = jnp.concatenate([head, ref[sub - 1:n - 1, :]], axis=0)
    if start + n < seq:
        nxt = ref[start + 1:start + 1 + n, :]
    else:
        tail = jnp.where(edge == sub - 1, 0.0, pltpu.roll(ref[seq - sub:seq, :], sub - 1, 0))
        nxt = jnp.concatenate([ref[start + 1:seq - sub + 1, :], tail], axis=0)
    return cur * (1.0 - mp - mn) + mp * prev + mn * nxt


def _chunk_cumsum(x, rev):
    n = x.shape[0]
    sub = 8
    pos = _iota((n, LANES), 0) % CH
    s = 1
    while s < sub:
        if rev:
            x = x + jnp.where(pos < CH - s, pltpu.roll(x, n - s, 0), 0.0)
        else:
            x = x + jnp.where(pos >= s, pltpu.roll(x, s, 0), 0.0)
        s *= 2
    g = CH // sub
    x = x.reshape(n // CH, g, sub, LANES)
    s = 1
    while s < g:
        z = jnp.zeros((n // CH, s, sub, LANES), F32)
        if rev:
            x = x + jnp.concatenate([x[:, s:], z], axis=1)
        else:
            x = x + jnp.concatenate([z, x[:, :-s]], axis=1)
        s *= 2
    return x.reshape(n, LANES)


(V_MPR, V_MPK, V_MPV, V_MNR, V_MNK, V_MNV, V_DBF, V_DBB, V_IBF, V_IBB,
 V_KK, V_KA, V_RK, V_LG, V_LB, V_VB) = range(16)
W_DEC, W_ICL, W_VR = range(3)
I_RT, I_AT, I_KT, I_BT = range(4)
I_NAMES = dict(rt=I_RT, at=I_AT, kt=I_KT, bt=I_BT)


def _wkv_kernel(*refs, seq, first):
    if first:
        (r_ref, k_ref, v_ref, ga_ref, dec_ref, icl_ref, vec_ref, mul_ref, wl_ref,
         ya_ref, vf_out_ref, lhs_s, add_s, yf_s, yb_s, bon_s, it_s, v_s, wt_s) = refs
        vres_ref = vf_ref = None
    else:
        (r_ref, k_ref, v_ref, ga_ref, dec_ref, icl_ref, vres_ref, vf_ref, vec_ref, mul_ref, wl_ref,
         ya_ref, lhs_s, add_s, yf_s, yb_s, bon_s, it_s, v_s, wt_s) = refs
        vf_out_ref = None
    ns = seq // SLAB
    per = SLAB // CH
    cst = _wkv_consts()
    ones_bd = jnp.where(cst["bd"], 1.0, 0.0).astype(BF16)

    def vec(i):
        return vec_ref[i:i + 1, :]

    def prologue(sb):
        start = sb * SLAB
        rows = slice(start, start + SLAB)
        r = _shifted(r_ref, start, SLAB, vec(V_MPR), vec(V_MNR), seq)
        yield
        k = _shifted(k_ref, start, SLAB, vec(V_MPK), vec(V_MNK), seq)
        yield
        v = _shifted(v_ref, start, SLAB, vec(V_MPV), vec(V_MNV), seq)
        yield
        dec = _shifted(dec_ref, start, SLAB, mul_ref[0:1, :], mul_ref[2:3, :], seq)
        icl = _shifted(icl_ref, start, SLAB, mul_ref[1:2, :], mul_ref[3:4, :], seq)
        yield
        if first:
            vf_out_ref[rows, :] = v
        else:
            mix = _sigmoid(vec(V_VB) + _mm(vres_ref[rows, :], wl_ref[W_VR], prec=P_LORA)[:, :LANES])
            v = v + (vf_ref[rows, :] - v) * mix
        v_s[rows, :] = v
        yield
        zd = _mm(jnp.tanh(dec), wl_ref[W_DEC], prec=P_LORA)
        zi = _mm(icl, wl_ref[W_ICL], prec=P_LORA)
        kk = k * vec(V_KK)
        yield
        kds, ags, lws = [], [], []
        kh = k * (0.5 * vec(V_KA))
        kb2 = k - kh
        for d in range(2):
            th_w = jnp.tanh(vec(V_DBF + d) + zd[:, d * LANES:(d + 1) * LANES])
            lws.append((-0.5 * EXP_M_HALF) * th_w - 0.5 * EXP_M_HALF)
            th_a = jnp.tanh(vec(V_IBF + d) + zi[:, d * LANES:(d + 1) * LANES])
            ags.append(0.5 * th_a + 0.5)
            kds.append(kb2 + kh * th_a)
            yield
        ssq, bsum = _headsum([kk * kk, r * (kds[0] + kds[1]) * vec(V_RK)], ones_bd)
        kk = kk * lax.rsqrt(ssq + 1e-12)
        bon_s[rows, :] = bsum * v
        yield
        for d in range(2):
            cum = _chunk_cumsum(lws[d], d == 1)
            yield
            e_cum = jnp.exp(cum)
            e_inv = 1.0 / e_cum
            it_s[d, I_RT, rows, :] = r * e_cum
            it_s[d, I_AT, rows, :] = -kk * jnp.exp(cum - lws[d])
            yield
            it_s[d, I_KT, rows, :] = kds[d] * e_inv
            it_s[d, I_BT, rows, :] = (kk * ags[d]) * e_inv
            for i in range(per):
                c = sb * per + i
                row = i * CH if d == 1 else i * CH + CH - 1
                wt_s[d, c * 8:(c + 1) * 8, :] = jnp.broadcast_to(jnp.exp(cum[row:row + 1, :]), (8, LANES))
            yield

    def chunk_items(step):
        buf = step % 2
        items = []
        for d in range(2):
            slab = step if d == 0 else ns - 1 - step
            for i in range(per):
                c = slab * per + i
                crow = slice(c * CH, (c + 1) * CH)

                def load(name, d=d, crow=crow):
                    if name == "v":
                        return v_s[crow, :]
                    return it_s[d, I_NAMES[name], crow, :]

                def w_tot(d=d, c=c):
                    return wt_s[d, c * 8:c * 8 + 1, :]

                def store(rh, y0, m, g, d=d, i=i):
                    lhs_s[buf, d, i, :CH, :] = rh
                    lhs_s[buf, d, i, CH:, :] = m
                    add_s[buf, d, i, :CH, :] = y0
                    add_s[buf, d, i, CH:, :] = g

                items.append(dict(rev=d == 1, load=load, w_tot=w_tot, store=store))
        return items

    state = [jnp.zeros((LANES, LANES), F32), jnp.zeros((LANES, LANES), F32)]

    def scan(step):
        buf = step % 2
        for j in range(per):
            for d in range(2):
                i = j if d == 0 else per - 1 - j
                slab = step if d == 0 else ns - 1 - step
                o = _mm(lhs_s[buf, d, i], state[d], prec=P_WKV) + add_s[buf, d, i]
                rows = slice((slab * per + i) * CH, (slab * per + i + 1) * CH)
                (yf_s if d == 0 else yb_s)[rows, :] = o[:CH]
                state[d] = o[CH:]
            yield

    def post(start, n):
        rows = slice(start, start + n)
        y = yf_s[rows, :] + yb_s[rows, :]
        mu = _headsum([y], ones_bd)[0] * (1.0 / HEAD)
        yield
        dv = y - mu
        var = _headsum([dv * dv], ones_bd)[0] * (1.0 / HEAD)
        yield
        out = dv * lax.rsqrt(var + GN_EPS) * vec(V_LG) + vec(V_LB) + bon_s[rows, :]
        ga = ga_ref[rows, :]
        ya_ref[rows, :] = (out * (ga * _sigmoid(ga))).astype(BF16)
        yield

    ends = sorted({0, ns - 1})
    inner = [sb for sb in range(ns) if sb not in ends]
    for sb in ends:
        _run(prologue(sb))
    for step in range(ns):
        side = _chain([prologue(sb) for sb in inner]) if step == 0 else scan(step - 1)
        _interleave(_wkv_mxu_steps(chunk_items(step), cst), side, 3 if step == 0 else 1)
    _interleave(scan(ns - 1), _chain([post(sb * SLAB, SLAB) for sb in inner]), 1)
    for sb in ends:
        _run(post(sb * SLAB, SLAB))


def _wkv_call(proj3, vfirst, vecs, mul, wl, *, first):
    bsz, seq, _ = proj3.shape
    nc = seq // CH
    per = SLAB // CH

    def col(base):
        blk = base // LANES
        return pl.BlockSpec((None, seq, LANES), lambda b, p, blk=blk: (b, 0, blk + p))

    def fixed(base):
        blk = base // LANES
        return pl.BlockSpec((None, seq, LANES), lambda b, p, blk=blk: (b, 0, blk))

    in_specs = [col(C_R), col(C_K), col(C_V), col(C_GA), fixed(C_DEC), fixed(C_ICL)]
    args = [proj3] * 6
    if not first:
        in_specs += [fixed(C_VRES), pl.BlockSpec((None, seq, LANES), lambda b, p: (b, 0, p))]
        args += [proj3, vfirst]
    in_specs += [pl.BlockSpec((16, LANES), lambda b, p: (0, p)),
                 pl.BlockSpec((8, LANES), lambda b, p: (0, 0)),
                 pl.BlockSpec((None, 3, LANES, 2 * LANES), lambda b, p: (p, 0, 0, 0))]
    args += [vecs, mul, wl]
    out_block = pl.BlockSpec((None, seq, LANES), lambda b, p: (b, 0, p))
    out_sds = jax.ShapeDtypeStruct((bsz, seq, D), BF16)
    out_shape = (out_sds, jax.ShapeDtypeStruct((bsz, seq, D), F32)) if first else out_sds
    out_specs = (out_block, out_block) if first else out_block
    scratch = [pltpu.VMEM((2, 2, per, CH + LANES, LANES), F32),
               pltpu.VMEM((2, 2, per, CH + LANES, LANES), F32),
               pltpu.VMEM((seq, LANES), F32), pltpu.VMEM((seq, LANES), F32), pltpu.VMEM((seq, LANES), F32),
               pltpu.VMEM((2, 4, seq, LANES), F32),
               pltpu.VMEM((seq, LANES), F32),
               pltpu.VMEM((2, nc * 8, LANES), F32)]
    return pl.pallas_call(
        functools.partial(_wkv_kernel, seq=seq, first=first),
        grid=(bsz, D // LANES),
        in_specs=in_specs, out_specs=out_specs, out_shape=out_shape,
        scratch_shapes=scratch,
        compiler_params=pltpu.CompilerParams(dimension_semantics=("arbitrary", "arbitrary"),
                                             vmem_limit_bytes=VMEM_LIMIT),
        name="wkv7",
    )(*args)


def _rms(x, g):
    ms = jnp.mean(x * x, axis=-1, keepdims=True)
    return x * lax.rsqrt(ms + NORM_EPS) * g


def _proj_kernel(xn_ref, w_ref, o_ref):
    o_ref[...] = jnp.dot(xn_ref[...], w_ref[...], preferred_element_type=F32).astype(o_ref.dtype)


PROJ_BLOCK_BUDGET = 40 * 1024 * 1024


def _proj_tiles(t, n, out_bytes):
    best = None
    for tm in (1024, 512, 256):
        if t % tm:
            continue
        for tn in range(LANES, n + 1, LANES):
            if n % tn == 0 and 2 * (tm * D * 2 + D * tn * 2 + tm * tn * out_bytes) <= PROJ_BLOCK_BUDGET:
                if best is None or tm * tn > best[0] * best[1]:
                    best = (tm, tn)
    return best


def _proj_call(xn2, w, out_dtype):
    t = xn2.shape[0]
    n = w.shape[1]
    tm, tn = _proj_tiles(t, n, jnp.dtype(out_dtype).itemsize)
    return pl.pallas_call(
        _proj_kernel,
        grid=(n // tn, t // tm),
        in_specs=[pl.BlockSpec((tm, D), lambda j, i: (i, 0)),
                  pl.BlockSpec((D, tn), lambda j, i: (0, j))],
        out_specs=pl.BlockSpec((tm, tn), lambda j, i: (i, j)),
        out_shape=jax.ShapeDtypeStruct((t, n), out_dtype),
        compiler_params=pltpu.CompilerParams(dimension_semantics=("arbitrary", "arbitrary"),
                                             vmem_limit_bytes=VMEM_LIMIT),
        name="proj",
    )(xn2, w)


def _ret_kernel(q_ref, k_ref, v_ref, gb_ref, cos_ref, sin_ref, lg_ref, gain_ref, o_ref, qr_s, kr_s, y_s, *, seq):
    nc = seq // RET_C
    p = pl.program_id(1)
    scale = HEAD ** -0.5

    def rot(c, _):
        sl = pl.ds(pl.multiple_of(c * RET_C, RET_C), RET_C)
        cs, sn = cos_ref[sl, :], sin_ref[sl, :]
        q = q_ref[sl, :].astype(F32)
        k = k_ref[sl, :].astype(F32)
        qr_s[sl, :] = (q * cs + pltpu.roll(q, 64, 1) * sn) * scale
        kr_s[sl, :] = k * cs + pltpu.roll(k, 64, 1) * sn
        return 0

    lax.fori_loop(0, nc, rot, 0)

    lane = _iota((RET_C, LANES), 1)
    rowi = _iota((RET_C, LANES), 0).astype(F32)
    absd = jnp.abs(_iota((RET_C, RET_C), 0) - _iota((RET_C, RET_C), 1)).astype(F32)
    for j in range(2):
        hmask = ((lane // 32) % 2) == j
        lg = lg_ref[pl.ds(2 * p + j, 1), :]
        lg128 = lg[:, :LANES]
        dmat = jnp.exp(absd * lg)
        xi_f = jnp.exp((rowi + 1.0) * lg128)
        ze_f = jnp.exp((RET_C - 1.0 - rowi) * lg128)
        xi_b = jnp.exp((RET_C - rowi) * lg128)
        ze_b = jnp.exp(rowi * lg128)
        cdec = jnp.exp(RET_C * lg128)
        vs = slice(j * LANES, (j + 1) * LANES)
        fst = jnp.zeros((LANES, LANES), F32)
        for i in range(nc):
            sl = slice(i * RET_C, (i + 1) * RET_C)
            qh = jnp.where(hmask, qr_s[sl, :], 0.0)
            kh = kr_s[sl, :]
            vh = v_ref[sl, vs]
            sc = _mm(qh, kh, _NT) * dmat
            y = _mm(sc, vh)
            if i > 0:
                y = y + _mm(qh * xi_f, fst)
            if i < nc - 1:
                fst = fst * cdec + _mm(jnp.where(hmask, kh, 0.0) * ze_f, vh, _TN)
            y_s[sl, vs] = y
        gst = jnp.zeros((LANES, LANES), F32)
        for i in range(nc - 1, -1, -1):
            sl = slice(i * RET_C, (i + 1) * RET_C)
            y = y_s[sl, vs]
            if i < nc - 1:
                qh = jnp.where(hmask, qr_s[sl, :], 0.0)
                y = y + _mm(qh * xi_b, gst)
            if i > 0:
                kh = jnp.where(hmask, kr_s[sl, :], 0.0)
                gst = gst * cdec + _mm(kh * ze_b, v_ref[sl, vs], _TN)
            mu = jnp.mean(y, axis=-1, keepdims=True)
            dv = y - mu
            var = jnp.mean(dv * dv, axis=-1, keepdims=True)
            out = dv * lax.rsqrt(var + NORM_EPS) * gain_ref[:, vs]
            o_ref[sl, vs] = (out * jax.nn.silu(gb_ref[sl, vs].astype(F32))).astype(BF16)


def _ret_call(proj3, cos_t, sin_t, lg_t, gain):
    bsz, seq, _ = proj3.shape
    w2 = 2 * LANES

    def spec(base, width):
        blk = base // width
        return pl.BlockSpec((None, seq, width), lambda b, p, blk=blk: (b, 0, blk + p))

    return pl.pallas_call(
        functools.partial(_ret_kernel, seq=seq),
        grid=(bsz, RET_HEADS // 2),
        in_specs=[spec(C_QB, LANES), spec(C_KB, LANES), spec(C_VB, w2), spec(C_GB, w2),
                  pl.BlockSpec((seq, LANES), lambda b, p: (0, 0)),
                  pl.BlockSpec((seq, LANES), lambda b, p: (0, 0)),
                  pl.BlockSpec((RET_HEADS, w2), lambda b, p: (0, 0)),
                  pl.BlockSpec((1, w2), lambda b, p: (0, p))],
        out_specs=pl.BlockSpec((None, seq, w2), lambda b, p: (b, 0, p)),
        out_shape=jax.ShapeDtypeStruct((bsz, seq, D), BF16),
        scratch_shapes=[pltpu.VMEM((seq, LANES), F32), pltpu.VMEM((seq, LANES), F32), pltpu.VMEM((seq, w2), F32)],
        compiler_params=pltpu.CompilerParams(dimension_semantics=("arbitrary", "arbitrary"),
                                             vmem_limit_bytes=VMEM_LIMIT),
        name="retention",
    )(proj3, proj3, proj3, proj3, cos_t, sin_t, lg_t, gain)


def _merge_kernel(x_ref, ya_ref, yb_ref, ma_ref, mb_ref, wa_ref, wb_ref, wo_ref, g_ref, *o_refs, last):
    za = jnp.dot(ya_ref[...], wa_ref[...], preferred_element_type=F32)
    zb = jnp.dot(yb_ref[...], wb_ref[...], preferred_element_type=F32)
    merged = jax.nn.sigmoid(ma_ref[...].astype(F32)) * za + jax.nn.sigmoid(mb_ref[...].astype(F32)) * zb
    x = x_ref[...] + jnp.dot(merged.astype(BF16), wo_ref[...], preferred_element_type=F32)
    if last:
        o_refs[0][...] = _rms(x, g_ref[...])
    else:
        o_refs[0][...] = x
        o_refs[1][...] = _rms(x, g_ref[...]).astype(BF16)


def _merge_call(x2, ya2, yb2, proj2, wa, wb, wo, gain, *, last):
    t = x2.shape[0]
    tm = min(512, t)
    row = pl.BlockSpec((tm, D), lambda i: (i, 0))
    wspec = pl.BlockSpec((D, D), lambda i: (0, 0))
    x_sds = jax.ShapeDtypeStruct((t, D), F32)
    return pl.pallas_call(
        functools.partial(_merge_kernel, last=last),
        grid=(t // tm,),
        in_specs=[row, row, row,
                  pl.BlockSpec((tm, D), lambda i: (i, C_MA // D)),
                  pl.BlockSpec((tm, D), lambda i: (i, C_MB // D)),
                  wspec, wspec, wspec, pl.BlockSpec((1, D), lambda i: (0, 0))],
        out_specs=row if last else (row, row),
        out_shape=x_sds if last else (x_sds, jax.ShapeDtypeStruct((t, D), BF16)),
        compiler_params=pltpu.CompilerParams(dimension_semantics=("arbitrary",),
                                             vmem_limit_bytes=VMEM_LIMIT),
        name="merge",
    )(x2, ya2, yb2, proj2, proj2, wa, wb, wo, gain)


def _norm_kernel(x_ref, g_ref, o_ref):
    o_ref[...] = _rms(x_ref[...], g_ref[...]).astype(BF16)


def _norm_call(x2, gain):
    t = x2.shape[0]
    tm = min(1024, t)
    return pl.pallas_call(
        _norm_kernel,
        grid=(t // tm,),
        in_specs=[pl.BlockSpec((tm, D), lambda i: (i, 0)), pl.BlockSpec((1, D), lambda i: (0, 0))],
        out_specs=pl.BlockSpec((tm, D), lambda i: (i, 0)),
        out_shape=jax.ShapeDtypeStruct((t, D), BF16),
        compiler_params=pltpu.CompilerParams(dimension_semantics=("arbitrary",)),
        name="first_norm",
    )(x2, gain)


def _qk_perm():
    idx = []
    for pair in range(RET_HEADS // 2):
        for lane in range(LANES):
            seg, i = lane // 32, lane % 32
            head, half = 2 * pair + seg % 2, seg // 2
            idx.append(head * HEAD + half * 32 + i)
    return np.asarray(idx, np.int32)


def _pack_w_in(w_in, w_vres_down):
    depth = w_in.shape[0]
    perm = _qk_perm()
    vres = jnp.concatenate([jnp.zeros((1, D, 32), F32), w_vres_down], axis=0)
    vres = jnp.pad(vres, ((0, 0), (0, 0), (0, LANES - 32)))
    w_wkv = jnp.concatenate([w_in[:, :, 0:3072], w_in[:, :, 3328:4352], w_in[:, :, 3072:3328], vres], axis=2)
    w_rest = jnp.concatenate([w_in[:, :, 5376:6400], w_in[:, :, 6400:7424], w_in[:, :, 7424:8448],
                              w_in[:, :, 8448:9472], w_in[:, :, 4352:4864][:, :, perm],
                              w_in[:, :, 4864:5376][:, :, perm]], axis=2)
    assert w_wkv.shape == (depth, D, N_WKV) and w_rest.shape == (depth, D, N_REST)
    return w_wkv.astype(BF16), w_rest.astype(BF16)


def _pair_cat(wf, wb):
    depth, pairs = wf.shape[0], D // LANES
    cat = jnp.concatenate([wf.reshape(depth, LANES, pairs, LANES), wb.reshape(depth, LANES, pairs, LANES)], axis=3)
    return cat.transpose(0, 2, 1, 3)


def _wkv_params(shift_prev, shift_next, w_decay_up, decay_bias, w_iclr_up, iclr_bias, w_vres_up, vres_bias,
                k_k, k_a, r_k, lnx_gain, lnx_bias):
    depth = k_k.shape[0]
    vb = jnp.concatenate([jnp.zeros((1, D), F32), vres_bias], axis=0)
    vecs = jnp.stack([shift_prev[:, 0:1024], shift_prev[:, 1024:2048], shift_prev[:, 2048:3072],
                      shift_next[:, 0:1024], shift_next[:, 1024:2048], shift_next[:, 2048:3072],
                      0.5 * decay_bias[:, 0], 0.5 * decay_bias[:, 1], 0.5 * iclr_bias[:, 0], 0.5 * iclr_bias[:, 1],
                      k_k, k_a, r_k.reshape(depth, D), lnx_gain, lnx_bias, vb], axis=1)
    mul = jnp.concatenate([jnp.stack([shift_prev[:, 3072:3200], shift_prev[:, 3200:3328],
                                      shift_next[:, 3072:3200], shift_next[:, 3200:3328]], axis=1),
                           jnp.zeros((depth, 4, LANES), F32)], axis=1)
    zpad = jnp.zeros((depth, 64, D), F32)
    zmat = jnp.zeros((depth, LANES, D), F32)
    wvr = jnp.concatenate([jnp.zeros((1, LANES, D), F32),
                           jnp.pad(w_vres_up, ((0, 0), (0, LANES - 32), (0, 0)))], axis=0)
    wl = jnp.stack([0.5 * _pair_cat(jnp.concatenate([w_decay_up[:, 0], zpad], axis=1),
                                    jnp.concatenate([zpad, w_decay_up[:, 1]], axis=1)),
                    0.5 * _pair_cat(jnp.concatenate([w_iclr_up[:, 0], zpad], axis=1),
                                    jnp.concatenate([zpad, w_iclr_up[:, 1]], axis=1)),
                    _pair_cat(wvr, zmat)], axis=2)
    return vecs, mul, wl


def _rope_tables(seq):
    half = 32
    freqs = jnp.power(ROPE_BASE, -jnp.arange(half, dtype=F32) / half)
    ang = jnp.arange(seq, dtype=F32)[:, None] * freqs[None, :]
    cos = jnp.tile(jnp.cos(ang), (1, 4))
    sin = jnp.tile(jnp.sin(ang), (1, 4))
    sign = jnp.where(jnp.arange(LANES) < 64, -1.0, 1.0).astype(F32)
    return cos, sin * sign[None, :]


def kernel(x, norm_gain, w_in, w_vres_down, shift_prev, shift_next, w_decay_up, decay_bias, w_iclr_up, iclr_bias,
           w_vres_up, vres_bias, k_k, k_a, r_k, lnx_gain, lnx_bias, w_branch_a, ret_norm_gain, w_branch_b, w_out,
           final_gain):
    bsz, seq, _ = x.shape
    depth = w_in.shape[0]
    assert seq % RET_C == 0 and seq % SLAB == 0 and SLAB % CH == 0
    t = bsz * seq

    w_wkv, w_rest = _pack_w_in(w_in, w_vres_down)
    wa_all, wb_all, wo_all = (w.astype(BF16) for w in (w_branch_a, w_branch_b, w_out))
    cos_t, sin_t = _rope_tables(seq)
    log_g = jnp.log1p(-jnp.exp2(-5.0 - jnp.arange(RET_HEADS, dtype=F32)))
    lg_t = jnp.broadcast_to(log_g[:, None], (RET_HEADS, 2 * LANES))
    vecs_all, mul_all, wl_all = _wkv_params(shift_prev, shift_next, w_decay_up, decay_bias, w_iclr_up, iclr_bias,
                                            w_vres_up, vres_bias, k_k, k_a, r_k, lnx_gain, lnx_bias)

    x2 = x.reshape(t, D)
    xn2 = _norm_call(x2, norm_gain[0][None, :])
    vfirst = None
    for l in range(depth):
        proj3 = _proj_call(xn2, w_wkv[l], F32).reshape(bsz, seq, N_WKV)
        rest2 = _proj_call(xn2, w_rest[l], BF16)
        rest3 = rest2.reshape(bsz, seq, N_REST)

        vecs, mul, wl = vecs_all[l], mul_all[l], wl_all[l]

        if l == 0:
            ya, vfirst = _wkv_call(proj3, None, vecs, mul, wl, first=True)
        else:
            ya = _wkv_call(proj3, vfirst, vecs, mul, wl, first=False)
        yb = _ret_call(rest3, cos_t, sin_t, lg_t, ret_norm_gain[l][None, :])
        last = l == depth - 1
        gain = final_gain if last else norm_gain[l + 1]
        res = _merge_call(x2, ya.reshape(t, D), yb.reshape(t, D), rest2, wa_all[l], wb_all[l], wo_all[l],
                          gain[None, :], last=last)
        if not last:
            x2, xn2 = res

    return res.reshape(bsz, seq, D)
```

```python
import functools

import numpy as np
import jax
import jax.numpy as jnp
from jax import lax
from jax.experimental import pallas as pl
from jax.experimental.pallas import tpu as pltpu

F32 = jnp.float32
BF16 = jnp.bfloat16

D = 1024
HEAD = 64
LANES = 128
CH = 64
SLAB = 512
RET_HEADS = 8
RET_C = 256
NORM_EPS = 1e-6
GN_EPS = 64e-5
EXP_M_HALF = 0.6065306597126334
ROPE_BASE = 10000.0
VMEM_LIMIT = 56 * 1024 * 1024

C_R, C_K, C_V, C_GA, C_DEC, C_ICL, C_VRES, N_WKV = 0, 1024, 2048, 3072, 4096, 4224, 4352, 4480
C_VB, C_GB, C_MA, C_MB, C_QB, C_KB, N_REST = 0, 1024, 2048, 3072, 4096, 4608, 5120

P_LORA = 1
P_WKV = 1

_NN = (((1,), (0,)), ((), ()))
_NT = (((1,), (1,)), ((), ()))
_TN = (((0,), (0,)), ((), ()))


def _dg(a, b, dims):
    return lax.dot_general(a, b, dims, preferred_element_type=F32)


def _split2(x):
    hi = x.astype(BF16)
    lo = (x - hi.astype(F32)).astype(BF16)
    return hi, lo


def _mm(a, b, dims=_NN, prec=1):
    if prec == 1:
        return _dg(a.astype(BF16), b.astype(BF16), dims)
    ah, al = _split2(a)
    bh, bl = _split2(b)
    return _dg(ah, bh, dims) + (_dg(ah, bl, dims) + _dg(al, bh, dims))


def _headsum(xs, ones_bd):
    return [_dg(x.astype(BF16), ones_bd, _NN) for x in xs]


def _iota(shape, axis):
    return lax.broadcasted_iota(jnp.int32, shape, axis)


def _pair_bd(x, bdmask):
    return jnp.where(bdmask, jnp.concatenate([x, x], axis=0), 0.0)


def _wkv_consts():
    row = _iota((CH, LANES), 0)
    col = _iota((CH, LANES), 1) % CH
    r128 = _iota((LANES, LANES), 0)
    c128 = _iota((LANES, LANES), 1)

    def masks(lower):
        late, early = (row, col) if lower else (col, row)
        return dict(
            strict=early < late, incl=early <= late,
            m_d=(early < late) & ((row // 16) == (col // 16)),
            m_o1=((row // 32) == (col // 32)) & ((late // 16) % 2 == 1) & ((early // 16) % 2 == 0),
            m_o2=((late // 32) == 1) & ((early // 32) == 0))

    return dict(bd=(r128 // CH) == (c128 // CH), eye128=r128 == c128,
                eye_cat=jnp.where(row == col, 1.0, 0.0).astype(F32),
                fwd=masks(True), bwd=masks(False))


def _wkv_mxu_steps(items, cst):
    bdmask = cst["bd"]

    def bd(x):
        return _pair_bd(x, bdmask)

    def mm(a, b, dims=_NN):
        return _mm(a, b, dims, prec=P_WKV)

    st = [dict(it=it, mk=cst["bwd"] if it["rev"] else cst["fwd"]) for it in items]
    for s in st:
        ld = s["it"]["load"]
        amat = mm(jnp.concatenate([ld("at"), ld("rt")], axis=0),
                  jnp.concatenate([bd(ld("bt")), bd(ld("kt"))], axis=0), _NT)
        mk = s["mk"]
        s["n"] = amat[:CH, :LANES]
        s["akrk"] = jnp.concatenate([jnp.where(mk["strict"], amat[:CH, LANES:], 0.0),
                                     jnp.where(mk["incl"], amat[CH:, LANES:], 0.0)], axis=0)
        s["rb"] = jnp.where(mk["incl"], amat[CH:, :LANES], 0.0)
    yield
    for s in st:
        xv = mm(s["akrk"], bd(s["it"]["load"]("v")))
        s["x1"], s["y0v"] = xv[:CH], xv[CH:]
    yield
    for s in st:
        s["p"] = jnp.where(s["mk"]["m_d"], s["n"], 0.0)
        s["t"] = cst["eye_cat"] + s["p"]
        s["q"] = mm(s["p"], bd(s["p"]))
    yield
    for _ in range(2):
        for s in st:
            qt = mm(jnp.concatenate([s["q"], s["t"]], axis=0), bd(s["q"]))
            s["t"] = s["t"] + qt[CH:]
            s["q"] = qt[:CH]
        yield
    for s in st:
        s["t"] = s["t"] + mm(s["t"], bd(s["q"]))
    yield
    for key in ("m_o1", "m_o2"):
        for s in st:
            s["x"] = mm(s["t"], bd(jnp.where(s["mk"][key], s["n"], 0.0)))
        yield
        for s in st:
            s["t"] = s["t"] + mm(s["x"], bd(s["t"]))
        yield
    for s in st:
        au = mm(s["t"], jnp.concatenate([bd(s["it"]["load"]("at")), bd(s["x1"])], axis=1))
        s["ah"], s["u0"] = au[:, :LANES], au[:, LANES:]
    yield
    for s in st:
        ld = s["it"]["load"]
        w_tot = s["it"]["w_tot"]()
        ry = mm(s["rb"], jnp.concatenate([bd(s["ah"]), bd(s["u0"])], axis=1))
        rhs2 = jnp.concatenate([jnp.concatenate([s["ah"], s["u0"]], axis=1),
                                jnp.concatenate([jnp.zeros((CH, LANES), F32), ld("v")], axis=1)], axis=0)
        kb_end = jnp.concatenate([ld("bt") * w_tot, ld("kt") * w_tot], axis=0)
        mg = mm(kb_end, rhs2, _TN)
        m = jnp.where(bdmask, mg[:, :LANES], 0.0) + jnp.where(cst["eye128"], w_tot, 0.0)
        g = jnp.where(bdmask, mg[:, LANES:], 0.0)
        s["it"]["store"](ld("rt") + ry[:, :LANES], ry[:, LANES:] + s["y0v"], m, g)
    yield


def _run(gen):
    for _ in gen:
        pass


def _interleave(main, side, ratio):
    for _ in main:
        for _ in range(ratio):
            next(side, None)
    _run(side)


def _chain(gens):
    for g in gens:
        yield from g


def _sigmoid(x):
    return 0.5 * jnp.tanh(0.5 * x) + 0.5


def _shifted(ref, start, n, mp, mn, seq):
    sub = 8
    edge = _iota((sub, LANES), 0)
    cur = ref[start:start + n, :]
    if start > 0:
        prev = ref[start - 1:start - 1 + n, :]
    else:
        head = jnp.where(edge == 0, 0.0, pltpu.roll(ref[0:sub, :], 1, 0))
        prev = jnp.concatenate([head, ref[sub - 1:n - 1, :]], axis=0)
    if start + n < seq:
        nxt = ref[start + 1:start + 1 + n, :]
    else:
        tail = jnp.where(edge == sub - 1, 0.0, pltpu.roll(ref[seq - sub:seq, :], sub - 1, 0))
        nxt = jnp.concatenate([ref[start + 1:seq - sub + 1, :], tail], axis=0)
    return cur * (1.0 - mp - mn) + mp * prev + mn * nxt


def _chunk_cumsum(x, rev):
    n = x.shape[0]
    sub = 8
    pos = _iota((n, LANES), 0) % CH
    s = 1
    while s < sub:
        if rev:
            x = x + jnp.where(pos < CH - s, pltpu.roll(x, n - s, 0), 0.0)
        else:
            x = x + jnp.where(pos >= s, pltpu.roll(x, s, 0), 0.0)
        s *= 2
    g = CH // sub
    x = x.reshape(n // CH, g, sub, LANES)
    s = 1
    while s < g:
        z = jnp.zeros((n // CH, s, sub, LANES), F32)
        if rev:
            x = x + jnp.concatenate([x[:, s:], z], axis=1)
        else:
            x = x + jnp.concatenate([z, x[:, :-s]], axis=1)
        s *= 2
    return x.reshape(n, LANES)


(V_MPR, V_MPK, V_MPV, V_MNR, V_MNK, V_MNV, V_DBF, V_DBB, V_IBF, V_IBB,
 V_KK, V_KA, V_RK, V_LG, V_LB, V_VB) = range(16)
W_DEC, W_ICL, W_VR = range(3)
I_RT, I_AT, I_KT, I_BT = range(4)
I_NAMES = dict(rt=I_RT, at=I_AT, kt=I_KT, bt=I_BT)


def _wkv_kernel(*refs, seq, first):
    if first:
        (r_ref, k_ref, v_ref, ga_ref, dec_ref, icl_ref, vec_ref, mul_ref, wl_ref,
         ya_ref, vf_out_ref, lhs_s, add_s, yf_s, yb_s, bon_s, it_s, v_s, wt_s, lo_s) = refs
        vres_ref = vf_ref = None
    else:
        (r_ref, k_ref, v_ref, ga_ref, dec_ref, icl_ref, vres_ref, vf_ref, vec_ref, mul_ref, wl_ref,
         ya_ref, lhs_s, add_s, yf_s, yb_s, bon_s, it_s, v_s, wt_s, lo_s) = refs
        vf_out_ref = None
    ns = seq // SLAB
    per = SLAB // CH
    cst = _wkv_consts()
    ones_bd = jnp.where(cst["bd"], 1.0, 0.0).astype(BF16)

    def vec(i):
        return vec_ref[i:i + 1, :]

    @pl.when(pl.program_id(1) == 0)
    def _():
        for sb in range(ns):
            rows = slice(sb * SLAB, (sb + 1) * SLAB)
            lo_s[0, rows, :] = jnp.tanh(_shifted(dec_ref, sb * SLAB, SLAB, mul_ref[0:1, :], mul_ref[2:3, :], seq))
            lo_s[1, rows, :] = _shifted(icl_ref, sb * SLAB, SLAB, mul_ref[1:2, :], mul_ref[3:4, :], seq)

    def prologue(sb):
        start = sb * SLAB
        rows = slice(start, start + SLAB)
        r = _shifted(r_ref, start, SLAB, vec(V_MPR), vec(V_MNR), seq)
        yield
        k = _shifted(k_ref, start, SLAB, vec(V_MPK), vec(V_MNK), seq)
        yield
        v = _shifted(v_ref, start, SLAB, vec(V_MPV), vec(V_MNV), seq)
        yield
        if first:
            vf_out_ref[rows, :] = v
        else:
            mix = _sigmoid(vec(V_VB) + _mm(vres_ref[rows, :], wl_ref[W_VR], prec=P_LORA)[:, :LANES])
            v = v + (vf_ref[rows, :] - v) * mix
        v_s[rows, :] = v
        yield
        zd = _mm(lo_s[0, rows, :], wl_ref[W_DEC], prec=P_LORA)
        zi = _mm(lo_s[1, rows, :], wl_ref[W_ICL], prec=P_LORA)
        kk = k * vec(V_KK)
        yield
        kds, ags, lws = [], [], []
        kh = k * (0.5 * vec(V_KA))
        kb2 = k - kh
        for d in range(2):
            th_w = jnp.tanh(vec(V_DBF + d) + zd[:, d * LANES:(d + 1) * LANES])
            lws.append((-0.5 * EXP_M_HALF) * th_w - 0.5 * EXP_M_HALF)
            th_a = jnp.tanh(vec(V_IBF + d) + zi[:, d * LANES:(d + 1) * LANES])
            ags.append(0.5 * th_a + 0.5)
            kds.append(kb2 + kh * th_a)
            yield
        ssq, bsum = _headsum([kk * kk, r * (kds[0] + kds[1]) * vec(V_RK)], ones_bd)
        kk = kk * lax.rsqrt(ssq + 1e-12)
        bon_s[rows, :] = bsum * v
        yield
        for d in range(2):
            cum = _chunk_cumsum(lws[d], d == 1)
            yield
            e_cum = jnp.exp(cum)
            e_inv = 1.0 / e_cum
            it_s[d, I_RT, rows, :] = r * e_cum
            it_s[d, I_AT, rows, :] = -kk * jnp.exp(cum - lws[d])
            yield
            it_s[d, I_KT, rows, :] = kds[d] * e_inv
            it_s[d, I_BT, rows, :] = (kk * ags[d]) * e_inv
            for i in range(per):
                c = sb * per + i
                row = i * CH if d == 1 else i * CH + CH - 1
                wt_s[d, c * 8:(c + 1) * 8, :] = jnp.broadcast_to(jnp.exp(cum[row:row + 1, :]), (8, LANES))
            yield

    def chunk_items(step):
        buf = step % 2
        items = []
        for d in range(2):
            slab = step if d == 0 else ns - 1 - step
            for i in range(per):
                c = slab * per + i
                crow = slice(c * CH, (c + 1) * CH)

                def load(name, d=d, crow=crow):
                    if name == "v":
                        return v_s[crow, :]
                    return it_s[d, I_NAMES[name], crow, :]

                def w_tot(d=d, c=c):
                    return wt_s[d, c * 8:c * 8 + 1, :]

                def store(rh, y0, m, g, d=d, i=i):
                    lhs_s[buf, d, i, :CH, :] = rh
                    lhs_s[buf, d, i, CH:, :] = m
                    add_s[buf, d, i, :CH, :] = y0
                    add_s[buf, d, i, CH:, :] = g

                items.append(dict(rev=d == 1, load=load, w_tot=w_tot, store=store))
        return items

    state = [jnp.zeros((LANES, LANES), F32), jnp.zeros((LANES, LANES), F32)]

    def scan(step):
        buf = step % 2
        for j in range(per):
            for d in range(2):
                i = j if d == 0 else per - 1 - j
                slab = step if d == 0 else ns - 1 - step
                o = _mm(lhs_s[buf, d, i], state[d], prec=P_WKV) + add_s[buf, d, i]
                rows = slice((slab * per + i) * CH, (slab * per + i + 1) * CH)
                (yf_s if d == 0 else yb_s)[rows, :] = o[:CH]
                state[d] = o[CH:]
            yield

    def post(start, n):
        rows = slice(start, start + n)
        y = yf_s[rows, :] + yb_s[rows, :]
        mu = _headsum([y], ones_bd)[0] * (1.0 / HEAD)
        yield
        dv = y - mu
        var = _headsum([dv * dv], ones_bd)[0] * (1.0 / HEAD)
        yield
        out = dv * lax.rsqrt(var + GN_EPS) * vec(V_LG) + vec(V_LB) + bon_s[rows, :]
        ga = ga_ref[rows, :]
        ya_ref[rows, :] = (out * (ga * _sigmoid(ga))).astype(BF16)
        yield

    ends = sorted({0, ns - 1})
    inner = [sb for sb in range(ns) if sb not in ends]
    for sb in ends:
        _run(prologue(sb))
    for step in range(ns):
        side = _chain([prologue(sb) for sb in inner]) if step == 0 else scan(step - 1)
        _interleave(_wkv_mxu_steps(chunk_items(step), cst), side, 3 if step == 0 else 1)
    _interleave(scan(ns - 1), _chain([post(sb * SLAB, SLAB) for sb in inner]), 1)
    for sb in ends:
        _run(post(sb * SLAB, SLAB))


def _wkv_call(proj3, vfirst, vecs, mul, wl, *, first):
    bsz, seq, _ = proj3.shape
    nc = seq // CH
    per = SLAB // CH

    def col(base):
        blk = base // LANES
        return pl.BlockSpec((None, seq, LANES), lambda b, p, blk=blk: (b, 0, blk + p))

    def fixed(base):
        blk = base // LANES
        return pl.BlockSpec((None, seq, LANES), lambda b, p, blk=blk: (b, 0, blk))

    in_specs = [col(C_R), col(C_K), col(C_V), col(C_GA), fixed(C_DEC), fixed(C_ICL)]
    args = [proj3] * 6
    if not first:
        in_specs += [fixed(C_VRES), pl.BlockSpec((None, seq, LANES), lambda b, p: (b, 0, p))]
        args += [proj3, vfirst]
    in_specs += [pl.BlockSpec((16, LANES), lambda b, p: (0, p)),
                 pl.BlockSpec((8, LANES), lambda b, p: (0, 0)),
                 pl.BlockSpec((None, 3, LANES, 2 * LANES), lambda b, p: (p, 0, 0, 0))]
    args += [vecs, mul, wl]
    out_block = pl.BlockSpec((None, seq, LANES), lambda b, p: (b, 0, p))
    out_sds = jax.ShapeDtypeStruct((bsz, seq, D), BF16)
    out_shape = (out_sds, jax.ShapeDtypeStruct((bsz, seq, D), F32)) if first else out_sds
    out_specs = (out_block, out_block) if first else out_block
    scratch = [pltpu.VMEM((2, 2, per, CH + LANES, LANES), F32),
               pltpu.VMEM((2, 2, per, CH + LANES, LANES), F32),
               pltpu.VMEM((seq, LANES), F32), pltpu.VMEM((seq, LANES), F32), pltpu.VMEM((seq, LANES), F32),
               pltpu.VMEM((2, 4, seq, LANES), F32),
               pltpu.VMEM((seq, LANES), F32),
               pltpu.VMEM((2, nc * 8, LANES), F32),
               pltpu.VMEM((2, seq, LANES), F32)]
    return pl.pallas_call(
        functools.partial(_wkv_kernel, seq=seq, first=first),
        grid=(bsz, D // LANES),
        in_specs=in_specs, out_specs=out_specs, out_shape=out_shape,
        scratch_shapes=scratch,
        compiler_params=pltpu.CompilerParams(dimension_semantics=("arbitrary", "arbitrary"),
                                             vmem_limit_bytes=VMEM_LIMIT),
        name="wkv7",
    )(*args)


def _rms(x, g):
    ms = jnp.mean(x * x, axis=-1, keepdims=True)
    return x * lax.rsqrt(ms + NORM_EPS) * g


def _proj_kernel(xn_ref, w_ref, o_ref):
    o_ref[...] = jnp.dot(xn_ref[...], w_ref[...], preferred_element_type=F32).astype(o_ref.dtype)


PROJ_BLOCK_BUDGET = 40 * 1024 * 1024


def _proj_tiles(t, n, out_bytes):
    best = None
    for tm in (1024, 512, 256):
        if t % tm:
            continue
        for tn in range(LANES, n + 1, LANES):
            if n % tn == 0 and 2 * (tm * D * 2 + D * tn * 2 + tm * tn * out_bytes) <= PROJ_BLOCK_BUDGET:
                if best is None or tm * tn > best[0] * best[1]:
                    best = (tm, tn)
    return best


def _proj_call(xn2, w, out_dtype):
    t = xn2.shape[0]
    n = w.shape[1]
    tm, tn = _proj_tiles(t, n, jnp.dtype(out_dtype).itemsize)
    return pl.pallas_call(
        _proj_kernel,
        grid=(n // tn, t // tm),
        in_specs=[pl.BlockSpec((tm, D), lambda j, i: (i, 0)),
                  pl.BlockSpec((D, tn), lambda j, i: (0, j))],
        out_specs=pl.BlockSpec((tm, tn), lambda j, i: (i, j)),
        out_shape=jax.ShapeDtypeStruct((t, n), out_dtype),
        compiler_params=pltpu.CompilerParams(dimension_semantics=("arbitrary", "arbitrary"),
                                             vmem_limit_bytes=VMEM_LIMIT),
        name="proj",
    )(xn2, w)


def _ret_kernel(q_ref, k_ref, v_ref, gb_ref, cos_ref, sin_ref, lg_ref, gain_ref, o_ref, qr_s, kr_s, y_s, *, seq):
    nc = seq // RET_C
    p = pl.program_id(1)
    scale = HEAD ** -0.5

    def rot(c, _):
        sl = pl.ds(pl.multiple_of(c * RET_C, RET_C), RET_C)
        cs, sn = cos_ref[sl, :], sin_ref[sl, :]
        q = q_ref[sl, :].astype(F32)
        k = k_ref[sl, :].astype(F32)
        qr_s[sl, :] = (q * cs + pltpu.roll(q, 64, 1) * sn) * scale
        kr_s[sl, :] = k * cs + pltpu.roll(k, 64, 1) * sn
        return 0

    lax.fori_loop(0, nc, rot, 0)

    lane = _iota((RET_C, LANES), 1)
    rowi = _iota((RET_C, LANES), 0).astype(F32)
    absd = jnp.abs(_iota((RET_C, RET_C), 0) - _iota((RET_C, RET_C), 1)).astype(F32)
    for j in range(2):
        hmask = ((lane // 32) % 2) == j
        lg = lg_ref[pl.ds(2 * p + j, 1), :]
        lg128 = lg[:, :LANES]
        dmat = jnp.exp(absd * lg)
        xi_f = jnp.exp((rowi + 1.0) * lg128)
        ze_f = jnp.exp((RET_C - 1.0 - rowi) * lg128)
        xi_b = jnp.exp((RET_C - rowi) * lg128)
        ze_b = jnp.exp(rowi * lg128)
        cdec = jnp.exp(RET_C * lg128)
        vs = slice(j * LANES, (j + 1) * LANES)
        fst = jnp.zeros((LANES, LANES), F32)
        for i in range(nc):
            sl = slice(i * RET_C, (i + 1) * RET_C)
            qh = jnp.where(hmask, qr_s[sl, :], 0.0)
            kh = kr_s[sl, :]
            vh = v_ref[sl, vs]
            sc = _mm(qh, kh, _NT) * dmat
            y = _mm(sc, vh)
            if i > 0:
                y = y + _mm(qh * xi_f, fst)
            if i < nc - 1:
                fst = fst * cdec + _mm(jnp.where(hmask, kh, 0.0) * ze_f, vh, _TN)
            y_s[sl, vs] = y
        gst = jnp.zeros((LANES, LANES), F32)
        for i in range(nc - 1, -1, -1):
            sl = slice(i * RET_C, (i + 1) * RET_C)
            y = y_s[sl, vs]
            if i < nc - 1:
                qh = jnp.where(hmask, qr_s[sl, :], 0.0)
                y = y + _mm(qh * xi_b, gst)
            if i > 0:
                kh = jnp.where(hmask, kr_s[sl, :], 0.0)
                gst = gst * cdec + _mm(kh * ze_b, v_ref[sl, vs], _TN)
            mu = jnp.mean(y, axis=-1, keepdims=True)
            dv = y - mu
            var = jnp.mean(dv * dv, axis=-1, keepdims=True)
            out = dv * lax.rsqrt(var + NORM_EPS) * gain_ref[:, vs]
            o_ref[sl, vs] = (out * jax.nn.silu(gb_ref[sl, vs].astype(F32))).astype(BF16)


def _ret_call(proj3, cos_t, sin_t, lg_t, gain):
    bsz, seq, _ = proj3.shape
    w2 = 2 * LANES

    def spec(base, width):
        blk = base // width
        return pl.BlockSpec((None, seq, width), lambda b, p, blk=blk: (b, 0, blk + p))

    return pl.pallas_call(
        functools.partial(_ret_kernel, seq=seq),
        grid=(bsz, RET_HEADS // 2),
        in_specs=[spec(C_QB, LANES), spec(C_KB, LANES), spec(C_VB, w2), spec(C_GB, w2),
                  pl.BlockSpec((seq, LANES), lambda b, p: (0, 0)),
                  pl.BlockSpec((seq, LANES), lambda b, p: (0, 0)),
                  pl.BlockSpec((RET_HEADS, w2), lambda b, p: (0, 0)),
                  pl.BlockSpec((1, w2), lambda b, p: (0, p))],
        out_specs=pl.BlockSpec((None, seq, w2), lambda b, p: (b, 0, p)),
        out_shape=jax.ShapeDtypeStruct((bsz, seq, D), BF16),
        scratch_shapes=[pltpu.VMEM((seq, LANES), F32), pltpu.VMEM((seq, LANES), F32), pltpu.VMEM((seq, w2), F32)],
        compiler_params=pltpu.CompilerParams(dimension_semantics=("arbitrary", "arbitrary"),
                                             vmem_limit_bytes=VMEM_LIMIT),
        name="retention",
    )(proj3, proj3, proj3, proj3, cos_t, sin_t, lg_t, gain)


def _merge_kernel(x_ref, ya_ref, yb_ref, ma_ref, mb_ref, wa_ref, wb_ref, wo_ref, g_ref, *o_refs, last):
    za = jnp.dot(ya_ref[...], wa_ref[...], preferred_element_type=F32)
    zb = jnp.dot(yb_ref[...], wb_ref[...], preferred_element_type=F32)
    merged = jax.nn.sigmoid(ma_ref[...].astype(F32)) * za + jax.nn.sigmoid(mb_ref[...].astype(F32)) * zb
    x = x_ref[...] + jnp.dot(merged.astype(BF16), wo_ref[...], preferred_element_type=F32)
    if last:
        o_refs[0][...] = _rms(x, g_ref[...])
    else:
        o_refs[0][...] = x
        o_refs[1][...] = _rms(x, g_ref[...]).astype(BF16)


def _merge_call(x2, ya2, yb2, proj2, wa, wb, wo, gain, *, last):
    t = x2.shape[0]
    tm = min(512, t)
    row = pl.BlockSpec((tm, D), lambda i: (i, 0))
    wspec = pl.BlockSpec((D, D), lambda i: (0, 0))
    x_sds = jax.ShapeDtypeStruct((t, D), F32)
    return pl.pallas_call(
        functools.partial(_merge_kernel, last=last),
        grid=(t // tm,),
        in_specs=[row, row, row,
                  pl.BlockSpec((tm, D), lambda i: (i, C_MA // D)),
                  pl.BlockSpec((tm, D), lambda i: (i, C_MB // D)),
                  wspec, wspec, wspec, pl.BlockSpec((1, D), lambda i: (0, 0))],
        out_specs=row if last else (row, row),
        out_shape=x_sds if last else (x_sds, jax.ShapeDtypeStruct((t, D), BF16)),
        compiler_params=pltpu.CompilerParams(dimension_semantics=("arbitrary",),
                                             vmem_limit_bytes=VMEM_LIMIT),
        name="merge",
    )(x2, ya2, yb2, proj2, proj2, wa, wb, wo, gain)


def _norm_kernel(x_ref, g_ref, o_ref):
    o_ref[...] = _rms(x_ref[...], g_ref[...]).astype(BF16)


def _norm_call(x2, gain):
    t = x2.shape[0]
    tm = min(1024, t)
    return pl.pallas_call(
        _norm_kernel,
        grid=(t // tm,),
        in_specs=[pl.BlockSpec((tm, D), lambda i: (i, 0)), pl.BlockSpec((1, D), lambda i: (0, 0))],
        out_specs=pl.BlockSpec((tm, D), lambda i: (i, 0)),
        out_shape=jax.ShapeDtypeStruct((t, D), BF16),
        compiler_params=pltpu.CompilerParams(dimension_semantics=("arbitrary",)),
        name="first_norm",
    )(x2, gain)


def _qk_perm():
    idx = []
    for pair in range(RET_HEADS // 2):
        for lane in range(LANES):
            seg, i = lane // 32, lane % 32
            head, half = 2 * pair + seg % 2, seg // 2
            idx.append(head * HEAD + half * 32 + i)
    return np.asarray(idx, np.int32)


def _pack_w_in(w_in, w_vres_down):
    depth = w_in.shape[0]
    perm = _qk_perm()
    vres = jnp.concatenate([jnp.zeros((1, D, 32), F32), w_vres_down], axis=0)
    vres = jnp.pad(vres, ((0, 0), (0, 0), (0, LANES - 32)))
    w_wkv = jnp.concatenate([w_in[:, :, 0:3072], w_in[:, :, 3328:4352], w_in[:, :, 3072:3328], vres], axis=2)
    w_rest = jnp.concatenate([w_in[:, :, 5376:6400], w_in[:, :, 6400:7424], w_in[:, :, 7424:8448],
                              w_in[:, :, 8448:9472], w_in[:, :, 4352:4864][:, :, perm],
                              w_in[:, :, 4864:5376][:, :, perm]], axis=2)
    assert w_wkv.shape == (depth, D, N_WKV) and w_rest.shape == (depth, D, N_REST)
    return w_wkv.astype(BF16), w_rest.astype(BF16)


def _pair_cat(wf, wb):
    depth, pairs = wf.shape[0], D // LANES
    cat = jnp.concatenate([wf.reshape(depth, LANES, pairs, LANES), wb.reshape(depth, LANES, pairs, LANES)], axis=3)
    return cat.transpose(0, 2, 1, 3)


def _wkv_params(shift_prev, shift_next, w_decay_up, decay_bias, w_iclr_up, iclr_bias, w_vres_up, vres_bias,
                k_k, k_a, r_k, lnx_gain, lnx_bias):
    depth = k_k.shape[0]
    vb = jnp.concatenate([jnp.zeros((1, D), F32), vres_bias], axis=0)
    vecs = jnp.stack([shift_prev[:, 0:1024], shift_prev[:, 1024:2048], shift_prev[:, 2048:3072],
                      shift_next[:, 0:1024], shift_next[:, 1024:2048], shift_next[:, 2048:3072],
                      0.5 * decay_bias[:, 0], 0.5 * decay_bias[:, 1], 0.5 * iclr_bias[:, 0], 0.5 * iclr_bias[:, 1],
                      k_k, k_a, r_k.reshape(depth, D), lnx_gain, lnx_bias, vb], axis=1)
    mul = jnp.concatenate([jnp.stack([shift_prev[:, 3072:3200], shift_prev[:, 3200:3328],
                                      shift_next[:, 3072:3200], shift_next[:, 3200:3328]], axis=1),
                           jnp.zeros((depth, 4, LANES), F32)], axis=1)
    zpad = jnp.zeros((depth, 64, D), F32)
    zmat = jnp.zeros((depth, LANES, D), F32)
    wvr = jnp.concatenate([jnp.zeros((1, LANES, D), F32),
                           jnp.pad(w_vres_up, ((0, 0), (0, LANES - 32), (0, 0)))], axis=0)
    wl = jnp.stack([0.5 * _pair_cat(jnp.concatenate([w_decay_up[:, 0], zpad], axis=1),
                                    jnp.concatenate([zpad, w_decay_up[:, 1]], axis=1)),
                    0.5 * _pair_cat(jnp.concatenate([w_iclr_up[:, 0], zpad], axis=1),
                                    jnp.concatenate([zpad, w_iclr_up[:, 1]], axis=1)),
                    _pair_cat(wvr, zmat)], axis=2)
    return vecs, mul, wl


def _rope_tables(seq):
    half = 32
    freqs = jnp.power(ROPE_BASE, -jnp.arange(half, dtype=F32) / half)
    ang = jnp.arange(seq, dtype=F32)[:, None] * freqs[None, :]
    cos = jnp.tile(jnp.cos(ang), (1, 4))
    sin = jnp.tile(jnp.sin(ang), (1, 4))
    sign = jnp.where(jnp.arange(LANES) < 64, -1.0, 1.0).astype(F32)
    return cos, sin * sign[None, :]


def kernel(x, norm_gain, w_in, w_vres_down, shift_prev, shift_next, w_decay_up, decay_bias, w_iclr_up, iclr_bias,
           w_vres_up, vres_bias, k_k, k_a, r_k, lnx_gain, lnx_bias, w_branch_a, ret_norm_gain, w_branch_b, w_out,
           final_gain):
    bsz, seq, _ = x.shape
    depth = w_in.shape[0]
    assert seq % RET_C == 0 and seq % SLAB == 0 and SLAB % CH == 0
    t = bsz * seq

    w_wkv, w_rest = _pack_w_in(w_in, w_vres_down)
    wa_all, wb_all, wo_all = (w.astype(BF16) for w in (w_branch_a, w_branch_b, w_out))
    cos_t, sin_t = _rope_tables(seq)
    log_g = jnp.log1p(-jnp.exp2(-5.0 - jnp.arange(RET_HEADS, dtype=F32)))
    lg_t = jnp.broadcast_to(log_g[:, None], (RET_HEADS, 2 * LANES))
    vecs_all, mul_all, wl_all = _wkv_params(shift_prev, shift_next, w_decay_up, decay_bias, w_iclr_up, iclr_bias,
                                            w_vres_up, vres_bias, k_k, k_a, r_k, lnx_gain, lnx_bias)

    x2 = x.reshape(t, D)
    xn2 = _norm_call(x2, norm_gain[0][None, :])
    vfirst = None
    for l in range(depth):
        proj3 = _proj_call(xn2, w_wkv[l], F32).reshape(bsz, seq, N_WKV)
        rest2 = _proj_call(xn2, w_rest[l], BF16)
        rest3 = rest2.reshape(bsz, seq, N_REST)

        vecs, mul, wl = vecs_all[l], mul_all[l], wl_all[l]

        if l == 0:
            ya, vfirst = _wkv_call(proj3, None, vecs, mul, wl, first=True)
        else:
            ya = _wkv_call(proj3, vfirst, vecs, mul, wl, first=False)
        yb = _ret_call(rest3, cos_t, sin_t, lg_t, ret_norm_gain[l][None, :])
        last = l == depth - 1
        gain = final_gain if last else norm_gain[l + 1]
        res = _merge_call(x2, ya.reshape(t, D), yb.reshape(t, D), rest2, wa_all[l], wb_all[l], wo_all[l],
                          gain[None, :], last=last)
        if not last:
            x2, xn2 = res

    return res.reshape(bsz, seq, D)
```

```python
import functools

import numpy as np
import jax
import jax.numpy as jnp
from jax import lax
from jax.experimental import pallas as pl
from jax.experimental.pallas import tpu as pltpu

F32 = jnp.float32
BF16 = jnp.bfloat16

D = 1024
HEAD = 64
LANES = 128
CH = 64
SLAB = 512
RET_HEADS = 8
RET_C = 256
NORM_EPS = 1e-6
GN_EPS = 64e-5
EXP_M_HALF = 0.6065306597126334
ROPE_BASE = 10000.0
V7X_VMEM_BYTES = 64 * 1024 * 1024
VMEM_LIMIT = V7X_VMEM_BYTES * 7 // 8

C_R, C_K, C_V, C_GA, C_DEC, C_ICL, C_VRES, N_WKV = 0, 1024, 2048, 3072, 4096, 4224, 4352, 4480
C_VB, C_GB, C_MA, C_MB, C_QB, C_KB, N_REST = 0, 1024, 2048, 3072, 4096, 4608, 5120

_NN = (((1,), (0,)), ((), ()))
_NT = (((1,), (1,)), ((), ()))
_TN = (((0,), (0,)), ((), ()))


def _dg(a, b, dims):
    return lax.dot_general(a, b, dims, preferred_element_type=F32)


def _mm(a, b, dims=_NN):
    return _dg(a.astype(BF16), b.astype(BF16), dims)


def _headsum(xs, ones_bd):
    return [_dg(x.astype(BF16), ones_bd, _NN) for x in xs]


def _iota(shape, axis):
    return lax.broadcasted_iota(jnp.int32, shape, axis)


def _pair_bd(x, bdmask):
    return jnp.where(bdmask, jnp.concatenate([x, x], axis=0), 0.0)


def _wkv_consts():
    row = _iota((CH, LANES), 0)
    col = _iota((CH, LANES), 1) % CH
    r128 = _iota((LANES, LANES), 0)
    c128 = _iota((LANES, LANES), 1)

    def masks(lower):
        late, early = (row, col) if lower else (col, row)
        return dict(
            strict=early < late, incl=early <= late,
            m_d=(early < late) & ((row // 16) == (col // 16)),
            m_o1=((row // 32) == (col // 32)) & ((late // 16) % 2 == 1) & ((early // 16) % 2 == 0),
            m_o2=((late // 32) == 1) & ((early // 32) == 0))

    return dict(bd=(r128 // CH) == (c128 // CH), eye128=r128 == c128,
                eye_cat=jnp.where(row == col, 1.0, 0.0).astype(F32),
                fwd=masks(True), bwd=masks(False))


def _wkv_mxu_steps(items, cst):
    bdmask = cst["bd"]

    def bd(x):
        return _pair_bd(x, bdmask)

    mm = _mm

    st = [dict(it=it, mk=cst["bwd"] if it["rev"] else cst["fwd"]) for it in items]
    for s in st:
        ld = s["it"]["load"]
        amat = mm(jnp.concatenate([ld("at"), ld("rt")], axis=0),
                  jnp.concatenate([bd(ld("bt")), bd(ld("kt"))], axis=0), _NT)
        mk = s["mk"]
        s["n"] = amat[:CH, :LANES]
        s["akrk"] = jnp.concatenate([jnp.where(mk["strict"], amat[:CH, LANES:], 0.0),
                                     jnp.where(mk["incl"], amat[CH:, LANES:], 0.0)], axis=0)
        s["rb"] = jnp.where(mk["incl"], amat[CH:, :LANES], 0.0)
    yield
    for s in st:
        xv = mm(s["akrk"], bd(s["it"]["load"]("v")))
        s["x1"], s["y0v"] = xv[:CH], xv[CH:]
    yield
    for s in st:
        s["p"] = jnp.where(s["mk"]["m_d"], s["n"], 0.0)
        s["t"] = cst["eye_cat"] + s["p"]
        s["q"] = mm(s["p"], bd(s["p"]))
    yield
    for _ in range(2):
        for s in st:
            qt = mm(jnp.concatenate([s["q"], s["t"]], axis=0), bd(s["q"]))
            s["t"] = s["t"] + qt[CH:]
            s["q"] = qt[:CH]
        yield
    for s in st:
        s["t"] = s["t"] + mm(s["t"], bd(s["q"]))
    yield
    for key in ("m_o1", "m_o2"):
        for s in st:
            s["x"] = mm(s["t"], bd(jnp.where(s["mk"][key], s["n"], 0.0)))
        yield
        for s in st:
            s["t"] = s["t"] + mm(s["x"], bd(s["t"]))
        yield
    for s in st:
        au = mm(s["t"], jnp.concatenate([bd(s["it"]["load"]("at")), bd(s["x1"])], axis=1))
        s["ah"], s["u0"] = au[:, :LANES], au[:, LANES:]
    yield
    for s in st:
        ld = s["it"]["load"]
        w_tot = s["it"]["w_tot"]()
        ry = mm(s["rb"], jnp.concatenate([bd(s["ah"]), bd(s["u0"])], axis=1))
        rhs2 = jnp.concatenate([jnp.concatenate([s["ah"], s["u0"]], axis=1),
                                jnp.concatenate([jnp.zeros((CH, LANES), F32), ld("v")], axis=1)], axis=0)
        kb_end = jnp.concatenate([ld("bt") * w_tot, ld("kt") * w_tot], axis=0)
        mg = mm(kb_end, rhs2, _TN)
        m = jnp.where(bdmask, mg[:, :LANES], 0.0) + jnp.where(cst["eye128"], w_tot, 0.0)
        g = jnp.where(bdmask, mg[:, LANES:], 0.0)
        s["it"]["store"](ld("rt") + ry[:, :LANES], ry[:, LANES:] + s["y0v"], m, g)
    yield


def _run(gen):
    for _ in gen:
        pass


def _interleave(main, side, ratio):
    for _ in main:
        for _ in range(ratio):
            next(side, None)
    _run(side)


def _chain(gens):
    for g in gens:
        yield from g


def _sigmoid(x):
    return 0.5 * jnp.tanh(0.5 * x) + 0.5


def _shifted(ref, start, n, mp, mn, seq):
    sub = 8
    edge = _iota((sub, LANES), 0)
    cur = ref[start:start + n, :]
    if start > 0:
        prev = ref[start - 1:start - 1 + n, :]
    else:
        head = jnp.where(edge == 0, 0.0, pltpu.roll(ref[0:sub, :], 1, 0))
        prev = jnp.concatenate([head, ref[sub - 1:n - 1, :]], axis=0)
    if start + n < seq:
        nxt = ref[start + 1:start + 1 + n, :]
    else:
        tail = jnp.where(edge == sub - 1, 0.0, pltpu.roll(ref[seq - sub:seq, :], sub - 1, 0))
        nxt = jnp.concatenate([ref[start + 1:seq - sub + 1, :], tail], axis=0)
    return cur * (1.0 - mp - mn) + mp * prev + mn * nxt


def _chunk_cumsum(x, rev):
    n = x.shape[0]
    sub = 8
    pos = _iota((n, LANES), 0) % CH
    s = 1
    while s < sub:
        if rev:
            x = x + jnp.where(pos < CH - s, pltpu.roll(x, n - s, 0), 0.0)
        else:
            x = x + jnp.where(pos >= s, pltpu.roll(x, s, 0), 0.0)
        s *= 2
    g = CH // sub
    x = x.reshape(n // CH, g, sub, LANES)
    s = 1
    while s < g:
        z = jnp.zeros((n // CH, s, sub, LANES), F32)
        if rev:
            x = x + jnp.concatenate([x[:, s:], z], axis=1)
        else:
            x = x + jnp.concatenate([z, x[:, :-s]], axis=1)
        s *= 2
    return x.reshape(n, LANES)


(V_MPR, V_MPK, V_MPV, V_MNR, V_MNK, V_MNV, V_DBF, V_DBB, V_IBF, V_IBB,
 V_KK, V_KA, V_RK, V_LG, V_LB, V_VB) = range(16)
W_DEC, W_ICL, W_VR = range(3)
I_RT, I_AT, I_KT, I_BT = range(4)
I_NAMES = dict(rt=I_RT, at=I_AT, kt=I_KT, bt=I_BT)


def _wkv_kernel(*refs, seq, first):
    if first:
        (r_ref, k_ref, v_ref, ga_ref, dec_ref, icl_ref, vec_ref, mul_ref, wl_ref,
         ya_ref, vf_out_ref, lhs_s, add_s, yf_s, yb_s, bon_s, it_s, v_s, wt_s) = refs
        vres_ref = vf_ref = None
    else:
        (r_ref, k_ref, v_ref, ga_ref, dec_ref, icl_ref, vres_ref, vf_ref, vec_ref, mul_ref, wl_ref,
         ya_ref, lhs_s, add_s, yf_s, yb_s, bon_s, it_s, v_s, wt_s) = refs
        vf_out_ref = None
    ns = seq // SLAB
    per = SLAB // CH
    cst = _wkv_consts()
    ones_bd = jnp.where(cst["bd"], 1.0, 0.0).astype(BF16)

    def vec(i):
        return vec_ref[i:i + 1, :]

    def prologue(sb):
        start = sb * SLAB
        rows = slice(start, start + SLAB)
        r = _shifted(r_ref, start, SLAB, vec(V_MPR), vec(V_MNR), seq)
        yield
        k = _shifted(k_ref, start, SLAB, vec(V_MPK), vec(V_MNK), seq)
        yield
        v = _shifted(v_ref, start, SLAB, vec(V_MPV), vec(V_MNV), seq)
        yield
        dec = _shifted(dec_ref, start, SLAB, mul_ref[0:1, :], mul_ref[2:3, :], seq)
        icl = _shifted(icl_ref, start, SLAB, mul_ref[1:2, :], mul_ref[3:4, :], seq)
        yield
        if first:
            vf_out_ref[rows, :] = v
        else:
            mix = _sigmoid(vec(V_VB) + _mm(vres_ref[rows, :], wl_ref[W_VR])[:, :LANES])
            v = v + (vf_ref[rows, :] - v) * mix
        v_s[rows, :] = v
        yield
        zd = _mm(jnp.tanh(dec), wl_ref[W_DEC])
        zi = _mm(icl, wl_ref[W_ICL])
        kk = k * vec(V_KK)
        yield
        kds, ags, lws = [], [], []
        kh = k * (0.5 * vec(V_KA))
        kb2 = k - kh
        for d in range(2):
            th_w = jnp.tanh(vec(V_DBF + d) + zd[:, d * LANES:(d + 1) * LANES])
            lws.append((-0.5 * EXP_M_HALF) * th_w - 0.5 * EXP_M_HALF)
            th_a = jnp.tanh(vec(V_IBF + d) + zi[:, d * LANES:(d + 1) * LANES])
            ags.append(0.5 * th_a + 0.5)
            kds.append(kb2 + kh * th_a)
            yield
        ssq, bsum = _headsum([kk * kk, r * (kds[0] + kds[1]) * vec(V_RK)], ones_bd)
        kk = kk * lax.rsqrt(ssq + 1e-12)
        bon_s[rows, :] = bsum * v
        yield
        for d in range(2):
            cum = _chunk_cumsum(lws[d], d == 1)
            yield
            e_cum = jnp.exp(cum)
            e_inv = 1.0 / e_cum
            it_s[d, I_RT, rows, :] = r * e_cum
            it_s[d, I_AT, rows, :] = -kk * jnp.exp(cum - lws[d])
            yield
            it_s[d, I_KT, rows, :] = kds[d] * e_inv
            it_s[d, I_BT, rows, :] = (kk * ags[d]) * e_inv
            for i in range(per):
                c = sb * per + i
                row = i * CH if d == 1 else i * CH + CH - 1
                wt_s[d, c * 8:(c + 1) * 8, :] = jnp.broadcast_to(jnp.exp(cum[row:row + 1, :]), (8, LANES))
            yield

    def chunk_items(step):
        buf = step % 2
        items = []
        for d in range(2):
            slab = step if d == 0 else ns - 1 - step
            for i in range(per):
                c = slab * per + i
                crow = slice(c * CH, (c + 1) * CH)

                def load(name, d=d, crow=crow):
                    if name == "v":
                        return v_s[crow, :]
                    return it_s[d, I_NAMES[name], crow, :]

                def w_tot(d=d, c=c):
                    return wt_s[d, c * 8:c * 8 + 1, :]

                def store(rh, y0, m, g, d=d, i=i):
                    lhs_s[buf, d, i, :CH, :] = rh
                    lhs_s[buf, d, i, CH:, :] = m
                    add_s[buf, d, i, :CH, :] = y0
                    add_s[buf, d, i, CH:, :] = g

                items.append(dict(rev=d == 1, load=load, w_tot=w_tot, store=store))
        return items

    state = [jnp.zeros((LANES, LANES), F32), jnp.zeros((LANES, LANES), F32)]

    def scan(step):
        buf = step % 2
        for j in range(per):
            for d in range(2):
                i = j if d == 0 else per - 1 - j
                slab = step if d == 0 else ns - 1 - step
                o = _mm(lhs_s[buf, d, i], state[d]) + add_s[buf, d, i]
                rows = slice((slab * per + i) * CH, (slab * per + i + 1) * CH)
                (yf_s if d == 0 else yb_s)[rows, :] = o[:CH]
                state[d] = o[CH:]
            yield

    def post(start, n):
        rows = slice(start, start + n)
        y = yf_s[rows, :] + yb_s[rows, :]
        mu = _headsum([y], ones_bd)[0] * (1.0 / HEAD)
        yield
        dv = y - mu
        var = _headsum([dv * dv], ones_bd)[0] * (1.0 / HEAD)
        yield
        out = dv * lax.rsqrt(var + GN_EPS) * vec(V_LG) + vec(V_LB) + bon_s[rows, :]
        ga = ga_ref[rows, :]
        ya_ref[rows, :] = (out * (ga * _sigmoid(ga))).astype(BF16)
        yield

    ends = sorted({0, ns - 1})
    inner = [sb for sb in range(ns) if sb not in ends]
    for sb in ends:
        _run(prologue(sb))
    for step in range(ns):
        side = _chain([prologue(sb) for sb in inner]) if step == 0 else scan(step - 1)
        _interleave(_wkv_mxu_steps(chunk_items(step), cst), side, 3 if step == 0 else 1)
    _interleave(scan(ns - 1), _chain([post(sb * SLAB, SLAB) for sb in inner]), 1)
    for sb in ends:
        _run(post(sb * SLAB, SLAB))


def _wkv_call(proj3, vfirst, vecs, mul, wl, *, first):
    bsz, seq, _ = proj3.shape
    nc = seq // CH
    per = SLAB // CH

    def col(base):
        blk = base // LANES
        return pl.BlockSpec((None, seq, LANES), lambda b, p, blk=blk: (b, 0, blk + p))

    def fixed(base):
        blk = base // LANES
        return pl.BlockSpec((None, seq, LANES), lambda b, p, blk=blk: (b, 0, blk))

    in_specs = [col(C_R), col(C_K), col(C_V), col(C_GA), fixed(C_DEC), fixed(C_ICL)]
    args = [proj3] * 6
    if not first:
        in_specs += [fixed(C_VRES), pl.BlockSpec((None, seq, LANES), lambda b, p: (b, 0, p))]
        args += [proj3, vfirst]
    in_specs += [pl.BlockSpec((16, LANES), lambda b, p: (0, p)),
                 pl.BlockSpec((8, LANES), lambda b, p: (0, 0)),
                 pl.BlockSpec((None, 3, LANES, 2 * LANES), lambda b, p: (p, 0, 0, 0))]
    args += [vecs, mul, wl]
    out_block = pl.BlockSpec((None, seq, LANES), lambda b, p: (b, 0, p))
    out_sds = jax.ShapeDtypeStruct((bsz, seq, D), BF16)
    out_shape = (out_sds, jax.ShapeDtypeStruct((bsz, seq, D), F32)) if first else out_sds
    out_specs = (out_block, out_block) if first else out_block
    scratch = [pltpu.VMEM((2, 2, per, CH + LANES, LANES), F32),
               pltpu.VMEM((2, 2, per, CH + LANES, LANES), F32),
               pltpu.VMEM((seq, LANES), F32), pltpu.VMEM((seq, LANES), F32), pltpu.VMEM((seq, LANES), F32),
               pltpu.VMEM((2, 4, seq, LANES), F32),
               pltpu.VMEM((seq, LANES), F32),
               pltpu.VMEM((2, nc * 8, LANES), F32)]
    return pl.pallas_call(
        functools.partial(_wkv_kernel, seq=seq, first=first),
        grid=(bsz, D // LANES),
        in_specs=in_specs, out_specs=out_specs, out_shape=out_shape,
        scratch_shapes=scratch,
        compiler_params=pltpu.CompilerParams(dimension_semantics=("arbitrary", "arbitrary"),
                                             vmem_limit_bytes=VMEM_LIMIT),
        name="wkv7",
    )(*args)


def _rms(x, g):
    ms = jnp.mean(x * x, axis=-1, keepdims=True)
    return x * lax.rsqrt(ms + NORM_EPS) * g


def _proj_kernel(xn_ref, w_ref, o_ref):
    o_ref[...] = jnp.dot(xn_ref[...], w_ref[...], preferred_element_type=F32).astype(o_ref.dtype)


PROJ_BLOCK_BUDGET = 40 * 1024 * 1024


def _proj_tiles(t, n, out_bytes):
    best = None
    for tm in (1024, 512, 256):
        if t % tm:
            continue
        for tn in range(LANES, n + 1, LANES):
            if n % tn == 0 and 2 * (tm * D * 2 + D * tn * 2 + tm * tn * out_bytes) <= PROJ_BLOCK_BUDGET:
                if best is None or tm * tn > best[0] * best[1]:
                    best = (tm, tn)
    return best


def _proj_call(xn2, w, out_dtype):
    t = xn2.shape[0]
    n = w.shape[1]
    tm, tn = _proj_tiles(t, n, jnp.dtype(out_dtype).itemsize)
    return pl.pallas_call(
        _proj_kernel,
        grid=(n // tn, t // tm),
        in_specs=[pl.BlockSpec((tm, D), lambda j, i: (i, 0)),
                  pl.BlockSpec((D, tn), lambda j, i: (0, j))],
        out_specs=pl.BlockSpec((tm, tn), lambda j, i: (i, j)),
        out_shape=jax.ShapeDtypeStruct((t, n), out_dtype),
        compiler_params=pltpu.CompilerParams(dimension_semantics=("arbitrary", "arbitrary"),
                                             vmem_limit_bytes=VMEM_LIMIT),
        name="proj",
    )(xn2, w)


def _ret_kernel(q_ref, k_ref, v_ref, gb_ref, cos_ref, sin_ref, lg_ref, gain_ref, o_ref, qr_s, kr_s, y_s, *, seq):
    nc = seq // RET_C
    p = pl.program_id(1)
    scale = HEAD ** -0.5

    def rot(c, _):
        sl = pl.ds(pl.multiple_of(c * RET_C, RET_C), RET_C)
        cs, sn = cos_ref[sl, :], sin_ref[sl, :]
        q = q_ref[sl, :].astype(F32)
        k = k_ref[sl, :].astype(F32)
        qr_s[sl, :] = (q * cs + pltpu.roll(q, 64, 1) * sn) * scale
        kr_s[sl, :] = k * cs + pltpu.roll(k, 64, 1) * sn
        return 0

    lax.fori_loop(0, nc, rot, 0)

    lane = _iota((RET_C, LANES), 1)
    rowi = _iota((RET_C, LANES), 0).astype(F32)
    absd = jnp.abs(_iota((RET_C, RET_C), 0) - _iota((RET_C, RET_C), 1)).astype(F32)
    for j in range(2):
        hmask = ((lane // 32) % 2) == j
        lg = lg_ref[pl.ds(2 * p + j, 1), :]
        lg128 = lg[:, :LANES]
        dmat = jnp.exp(absd * lg)
        xi_f = jnp.exp((rowi + 1.0) * lg128)
        ze_f = jnp.exp((RET_C - 1.0 - rowi) * lg128)
        xi_b = jnp.exp((RET_C - rowi) * lg128)
        ze_b = jnp.exp(rowi * lg128)
        cdec = jnp.exp(RET_C * lg128)
        vs = slice(j * LANES, (j + 1) * LANES)
        fst = jnp.zeros((LANES, LANES), F32)
        for i in range(nc):
            sl = slice(i * RET_C, (i + 1) * RET_C)
            qh = jnp.where(hmask, qr_s[sl, :], 0.0)
            kh = kr_s[sl, :]
            vh = v_ref[sl, vs]
            sc = _mm(qh, kh, _NT) * dmat
            y = _mm(sc, vh)
            if i > 0:
                y = y + _mm(qh * xi_f, fst)
            if i < nc - 1:
                fst = fst * cdec + _mm(jnp.where(hmask, kh, 0.0) * ze_f, vh, _TN)
            y_s[sl, vs] = y
        gst = jnp.zeros((LANES, LANES), F32)
        for i in range(nc - 1, -1, -1):
            sl = slice(i * RET_C, (i + 1) * RET_C)
            y = y_s[sl, vs]
            if i < nc - 1:
                qh = jnp.where(hmask, qr_s[sl, :], 0.0)
                y = y + _mm(qh * xi_b, gst)
            if i > 0:
                kh = jnp.where(hmask, kr_s[sl, :], 0.0)
                gst = gst * cdec + _mm(kh * ze_b, v_ref[sl, vs], _TN)
            mu = jnp.mean(y, axis=-1, keepdims=True)
            dv = y - mu
            var = jnp.mean(dv * dv, axis=-1, keepdims=True)
            out = dv * lax.rsqrt(var + NORM_EPS) * gain_ref[:, vs]
            o_ref[sl, vs] = (out * jax.nn.silu(gb_ref[sl, vs].astype(F32))).astype(BF16)


def _ret_call(proj3, cos_t, sin_t, lg_t, gain):
    bsz, seq, _ = proj3.shape
    w2 = 2 * LANES

    def spec(base, width):
        blk = base // width
        return pl.BlockSpec((None, seq, width), lambda b, p, blk=blk: (b, 0, blk + p))

    return pl.pallas_call(
        functools.partial(_ret_kernel, seq=seq),
        grid=(bsz, RET_HEADS // 2),
        in_specs=[spec(C_QB, LANES), spec(C_KB, LANES), spec(C_VB, w2), spec(C_GB, w2),
                  pl.BlockSpec((seq, LANES), lambda b, p: (0, 0)),
                  pl.BlockSpec((seq, LANES), lambda b, p: (0, 0)),
                  pl.BlockSpec((RET_HEADS, w2), lambda b, p: (0, 0)),
                  pl.BlockSpec((1, w2), lambda b, p: (0, p))],
        out_specs=pl.BlockSpec((None, seq, w2), lambda b, p: (b, 0, p)),
        out_shape=jax.ShapeDtypeStruct((bsz, seq, D), BF16),
        scratch_shapes=[pltpu.VMEM((seq, LANES), F32), pltpu.VMEM((seq, LANES), F32), pltpu.VMEM((seq, w2), F32)],
        compiler_params=pltpu.CompilerParams(dimension_semantics=("arbitrary", "arbitrary"),
                                             vmem_limit_bytes=VMEM_LIMIT),
        name="retention",
    )(proj3, proj3, proj3, proj3, cos_t, sin_t, lg_t, gain)


def _merge_kernel(x_ref, ya_ref, yb_ref, ma_ref, mb_ref, wa_ref, wb_ref, wo_ref, g_ref, *o_refs, last):
    za = jnp.dot(ya_ref[...], wa_ref[...], preferred_element_type=F32)
    zb = jnp.dot(yb_ref[...], wb_ref[...], preferred_element_type=F32)
    merged = jax.nn.sigmoid(ma_ref[...].astype(F32)) * za + jax.nn.sigmoid(mb_ref[...].astype(F32)) * zb
    x = x_ref[...] + jnp.dot(merged.astype(BF16), wo_ref[...], preferred_element_type=F32)
    if last:
        o_refs[0][...] = _rms(x, g_ref[...])
    else:
        o_refs[0][...] = x
        o_refs[1][...] = _rms(x, g_ref[...]).astype(BF16)


def _merge_call(x2, ya2, yb2, proj2, wa, wb, wo, gain, *, last):
    t = x2.shape[0]
    tm = min(512, t)
    row = pl.BlockSpec((tm, D), lambda i: (i, 0))
    wspec = pl.BlockSpec((D, D), lambda i: (0, 0))
    x_sds = jax.ShapeDtypeStruct((t, D), F32)
    return pl.pallas_call(
        functools.partial(_merge_kernel, last=last),
        grid=(t // tm,),
        in_specs=[row, row, row,
                  pl.BlockSpec((tm, D), lambda i: (i, C_MA // D)),
                  pl.BlockSpec((tm, D), lambda i: (i, C_MB // D)),
                  wspec, wspec, wspec, pl.BlockSpec((1, D), lambda i: (0, 0))],
        out_specs=row if last else (row, row),
        out_shape=x_sds if last else (x_sds, jax.ShapeDtypeStruct((t, D), BF16)),
        compiler_params=pltpu.CompilerParams(dimension_semantics=("arbitrary",),
                                             vmem_limit_bytes=VMEM_LIMIT),
        name="merge",
    )(x2, ya2, yb2, proj2, proj2, wa, wb, wo, gain)


def _norm_kernel(x_ref, g_ref, o_ref):
    o_ref[...] = _rms(x_ref[...], g_ref[...]).astype(BF16)


def _norm_call(x2, gain):
    t = x2.shape[0]
    tm = min(1024, t)
    return pl.pallas_call(
        _norm_kernel,
        grid=(t // tm,),
        in_specs=[pl.BlockSpec((tm, D), lambda i: (i, 0)), pl.BlockSpec((1, D), lambda i: (0, 0))],
        out_specs=pl.BlockSpec((tm, D), lambda i: (i, 0)),
        out_shape=jax.ShapeDtypeStruct((t, D), BF16),
        compiler_params=pltpu.CompilerParams(dimension_semantics=("arbitrary",)),
        name="first_norm",
    )(x2, gain)


def _qk_perm():
    idx = []
    for pair in range(RET_HEADS // 2):
        for lane in range(LANES):
            seg, i = lane // 32, lane % 32
            head, half = 2 * pair + seg % 2, seg // 2
            idx.append(head * HEAD + half * 32 + i)
    return np.asarray(idx, np.int32)


def _pack_w_in(w_in, w_vres_down):
    depth = w_in.shape[0]
    perm = _qk_perm()
    vres = jnp.concatenate([jnp.zeros((1, D, 32), F32), w_vres_down], axis=0)
    vres = jnp.pad(vres, ((0, 0), (0, 0), (0, LANES - 32)))
    w_wkv = jnp.concatenate([w_in[:, :, 0:3072], w_in[:, :, 3328:4352], w_in[:, :, 3072:3328], vres], axis=2)
    w_rest = jnp.concatenate([w_in[:, :, 5376:6400], w_in[:, :, 6400:7424], w_in[:, :, 7424:8448],
                              w_in[:, :, 8448:9472], w_in[:, :, 4352:4864][:, :, perm],
                              w_in[:, :, 4864:5376][:, :, perm]], axis=2)
    assert w_wkv.shape == (depth, D, N_WKV) and w_rest.shape == (depth, D, N_REST)
    return w_wkv.astype(BF16), w_rest.astype(BF16)


def _pair_cat(wf, wb):
    depth, pairs = wf.shape[0], D // LANES
    cat = jnp.concatenate([wf.reshape(depth, LANES, pairs, LANES), wb.reshape(depth, LANES, pairs, LANES)], axis=3)
    return cat.transpose(0, 2, 1, 3)


def _wkv_params(shift_prev, shift_next, w_decay_up, decay_bias, w_iclr_up, iclr_bias, w_vres_up, vres_bias,
                k_k, k_a, r_k, lnx_gain, lnx_bias):
    depth = k_k.shape[0]
    vb = jnp.concatenate([jnp.zeros((1, D), F32), vres_bias], axis=0)
    vecs = jnp.stack([shift_prev[:, 0:1024], shift_prev[:, 1024:2048], shift_prev[:, 2048:3072],
                      shift_next[:, 0:1024], shift_next[:, 1024:2048], shift_next[:, 2048:3072],
                      0.5 * decay_bias[:, 0], 0.5 * decay_bias[:, 1], 0.5 * iclr_bias[:, 0], 0.5 * iclr_bias[:, 1],
                      k_k, k_a, r_k.reshape(depth, D), lnx_gain, lnx_bias, vb], axis=1)
    mul = jnp.concatenate([jnp.stack([shift_prev[:, 3072:3200], shift_prev[:, 3200:3328],
                                      shift_next[:, 3072:3200], shift_next[:, 3200:3328]], axis=1),
                           jnp.zeros((depth, 4, LANES), F32)], axis=1)
    zpad = jnp.zeros((depth, 64, D), F32)
    zmat = jnp.zeros((depth, LANES, D), F32)
    wvr = jnp.concatenate([jnp.zeros((1, LANES, D), F32),
                           jnp.pad(w_vres_up, ((0, 0), (0, LANES - 32), (0, 0)))], axis=0)
    wl = jnp.stack([0.5 * _pair_cat(jnp.concatenate([w_decay_up[:, 0], zpad], axis=1),
                                    jnp.concatenate([zpad, w_decay_up[:, 1]], axis=1)),
                    0.5 * _pair_cat(jnp.concatenate([w_iclr_up[:, 0], zpad], axis=1),
                                    jnp.concatenate([zpad, w_iclr_up[:, 1]], axis=1)),
                    _pair_cat(wvr, zmat)], axis=2)
    return vecs, mul, wl


def _rope_tables(seq):
    half = 32
    freqs = jnp.power(ROPE_BASE, -jnp.arange(half, dtype=F32) / half)
    ang = jnp.arange(seq, dtype=F32)[:, None] * freqs[None, :]
    cos = jnp.tile(jnp.cos(ang), (1, 4))
    sin = jnp.tile(jnp.sin(ang), (1, 4))
    sign = jnp.where(jnp.arange(LANES) < 64, -1.0, 1.0).astype(F32)
    return cos, sin * sign[None, :]


def kernel(x, norm_gain, w_in, w_vres_down, shift_prev, shift_next, w_decay_up, decay_bias, w_iclr_up, iclr_bias,
           w_vres_up, vres_bias, k_k, k_a, r_k, lnx_gain, lnx_bias, w_branch_a, ret_norm_gain, w_branch_b, w_out,
           final_gain):
    bsz, seq, _ = x.shape
    depth = w_in.shape[0]
    assert seq % RET_C == 0 and seq % SLAB == 0 and SLAB % CH == 0
    t = bsz * seq

    w_wkv, w_rest = _pack_w_in(w_in, w_vres_down)
    wa_all, wb_all, wo_all = (w.astype(BF16) for w in (w_branch_a, w_branch_b, w_out))
    cos_t, sin_t = _rope_tables(seq)
    log_g = jnp.log1p(-jnp.exp2(-5.0 - jnp.arange(RET_HEADS, dtype=F32)))
    lg_t = jnp.broadcast_to(log_g[:, None], (RET_HEADS, 2 * LANES))
    vecs_all, mul_all, wl_all = _wkv_params(shift_prev, shift_next, w_decay_up, decay_bias, w_iclr_up, iclr_bias,
                                            w_vres_up, vres_bias, k_k, k_a, r_k, lnx_gain, lnx_bias)

    x2 = x.reshape(t, D)
    xn2 = _norm_call(x2, norm_gain[0][None, :])
    vfirst = None
    for l in range(depth):
        proj3 = _proj_call(xn2, w_wkv[l], F32).reshape(bsz, seq, N_WKV)
        rest2 = _proj_call(xn2, w_rest[l], BF16)
        rest3 = rest2.reshape(bsz, seq, N_REST)

        vecs, mul, wl = vecs_all[l], mul_all[l], wl_all[l]

        if l == 0:
            ya, vfirst = _wkv_call(proj3, None, vecs, mul, wl, first=True)
        else:
            ya = _wkv_call(proj3, vfirst, vecs, mul, wl, first=False)
        yb = _ret_call(rest3, cos_t, sin_t, lg_t, ret_norm_gain[l][None, :])
        last = l == depth - 1
        gain = final_gain if last else norm_gain[l + 1]
        res = _merge_call(x2, ya.reshape(t, D), yb.reshape(t, D), rest2, wa_all[l], wb_all[l], wo_all[l],
                          gain[None, :], last=last)
        if not last:
            x2, xn2 = res

    return res.reshape(bsz, seq, D)
```

```python
import functools

import numpy as np
import jax
import jax.numpy as jnp
from jax import lax
from jax.experimental import pallas as pl
from jax.experimental.pallas import tpu as pltpu

F32 = jnp.float32
BF16 = jnp.bfloat16

D = 1024
HEAD = 64
LANES = 128
CH = 64
SLAB = 512
RET_HEADS = 8
RET_C = 256
NORM_EPS = 1e-6
GN_EPS = 64e-5
EXP_M_HALF = 0.6065306597126334
ROPE_BASE = 10000.0
V7X_VMEM_BYTES = 64 * 1024 * 1024
VMEM_LIMIT = V7X_VMEM_BYTES * 7 // 8

C_R, C_K, C_V, C_GA, C_DEC, C_ICL, C_VRES, N_WKV = 0, 1024, 2048, 3072, 4096, 4224, 4352, 4480
C_VB, C_GB, C_MA, C_MB, C_QB, C_KB, N_REST = 0, 1024, 2048, 3072, 4096, 4608, 5120

_NN = (((1,), (0,)), ((), ()))
_NT = (((1,), (1,)), ((), ()))
_TN = (((0,), (0,)), ((), ()))


def _dg(a, b, dims):
    return lax.dot_general(a, b, dims, preferred_element_type=F32)


def _mm(a, b, dims=_NN):
    return _dg(a.astype(BF16), b.astype(BF16), dims)


def _headsum(xs, ones_bd):
    return [_dg(x.astype(BF16), ones_bd, _NN) for x in xs]


def _iota(shape, axis):
    return lax.broadcasted_iota(jnp.int32, shape, axis)


def _pair_bd(x, bdmask):
    return jnp.where(bdmask, jnp.concatenate([x, x], axis=0), 0.0)


def _wkv_consts():
    row = _iota((CH, LANES), 0)
    col = _iota((CH, LANES), 1) % CH
    r128 = _iota((LANES, LANES), 0)
    c128 = _iota((LANES, LANES), 1)

    def masks(lower):
        late, early = (row, col) if lower else (col, row)
        return dict(
            strict=early < late, incl=early <= late,
            m_d=(early < late) & ((row // 16) == (col // 16)),
            m_o1=((row // 32) == (col // 32)) & ((late // 16) % 2 == 1) & ((early // 16) % 2 == 0),
            m_o2=((late // 32) == 1) & ((early // 32) == 0))

    return dict(bd=(r128 // CH) == (c128 // CH), eye128=r128 == c128,
                eye_cat=jnp.where(row == col, 1.0, 0.0).astype(F32),
                fwd=masks(True), bwd=masks(False))


def _wkv_mxu_steps(items, cst):
    bdmask = cst["bd"]

    def bd(x):
        return _pair_bd(x, bdmask)

    mm = _mm

    st = [dict(it=it, mk=cst["bwd"] if it["rev"] else cst["fwd"]) for it in items]
    for s in st:
        ld = s["it"]["load"]
        amat = mm(jnp.concatenate([ld("at"), ld("rt")], axis=0),
                  jnp.concatenate([bd(ld("bt")), bd(ld("kt"))], axis=0), _NT)
        mk = s["mk"]
        s["n"] = amat[:CH, :LANES]
        s["akrk"] = jnp.concatenate([jnp.where(mk["strict"], amat[:CH, LANES:], 0.0),
                                     jnp.where(mk["incl"], amat[CH:, LANES:], 0.0)], axis=0)
        s["rb"] = jnp.where(mk["incl"], amat[CH:, :LANES], 0.0)
    yield
    for s in st:
        xv = mm(s["akrk"], bd(s["it"]["load"]("v")))
        s["x1"], s["y0v"] = xv[:CH], xv[CH:]
    yield
    for s in st:
        s["p"] = jnp.where(s["mk"]["m_d"], s["n"], 0.0)
        s["t"] = cst["eye_cat"] + s["p"]
        s["q"] = mm(s["p"], bd(s["p"]))
    yield
    for _ in range(2):
        for s in st:
            qt = mm(jnp.concatenate([s["q"], s["t"]], axis=0), bd(s["q"]))
            s["t"] = s["t"] + qt[CH:]
            s["q"] = qt[:CH]
        yield
    for s in st:
        s["t"] = s["t"] + mm(s["t"], bd(s["q"]))
    yield
    for key in ("m_o1", "m_o2"):
        for s in st:
            s["x"] = mm(s["t"], bd(jnp.where(s["mk"][key], s["n"], 0.0)))
        yield
        for s in st:
            s["t"] = s["t"] + mm(s["x"], bd(s["t"]))
        yield
    for s in st:
        au = mm(s["t"], jnp.concatenate([bd(s["it"]["load"]("at")), bd(s["x1"])], axis=1))
        s["ah"], s["u0"] = au[:, :LANES], au[:, LANES:]
    yield
    for s in st:
        ld = s["it"]["load"]
        w_tot = s["it"]["w_tot"]()
        ry = mm(s["rb"], jnp.concatenate([bd(s["ah"]), bd(s["u0"])], axis=1))
        rhs2 = jnp.concatenate([jnp.concatenate([s["ah"], s["u0"]], axis=1),
                                jnp.concatenate([jnp.zeros((CH, LANES), F32), ld("v")], axis=1)], axis=0)
        kb_end = jnp.concatenate([ld("bt") * w_tot, ld("kt") * w_tot], axis=0)
        mg = mm(kb_end, rhs2, _TN)
        m = jnp.where(bdmask, mg[:, :LANES], 0.0) + jnp.where(cst["eye128"], w_tot, 0.0)
        g = jnp.where(bdmask, mg[:, LANES:], 0.0)
        s["it"]["store"](ld("rt") + ry[:, :LANES], ry[:, LANES:] + s["y0v"], m, g)
    yield


def _run(gen):
    for _ in gen:
        pass


def _interleave(main, side, ratio):
    for _ in main:
        for _ in range(ratio):
            next(side, None)
    _run(side)


def _chain(gens):
    for g in gens:
        yield from g


def _sigmoid(x):
    return 0.5 * jnp.tanh(0.5 * x) + 0.5


def _shifted(ref, start, n, mp, mn, seq):
    sub = 8
    edge = _iota((sub, LANES), 0)
    cur = ref[start:start + n, :]
    if start > 0:
        prev = ref[start - 1:start - 1 + n, :]
    else:
        head = jnp.where(edge == 0, 0.0, pltpu.roll(ref[0:sub, :], 1, 0))
        prev = jnp.concatenate([head, ref[sub - 1:n - 1, :]], axis=0)
    if start + n < seq:
        nxt = ref[start + 1:start + 1 + n, :]
    else:
        tail = jnp.where(edge == sub - 1, 0.0, pltpu.roll(ref[seq - sub:seq, :], sub - 1, 0))
        nxt = jnp.concatenate([ref[start + 1:seq - sub + 1, :], tail], axis=0)
    return cur * (1.0 - mp - mn) + mp * prev + mn * nxt


def _chunk_cumsum(x, rev):
    n = x.shape[0]
    sub = 8
    pos = _iota((n, LANES), 0) % CH
    s = 1
    while s < sub:
        if rev:
            x = x + jnp.where(pos < CH - s, pltpu.roll(x, n - s, 0), 0.0)
        else:
            x = x + jnp.where(pos >= s, pltpu.roll(x, s, 0), 0.0)
        s *= 2
    g = CH // sub
    x = x.reshape(n // CH, g, sub, LANES)
    s = 1
    while s < g:
        z = jnp.zeros((n // CH, s, sub, LANES), F32)
        if rev:
            x = x + jnp.concatenate([x[:, s:], z], axis=1)
        else:
            x = x + jnp.concatenate([z, x[:, :-s]], axis=1)
        s *= 2
    return x.reshape(n, LANES)


(V_MPR, V_MPK, V_MPV, V_MNR, V_MNK, V_MNV, V_DBF, V_DBB, V_IBF, V_IBB,
 V_KK, V_KA, V_RK, V_LG, V_LB, V_VB) = range(16)
W_DEC, W_ICL, W_VR = range(3)
I_RT, I_AT, I_KT, I_BT = range(4)
I_NAMES = dict(rt=I_RT, at=I_AT, kt=I_KT, bt=I_BT)


def _wkv_kernel(*refs, seq, first):
    if first:
        (r_ref, k_ref, v_ref, ga_ref, dec_ref, icl_ref, vec_ref, mul_ref, wl_ref,
         ya_ref, vf_out_ref, lhs_s, add_s, yf_s, yb_s, bon_s, it_s, v_s, wt_s) = refs
        vres_ref = vf_ref = None
    else:
        (r_ref, k_ref, v_ref, ga_ref, dec_ref, icl_ref, vres_ref, vf_ref, vec_ref, mul_ref, wl_ref,
         ya_ref, lhs_s, add_s, yf_s, yb_s, bon_s, it_s, v_s, wt_s) = refs
        vf_out_ref = None
    ns = seq // SLAB
    per = SLAB // CH
    cst = _wkv_consts()
    ones_bd = jnp.where(cst["bd"], 1.0, 0.0).astype(BF16)

    def vec(i):
        return vec_ref[i:i + 1, :]

    def prologue(sb):
        start = sb * SLAB
        rows = slice(start, start + SLAB)
        r = _shifted(r_ref, start, SLAB, vec(V_MPR), vec(V_MNR), seq)
        yield
        k = _shifted(k_ref, start, SLAB, vec(V_MPK), vec(V_MNK), seq)
        yield
        v = _shifted(v_ref, start, SLAB, vec(V_MPV), vec(V_MNV), seq)
        yield
        dec = _shifted(dec_ref, start, SLAB, mul_ref[0:1, :], mul_ref[2:3, :], seq)
        icl = _shifted(icl_ref, start, SLAB, mul_ref[1:2, :], mul_ref[3:4, :], seq)
        yield
        if first:
            vf_out_ref[rows, :] = v
        else:
            mix = _sigmoid(vec(V_VB) + _mm(vres_ref[rows, :], wl_ref[W_VR])[:, :LANES])
            v = v + (vf_ref[rows, :] - v) * mix
        v_s[rows, :] = v
        yield
        zd = _mm(jnp.tanh(dec), wl_ref[W_DEC])
        zi = _mm(icl, wl_ref[W_ICL])
        kk = k * vec(V_KK)
        yield
        kds, ags, lws = [], [], []
        kh = k * (0.5 * vec(V_KA))
        kb2 = k - kh
        for d in range(2):
            th_w = jnp.tanh(vec(V_DBF + d) + zd[:, d * LANES:(d + 1) * LANES])
            lws.append((-0.5 * EXP_M_HALF) * th_w - 0.5 * EXP_M_HALF)
            th_a = jnp.tanh(vec(V_IBF + d) + zi[:, d * LANES:(d + 1) * LANES])
            ags.append(0.5 * th_a + 0.5)
            kds.append(kb2 + kh * th_a)
            yield
        ssq, bsum = _headsum([kk * kk, r * (kds[0] + kds[1]) * vec(V_RK)], ones_bd)
        kk = kk * lax.rsqrt(ssq + 1e-12)
        bon_s[rows, :] = bsum * v
        yield
        for d in range(2):
            cum = _chunk_cumsum(lws[d], d == 1)
            yield
            e_cum = jnp.exp(cum)
            e_inv = 1.0 / e_cum
            it_s[d, I_RT, rows, :] = r * e_cum
            it_s[d, I_AT, rows, :] = -kk * jnp.exp(cum - lws[d])
            yield
            it_s[d, I_KT, rows, :] = kds[d] * e_inv
            it_s[d, I_BT, rows, :] = (kk * ags[d]) * e_inv
            for i in range(per):
                c = sb * per + i
                row = i * CH if d == 1 else i * CH + CH - 1
                wt_s[d, c * 8:(c + 1) * 8, :] = jnp.broadcast_to(jnp.exp(cum[row:row + 1, :]), (8, LANES))
            yield

    def chunk_items(step):
        buf = step % 2
        items = []
        for d in range(2):
            slab = step if d == 0 else ns - 1 - step
            for i in range(per):
                c = slab * per + i
                crow = slice(c * CH, (c + 1) * CH)

                def load(name, d=d, crow=crow):
                    if name == "v":
                        return v_s[crow, :]
                    return it_s[d, I_NAMES[name], crow, :]

                def w_tot(d=d, c=c):
                    return wt_s[d, c * 8:c * 8 + 1, :]

                def store(rh, y0, m, g, d=d, i=i):
                    lhs_s[buf, d, i, :CH, :] = rh
                    lhs_s[buf, d, i, CH:, :] = m
                    add_s[buf, d, i, :CH, :] = y0
                    add_s[buf, d, i, CH:, :] = g

                items.append(dict(rev=d == 1, load=load, w_tot=w_tot, store=store))
        return items

    state = [jnp.zeros((LANES, LANES), F32), jnp.zeros((LANES, LANES), F32)]

    def scan(step):
        buf = step % 2
        for j in range(per):
            for d in range(2):
                i = j if d == 0 else per - 1 - j
                slab = step if d == 0 else ns - 1 - step
                o = _mm(lhs_s[buf, d, i], state[d]) + add_s[buf, d, i]
                rows = slice((slab * per + i) * CH, (slab * per + i + 1) * CH)
                (yf_s if d == 0 else yb_s)[rows, :] = o[:CH]
                state[d] = o[CH:]
            yield

    def post(start, n):
        rows = slice(start, start + n)
        y = yf_s[rows, :] + yb_s[rows, :]
        mu = _headsum([y], ones_bd)[0] * (1.0 / HEAD)
        yield
        dv = y - mu
        var = _headsum([dv * dv], ones_bd)[0] * (1.0 / HEAD)
        yield
        out = dv * lax.rsqrt(var + GN_EPS) * vec(V_LG) + vec(V_LB) + bon_s[rows, :]
        ga = ga_ref[rows, :]
        ya_ref[rows, :] = (out * (ga * _sigmoid(ga))).astype(BF16)
        yield

    ends = sorted({0, ns - 1})
    inner = [sb for sb in range(ns) if sb not in ends]
    for sb in ends:
        _run(prologue(sb))
    for step in range(ns):
        side = _chain([prologue(sb) for sb in inner]) if step == 0 else scan(step - 1)
        _interleave(_wkv_mxu_steps(chunk_items(step), cst), side, 3 if step == 0 else 1)
    _interleave(scan(ns - 1), _chain([post(sb * SLAB, SLAB) for sb in inner]), 1)
    for sb in ends:
        _run(post(sb * SLAB, SLAB))


def _wkv_call(proj3, vfirst, vecs, mul, wl, *, first):
    bsz, seq, _ = proj3.shape
    nc = seq // CH
    per = SLAB // CH

    def col(base):
        blk = base // LANES
        return pl.BlockSpec((None, seq, LANES), lambda b, p, blk=blk: (b, 0, blk + p))

    def fixed(base):
        blk = base // LANES
        return pl.BlockSpec((None, seq, LANES), lambda b, p, blk=blk: (b, 0, blk))

    in_specs = [col(C_R), col(C_K), col(C_V), col(C_GA), fixed(C_DEC), fixed(C_ICL)]
    args = [proj3] * 6
    if not first:
        in_specs += [fixed(C_VRES), pl.BlockSpec((None, seq, LANES), lambda b, p: (b, 0, p))]
        args += [proj3, vfirst]
    in_specs += [pl.BlockSpec((16, LANES), lambda b, p: (0, p)),
                 pl.BlockSpec((8, LANES), lambda b, p: (0, 0)),
                 pl.BlockSpec((None, 3, LANES, 2 * LANES), lambda b, p: (p, 0, 0, 0))]
    args += [vecs, mul, wl]
    out_block = pl.BlockSpec((None, seq, LANES), lambda b, p: (b, 0, p))
    out_sds = jax.ShapeDtypeStruct((bsz, seq, D), BF16)
    out_shape = (out_sds, jax.ShapeDtypeStruct((bsz, seq, D), F32)) if first else out_sds
    out_specs = (out_block, out_block) if first else out_block
    scratch = [pltpu.VMEM((2, 2, per, CH + LANES, LANES), F32),
               pltpu.VMEM((2, 2, per, CH + LANES, LANES), F32),
               pltpu.VMEM((seq, LANES), F32), pltpu.VMEM((seq, LANES), F32), pltpu.VMEM((seq, LANES), F32),
               pltpu.VMEM((2, 4, seq, LANES), F32),
               pltpu.VMEM((seq, LANES), F32),
               pltpu.VMEM((2, nc * 8, LANES), F32)]
    return pl.pallas_call(
        functools.partial(_wkv_kernel, seq=seq, first=first),
        grid=(bsz, D // LANES),
        in_specs=in_specs, out_specs=out_specs, out_shape=out_shape,
        scratch_shapes=scratch,
        compiler_params=pltpu.CompilerParams(dimension_semantics=("arbitrary", "arbitrary"),
                                             vmem_limit_bytes=VMEM_LIMIT),
        name="wkv7",
    )(*args)


def _rms(x, g):
    ms = jnp.mean(x * x, axis=-1, keepdims=True)
    return x * lax.rsqrt(ms + NORM_EPS) * g


def _proj_kernel(xn_ref, w_ref, o_ref):
    o_ref[...] = jnp.dot(xn_ref[...], w_ref[...], preferred_element_type=F32).astype(o_ref.dtype)


PROJ_BLOCK_BUDGET = 40 * 1024 * 1024


def _proj_tiles(t, n, out_bytes):
    best = None
    for tm in (256, 512, 1024):
        if t % tm:
            continue
        for tn in range(LANES, n + 1, LANES):
            if n % tn == 0 and 2 * (tm * D * 2 + D * tn * 2 + tm * tn * out_bytes) <= PROJ_BLOCK_BUDGET:
                if best is None or tm * tn > best[0] * best[1]:
                    best = (tm, tn)
    return best


def _proj_call(xn2, w, out_dtype):
    t = xn2.shape[0]
    n = w.shape[1]
    tm, tn = _proj_tiles(t, n, jnp.dtype(out_dtype).itemsize)
    return pl.pallas_call(
        _proj_kernel,
        grid=(n // tn, t // tm),
        in_specs=[pl.BlockSpec((tm, D), lambda j, i: (i, 0)),
                  pl.BlockSpec((D, tn), lambda j, i: (0, j))],
        out_specs=pl.BlockSpec((tm, tn), lambda j, i: (i, j)),
        out_shape=jax.ShapeDtypeStruct((t, n), out_dtype),
        compiler_params=pltpu.CompilerParams(dimension_semantics=("arbitrary", "arbitrary"),
                                             vmem_limit_bytes=VMEM_LIMIT),
        name="proj",
    )(xn2, w)


def _ret_kernel(q_ref, k_ref, v_ref, gb_ref, cos_ref, sin_ref, lg_ref, gain_ref, o_ref, qr_s, kr_s, y_s, *, seq):
    nc = seq // RET_C
    p = pl.program_id(1)
    scale = HEAD ** -0.5

    def rot(c, _):
        sl = pl.ds(pl.multiple_of(c * RET_C, RET_C), RET_C)
        cs, sn = cos_ref[sl, :], sin_ref[sl, :]
        q = q_ref[sl, :].astype(F32)
        k = k_ref[sl, :].astype(F32)
        qr_s[sl, :] = (q * cs + pltpu.roll(q, 64, 1) * sn) * scale
        kr_s[sl, :] = k * cs + pltpu.roll(k, 64, 1) * sn
        return 0

    lax.fori_loop(0, nc, rot, 0)

    lane = _iota((RET_C, LANES), 1)
    rowi = _iota((RET_C, LANES), 0).astype(F32)
    absd = jnp.abs(_iota((RET_C, RET_C), 0) - _iota((RET_C, RET_C), 1)).astype(F32)
    for j in range(2):
        hmask = ((lane // 32) % 2) == j
        lg = lg_ref[pl.ds(2 * p + j, 1), :]
        lg128 = lg[:, :LANES]
        dmat = jnp.exp(absd * lg)
        xi_f = jnp.exp((rowi + 1.0) * lg128)
        ze_f = jnp.exp((RET_C - 1.0 - rowi) * lg128)
        xi_b = jnp.exp((RET_C - rowi) * lg128)
        ze_b = jnp.exp(rowi * lg128)
        cdec = jnp.exp(RET_C * lg128)
        vs = slice(j * LANES, (j + 1) * LANES)
        fst = jnp.zeros((LANES, LANES), F32)
        for i in range(nc):
            sl = slice(i * RET_C, (i + 1) * RET_C)
            qh = jnp.where(hmask, qr_s[sl, :], 0.0)
            kh = kr_s[sl, :]
            vh = v_ref[sl, vs]
            sc = _mm(qh, kh, _NT) * dmat
            y = _mm(sc, vh)
            if i > 0:
                y = y + _mm(qh * xi_f, fst)
            if i < nc - 1:
                fst = fst * cdec + _mm(jnp.where(hmask, kh, 0.0) * ze_f, vh, _TN)
            y_s[sl, vs] = y
        gst = jnp.zeros((LANES, LANES), F32)
        for i in range(nc - 1, -1, -1):
            sl = slice(i * RET_C, (i + 1) * RET_C)
            y = y_s[sl, vs]
            if i < nc - 1:
                qh = jnp.where(hmask, qr_s[sl, :], 0.0)
                y = y + _mm(qh * xi_b, gst)
            if i > 0:
                kh = jnp.where(hmask, kr_s[sl, :], 0.0)
                gst = gst * cdec + _mm(kh * ze_b, v_ref[sl, vs], _TN)
            mu = jnp.mean(y, axis=-1, keepdims=True)
            dv = y - mu
            var = jnp.mean(dv * dv, axis=-1, keepdims=True)
            out = dv * lax.rsqrt(var + NORM_EPS) * gain_ref[:, vs]
            o_ref[sl, vs] = (out * jax.nn.silu(gb_ref[sl, vs].astype(F32))).astype(BF16)


def _ret_call(proj3, cos_t, sin_t, lg_t, gain):
    bsz, seq, _ = proj3.shape
    w2 = 2 * LANES

    def spec(base, width):
        blk = base // width
        return pl.BlockSpec((None, seq, width), lambda b, p, blk=blk: (b, 0, blk + p))

    return pl.pallas_call(
        functools.partial(_ret_kernel, seq=seq),
        grid=(bsz, RET_HEADS // 2),
        in_specs=[spec(C_QB, LANES), spec(C_KB, LANES), spec(C_VB, w2), spec(C_GB, w2),
                  pl.BlockSpec((seq, LANES), lambda b, p: (0, 0)),
                  pl.BlockSpec((seq, LANES), lambda b, p: (0, 0)),
                  pl.BlockSpec((RET_HEADS, w2), lambda b, p: (0, 0)),
                  pl.BlockSpec((1, w2), lambda b, p: (0, p))],
        out_specs=pl.BlockSpec((None, seq, w2), lambda b, p: (b, 0, p)),
        out_shape=jax.ShapeDtypeStruct((bsz, seq, D), BF16),
        scratch_shapes=[pltpu.VMEM((seq, LANES), F32), pltpu.VMEM((seq, LANES), F32), pltpu.VMEM((seq, w2), F32)],
        compiler_params=pltpu.CompilerParams(dimension_semantics=("arbitrary", "arbitrary"),
                                             vmem_limit_bytes=VMEM_LIMIT),
        name="retention",
    )(proj3, proj3, proj3, proj3, cos_t, sin_t, lg_t, gain)


def _merge_kernel(x_ref, ya_ref, yb_ref, ma_ref, mb_ref, wa_ref, wb_ref, wo_ref, g_ref, *o_refs, last):
    za = jnp.dot(ya_ref[...], wa_ref[...], preferred_element_type=F32)
    zb = jnp.dot(yb_ref[...], wb_ref[...], preferred_element_type=F32)
    merged = jax.nn.sigmoid(ma_ref[...].astype(F32)) * za + jax.nn.sigmoid(mb_ref[...].astype(F32)) * zb
    x = x_ref[...] + jnp.dot(merged.astype(BF16), wo_ref[...], preferred_element_type=F32)
    if last:
        o_refs[0][...] = _rms(x, g_ref[...])
    else:
        o_refs[0][...] = x
        o_refs[1][...] = _rms(x, g_ref[...]).astype(BF16)


def _merge_call(x2, ya2, yb2, proj2, wa, wb, wo, gain, *, last):
    t = x2.shape[0]
    tm = min(512, t)
    row = pl.BlockSpec((tm, D), lambda i: (i, 0))
    wspec = pl.BlockSpec((D, D), lambda i: (0, 0))
    x_sds = jax.ShapeDtypeStruct((t, D), F32)
    return pl.pallas_call(
        functools.partial(_merge_kernel, last=last),
        grid=(t // tm,),
        in_specs=[row, row, row,
                  pl.BlockSpec((tm, D), lambda i: (i, C_MA // D)),
                  pl.BlockSpec((tm, D), lambda i: (i, C_MB // D)),
                  wspec, wspec, wspec, pl.BlockSpec((1, D), lambda i: (0, 0))],
        out_specs=row if last else (row, row),
        out_shape=x_sds if last else (x_sds, jax.ShapeDtypeStruct((t, D), BF16)),
        compiler_params=pltpu.CompilerParams(dimension_semantics=("arbitrary",),
                                             vmem_limit_bytes=VMEM_LIMIT),
        name="merge",
    )(x2, ya2, yb2, proj2, proj2, wa, wb, wo, gain)


def _norm_kernel(x_ref, g_ref, o_ref):
    o_ref[...] = _rms(x_ref[...], g_ref[...]).astype(BF16)


def _norm_call(x2, gain):
    t = x2.shape[0]
    tm = min(1024, t)
    return pl.pallas_call(
        _norm_kernel,
        grid=(t // tm,),
        in_specs=[pl.BlockSpec((tm, D), lambda i: (i, 0)), pl.BlockSpec((1, D), lambda i: (0, 0))],
        out_specs=pl.BlockSpec((tm, D), lambda i: (i, 0)),
        out_shape=jax.ShapeDtypeStruct((t, D), BF16),
        compiler_params=pltpu.CompilerParams(dimension_semantics=("arbitrary",)),
        name="first_norm",
    )(x2, gain)


def _qk_perm():
    idx = []
    for pair in range(RET_HEADS // 2):
        for lane in range(LANES):
            seg, i = lane // 32, lane % 32
            head, half = 2 * pair + seg % 2, seg // 2
            idx.append(head * HEAD + half * 32 + i)
    return np.asarray(idx, np.int32)


def _pack_w_in(w_in, w_vres_down):
    depth = w_in.shape[0]
    perm = _qk_perm()
    vres = jnp.concatenate([jnp.zeros((1, D, 32), F32), w_vres_down], axis=0)
    vres = jnp.pad(vres, ((0, 0), (0, 0), (0, LANES - 32)))
    w_wkv = jnp.concatenate([w_in[:, :, 0:3072], w_in[:, :, 3328:4352], w_in[:, :, 3072:3328], vres], axis=2)
    w_rest = jnp.concatenate([w_in[:, :, 5376:6400], w_in[:, :, 6400:7424], w_in[:, :, 7424:8448],
                              w_in[:, :, 8448:9472], w_in[:, :, 4352:4864][:, :, perm],
                              w_in[:, :, 4864:5376][:, :, perm]], axis=2)
    assert w_wkv.shape == (depth, D, N_WKV) and w_rest.shape == (depth, D, N_REST)
    return w_wkv.astype(BF16), w_rest.astype(BF16)


def _pair_cat(wf, wb):
    depth, pairs = wf.shape[0], D // LANES
    cat = jnp.concatenate([wf.reshape(depth, LANES, pairs, LANES), wb.reshape(depth, LANES, pairs, LANES)], axis=3)
    return cat.transpose(0, 2, 1, 3)


def _wkv_params(shift_prev, shift_next, w_decay_up, decay_bias, w_iclr_up, iclr_bias, w_vres_up, vres_bias,
                k_k, k_a, r_k, lnx_gain, lnx_bias):
    depth = k_k.shape[0]
    vb = jnp.concatenate([jnp.zeros((1, D), F32), vres_bias], axis=0)
    vecs = jnp.stack([shift_prev[:, 0:1024], shift_prev[:, 1024:2048], shift_prev[:, 2048:3072],
                      shift_next[:, 0:1024], shift_next[:, 1024:2048], shift_next[:, 2048:3072],
                      0.5 * decay_bias[:, 0], 0.5 * decay_bias[:, 1], 0.5 * iclr_bias[:, 0], 0.5 * iclr_bias[:, 1],
                      k_k, k_a, r_k.reshape(depth, D), lnx_gain, lnx_bias, vb], axis=1)
    mul = jnp.concatenate([jnp.stack([shift_prev[:, 3072:3200], shift_prev[:, 3200:3328],
                                      shift_next[:, 3072:3200], shift_next[:, 3200:3328]], axis=1),
                           jnp.zeros((depth, 4, LANES), F32)], axis=1)
    zpad = jnp.zeros((depth, 64, D), F32)
    zmat = jnp.zeros((depth, LANES, D), F32)
    wvr = jnp.concatenate([jnp.zeros((1, LANES, D), F32),
                           jnp.pad(w_vres_up, ((0, 0), (0, LANES - 32), (0, 0)))], axis=0)
    wl = jnp.stack([0.5 * _pair_cat(jnp.concatenate([w_decay_up[:, 0], zpad], axis=1),
                                    jnp.concatenate([zpad, w_decay_up[:, 1]], axis=1)),
                    0.5 * _pair_cat(jnp.concatenate([w_iclr_up[:, 0], zpad], axis=1),
                                    jnp.concatenate([zpad, w_iclr_up[:, 1]], axis=1)),
                    _pair_cat(wvr, zmat)], axis=2)
    return vecs, mul, wl


def _rope_tables(seq):
    half = 32
    freqs = jnp.power(ROPE_BASE, -jnp.arange(half, dtype=F32) / half)
    ang = jnp.arange(seq, dtype=F32)[:, None] * freqs[None, :]
    cos = jnp.tile(jnp.cos(ang), (1, 4))
    sin = jnp.tile(jnp.sin(ang), (1, 4))
    sign = jnp.where(jnp.arange(LANES) < 64, -1.0, 1.0).astype(F32)
    return cos, sin * sign[None, :]


def kernel(x, norm_gain, w_in, w_vres_down, shift_prev, shift_next, w_decay_up, decay_bias, w_iclr_up, iclr_bias,
           w_vres_up, vres_bias, k_k, k_a, r_k, lnx_gain, lnx_bias, w_branch_a, ret_norm_gain, w_branch_b, w_out,
           final_gain):
    bsz, seq, _ = x.shape
    depth = w_in.shape[0]
    assert seq % RET_C == 0 and seq % SLAB == 0 and SLAB % CH == 0
    t = bsz * seq

    w_wkv, w_rest = _pack_w_in(w_in, w_vres_down)
    wa_all, wb_all, wo_all = (w.astype(BF16) for w in (w_branch_a, w_branch_b, w_out))
    cos_t, sin_t = _rope_tables(seq)
    log_g = jnp.log1p(-jnp.exp2(-5.0 - jnp.arange(RET_HEADS, dtype=F32)))
    lg_t = jnp.broadcast_to(log_g[:, None], (RET_HEADS, 2 * LANES))
    vecs_all, mul_all, wl_all = _wkv_params(shift_prev, shift_next, w_decay_up, decay_bias, w_iclr_up, iclr_bias,
                                            w_vres_up, vres_bias, k_k, k_a, r_k, lnx_gain, lnx_bias)

    x2 = x.reshape(t, D)
    xn2 = _norm_call(x2, norm_gain[0][None, :])
    vfirst = None
    for l in range(depth):
        proj3 = _proj_call(xn2, w_wkv[l], F32).reshape(bsz, seq, N_WKV)
        rest2 = _proj_call(xn2, w_rest[l], BF16)
        rest3 = rest2.reshape(bsz, seq, N_REST)

        vecs, mul, wl = vecs_all[l], mul_all[l], wl_all[l]

        if l == 0:
            ya, vfirst = _wkv_call(proj3, None, vecs, mul, wl, first=True)
        else:
            ya = _wkv_call(proj3, vfirst, vecs, mul, wl, first=False)
        yb = _ret_call(rest3, cos_t, sin_t, lg_t, ret_norm_gain[l][None, :])
        last = l == depth - 1
        gain = final_gain if last else norm_gain[l + 1]
        res = _merge_call(x2, ya.reshape(t, D), yb.reshape(t, D), rest2, wa_all[l], wb_all[l], wo_all[l],
                          gain[None, :], last=last)
        if not last:
            x2, xn2 = res

    return res.reshape(bsz, seq, D)
```

```python
import functools

import numpy as np
import jax
import jax.numpy as jnp
from jax import lax
from jax.experimental import pallas as pl
from jax.experimental.pallas import tpu as pltpu

F32 = jnp.float32
BF16 = jnp.bfloat16

D = 1024
HEAD = 64
LANES = 128
CH = 64
SLAB = 512
RET_HEADS = 8
RET_C = 256
NORM_EPS = 1e-6
GN_EPS = 64e-5
EXP_M_HALF = 0.6065306597126334
ROPE_BASE = 10000.0
V7X_VMEM_BYTES = 64 * 1024 * 1024
VMEM_LIMIT = V7X_VMEM_BYTES * 7 // 8

C_R, C_K, C_V, C_GA, C_DEC, C_ICL, C_VRES, N_WKV = 0, 1024, 2048, 3072, 4096, 4224, 4352, 4480
C_VB, C_GB, C_MA, C_MB, C_QB, C_KB, N_REST = 0, 1024, 2048, 3072, 4096, 4608, 5120

_NN = (((1,), (0,)), ((), ()))
_NT = (((1,), (1,)), ((), ()))
_TN = (((0,), (0,)), ((), ()))


def _dg(a, b, dims):
    return lax.dot_general(a, b, dims, preferred_element_type=F32)


def _mm(a, b, dims=_NN):
    return _dg(a.astype(BF16), b.astype(BF16), dims)


def _headsum(xs, ones_bd):
    return [_dg(x.astype(BF16), ones_bd, _NN) for x in xs]


def _iota(shape, axis):
    return lax.broadcasted_iota(jnp.int32, shape, axis)


def _pair_bd(x, bdmask):
    return jnp.where(bdmask, jnp.concatenate([x, x], axis=0), 0.0)


def _wkv_consts():
    row = _iota((CH, LANES), 0)
    col = _iota((CH, LANES), 1) % CH
    r128 = _iota((LANES, LANES), 0)
    c128 = _iota((LANES, LANES), 1)

    def masks(lower):
        late, early = (row, col) if lower else (col, row)
        return dict(
            strict=early < late, incl=early <= late,
            m_d=(early < late) & ((row // 16) == (col // 16)),
            m_o1=((row // 32) == (col // 32)) & ((late // 16) % 2 == 1) & ((early // 16) % 2 == 0),
            m_o2=((late // 32) == 1) & ((early // 32) == 0))

    return dict(bd=(r128 // CH) == (c128 // CH), eye128=r128 == c128,
                eye_cat=jnp.where(row == col, 1.0, 0.0).astype(F32),
                fwd=masks(True), bwd=masks(False))


def _wkv_mxu_steps(items, cst):
    bdmask = cst["bd"]

    def bd(x):
        return _pair_bd(x, bdmask)

    mm = _mm

    st = [dict(it=it, mk=cst["bwd"] if it["rev"] else cst["fwd"]) for it in items]
    for s in st:
        ld = s["it"]["load"]
        amat = mm(jnp.concatenate([ld("at"), ld("rt")], axis=0),
                  jnp.concatenate([bd(ld("bt")), bd(ld("kt"))], axis=0), _NT)
        mk = s["mk"]
        s["n"] = amat[:CH, :LANES]
        s["akrk"] = jnp.concatenate([jnp.where(mk["strict"], amat[:CH, LANES:], 0.0),
                                     jnp.where(mk["incl"], amat[CH:, LANES:], 0.0)], axis=0)
        s["rb"] = jnp.where(mk["incl"], amat[CH:, :LANES], 0.0)
    yield
    for s in st:
        xv = mm(s["akrk"], bd(s["it"]["load"]("v")))
        s["x1"], s["y0v"] = xv[:CH], xv[CH:]
    yield
    for s in st:
        s["p"] = jnp.where(s["mk"]["m_d"], s["n"], 0.0)
        s["t"] = cst["eye_cat"] + s["p"]
        s["q"] = mm(s["p"], bd(s["p"]))
    yield
    for _ in range(2):
        for s in st:
            qt = mm(jnp.concatenate([s["q"], s["t"]], axis=0), bd(s["q"]))
            s["t"] = s["t"] + qt[CH:]
            s["q"] = qt[:CH]
        yield
    for s in st:
        s["t"] = s["t"] + mm(s["t"], bd(s["q"]))
    yield
    for key in ("m_o1", "m_o2"):
        for s in st:
            s["x"] = mm(s["t"], bd(jnp.where(s["mk"][key], s["n"], 0.0)))
        yield
        for s in st:
            s["t"] = s["t"] + mm(s["x"], bd(s["t"]))
        yield
    for s in st:
        au = mm(s["t"], jnp.concatenate([bd(s["it"]["load"]("at")), bd(s["x1"])], axis=1))
        s["ah"], s["u0"] = au[:, :LANES], au[:, LANES:]
    yield
    for s in st:
        ld = s["it"]["load"]
        w_tot = s["it"]["w_tot"]()
        ry = mm(s["rb"], jnp.concatenate([bd(s["ah"]), bd(s["u0"])], axis=1))
        rhs2 = jnp.concatenate([jnp.concatenate([s["ah"], s["u0"]], axis=1),
                                jnp.concatenate([jnp.zeros((CH, LANES), F32), ld("v")], axis=1)], axis=0)
        kb_end = jnp.concatenate([ld("bt") * w_tot, ld("kt") * w_tot], axis=0)
        mg = mm(kb_end, rhs2, _TN)
        m = jnp.where(bdmask, mg[:, :LANES], 0.0) + jnp.where(cst["eye128"], w_tot, 0.0)
        g = jnp.where(bdmask, mg[:, LANES:], 0.0)
        s["it"]["store"](ld("rt") + ry[:, :LANES], ry[:, LANES:] + s["y0v"], m, g)
    yield


def _run(gen):
    for _ in gen:
        pass


def _interleave(main, side, ratio):
    for _ in main:
        for _ in range(ratio):
            next(side, None)
    _run(side)


def _chain(gens):
    for g in gens:
        yield from g


def _sigmoid(x):
    return 0.5 * jnp.tanh(0.5 * x) + 0.5


def _shifted(ref, start, n, mp, mn, seq):
    sub = 8
    edge = _iota((sub, LANES), 0)
    cur = ref[start:start + n, :]
    if start > 0:
        prev = ref[start - 1:start - 1 + n, :]
    else:
        head = jnp.where(edge == 0, 0.0, pltpu.roll(ref[0:sub, :], 1, 0))
        prev = jnp.concatenate([head, ref[sub - 1:n - 1, :]], axis=0)
    if start + n < seq:
        nxt = ref[start + 1:start + 1 + n, :]
    else:
        tail = jnp.where(edge == sub - 1, 0.0, pltpu.roll(ref[seq - sub:seq, :], sub - 1, 0))
        nxt = jnp.concatenate([ref[start + 1:seq - sub + 1, :], tail], axis=0)
    return cur * (1.0 - mp - mn) + mp * prev + mn * nxt


def _chunk_cumsum(x, rev):
    n = x.shape[0]
    sub = 8
    pos = _iota((n, LANES), 0) % CH
    s = 1
    while s < sub:
        if rev:
            x = x + jnp.where(pos < CH - s, pltpu.roll(x, n - s, 0), 0.0)
        else:
            x = x + jnp.where(pos >= s, pltpu.roll(x, s, 0), 0.0)
        s *= 2
    g = CH // sub
    x = x.reshape(n // CH, g, sub, LANES)
    s = 1
    while s < g:
        z = jnp.zeros((n // CH, s, sub, LANES), F32)
        if rev:
            x = x + jnp.concatenate([x[:, s:], z], axis=1)
        else:
            x = x + jnp.concatenate([z, x[:, :-s]], axis=1)
        s *= 2
    return x.reshape(n, LANES)


(V_MPR, V_MPK, V_MPV, V_MNR, V_MNK, V_MNV, V_DBF, V_DBB, V_IBF, V_IBB,
 V_KK, V_KA, V_RK, V_LG, V_LB, V_VB) = range(16)
W_DEC, W_ICL, W_VR = range(3)
I_RT, I_AT, I_KT, I_BT = range(4)
I_NAMES = dict(rt=I_RT, at=I_AT, kt=I_KT, bt=I_BT)


def _wkv_kernel(*refs, seq, first):
    if first:
        (r_ref, k_ref, v_ref, ga_ref, dec_ref, icl_ref, vec_ref, mul_ref, wl_ref,
         ya_ref, vf_out_ref, lhs_s, add_s, yf_s, yb_s, bon_s, it_s, v_s, wt_s) = refs
        vres_ref = vf_ref = None
    else:
        (r_ref, k_ref, v_ref, ga_ref, dec_ref, icl_ref, vres_ref, vf_ref, vec_ref, mul_ref, wl_ref,
         ya_ref, lhs_s, add_s, yf_s, yb_s, bon_s, it_s, v_s, wt_s) = refs
        vf_out_ref = None
    ns = seq // SLAB
    per = SLAB // CH
    cst = _wkv_consts()
    ones_bd = jnp.where(cst["bd"], 1.0, 0.0).astype(BF16)

    def vec(i):
        return vec_ref[i:i + 1, :]

    def prologue(sb):
        start = sb * SLAB
        rows = slice(start, start + SLAB)
        r = _shifted(r_ref, start, SLAB, vec(V_MPR), vec(V_MNR), seq)
        yield
        k = _shifted(k_ref, start, SLAB, vec(V_MPK), vec(V_MNK), seq)
        yield
        v = _shifted(v_ref, start, SLAB, vec(V_MPV), vec(V_MNV), seq)
        yield
        dec = _shifted(dec_ref, start, SLAB, mul_ref[0:1, :], mul_ref[2:3, :], seq)
        icl = _shifted(icl_ref, start, SLAB, mul_ref[1:2, :], mul_ref[3:4, :], seq)
        yield
        if first:
            vf_out_ref[rows, :] = v
        else:
            mix = _sigmoid(vec(V_VB) + _mm(vres_ref[rows, :], wl_ref[W_VR])[:, :LANES])
            v = v + (vf_ref[rows, :] - v) * mix
        v_s[rows, :] = v
        yield
        zd = _mm(jnp.tanh(dec), wl_ref[W_DEC])
        zi = _mm(icl, wl_ref[W_ICL])
        kk = k * vec(V_KK)
        yield
        kds, ags, lws = [], [], []
        kh = k * (0.5 * vec(V_KA))
        kb2 = k - kh
        for d in range(2):
            th_w = jnp.tanh(vec(V_DBF + d) + zd[:, d * LANES:(d + 1) * LANES])
            lws.append((-0.5 * EXP_M_HALF) * th_w - 0.5 * EXP_M_HALF)
            th_a = jnp.tanh(vec(V_IBF + d) + zi[:, d * LANES:(d + 1) * LANES])
            ags.append(0.5 * th_a + 0.5)
            kds.append(kb2 + kh * th_a)
            yield
        ssq, bsum = _headsum([kk * kk, r * (kds[0] + kds[1]) * vec(V_RK)], ones_bd)
        kk = kk * lax.rsqrt(ssq + 1e-12)
        bon_s[rows, :] = bsum * v
        yield
        for d in range(2):
            cum = _chunk_cumsum(lws[d], d == 1)
            yield
            e_cum = jnp.exp(cum)
            e_inv = 1.0 / e_cum
            it_s[d, I_RT, rows, :] = r * e_cum
            it_s[d, I_AT, rows, :] = -kk * jnp.exp(cum - lws[d])
            yield
            it_s[d, I_KT, rows, :] = kds[d] * e_inv
            it_s[d, I_BT, rows, :] = (kk * ags[d]) * e_inv
            for i in range(per):
                c = sb * per + i
                row = i * CH if d == 1 else i * CH + CH - 1
                wt_s[d, c * 8:(c + 1) * 8, :] = jnp.broadcast_to(jnp.exp(cum[row:row + 1, :]), (8, LANES))
            yield

    def chunk_items(step):
        buf = step % 2
        items = []
        for d in range(2):
            slab = step if d == 0 else ns - 1 - step
            for i in range(per):
                c = slab * per + i
                crow = slice(c * CH, (c + 1) * CH)

                def load(name, d=d, crow=crow):
                    if name == "v":
                        return v_s[crow, :]
                    return it_s[d, I_NAMES[name], crow, :]

                def w_tot(d=d, c=c):
                    return wt_s[d, c * 8:c * 8 + 1, :]

                def store(rh, y0, m, g, d=d, i=i):
                    lhs_s[buf, d, i, :CH, :] = rh
                    lhs_s[buf, d, i, CH:, :] = m
                    add_s[buf, d, i, :CH, :] = y0
                    add_s[buf, d, i, CH:, :] = g

                items.append(dict(rev=d == 1, load=load, w_tot=w_tot, store=store))
        return items

    state = [jnp.zeros((LANES, LANES), F32), jnp.zeros((LANES, LANES), F32)]

    def scan(step):
        buf = step % 2
        for j in range(per):
            for d in range(2):
                i = j if d == 0 else per - 1 - j
                slab = step if d == 0 else ns - 1 - step
                o = _mm(lhs_s[buf, d, i], state[d]) + add_s[buf, d, i]
                rows = slice((slab * per + i) * CH, (slab * per + i + 1) * CH)
                (yf_s if d == 0 else yb_s)[rows, :] = o[:CH]
                state[d] = o[CH:]
            yield

    def post(start, n):
        rows = slice(start, start + n)
        y = yf_s[rows, :] + yb_s[rows, :]
        mu = _headsum([y], ones_bd)[0] * (1.0 / HEAD)
        yield
        dv = y - mu
        var = _headsum([dv * dv], ones_bd)[0] * (1.0 / HEAD)
        yield
        out = dv * lax.rsqrt(var + GN_EPS) * vec(V_LG) + vec(V_LB) + bon_s[rows, :]
        ga = ga_ref[rows, :]
        ya_ref[rows, :] = (out * (ga * _sigmoid(ga))).astype(BF16)
        yield

    ends = sorted({0, ns - 1})
    inner = [sb for sb in range(ns) if sb not in ends]
    for sb in ends:
        _run(prologue(sb))
    for step in range(ns):
        side = _chain([prologue(sb) for sb in inner]) if step == 0 else scan(step - 1)
        _interleave(_wkv_mxu_steps(chunk_items(step), cst), side, 3 if step == 0 else 1)
    _interleave(scan(ns - 1), _chain([post(sb * SLAB, SLAB) for sb in inner]), 1)
    for sb in ends:
        _run(post(sb * SLAB, SLAB))


def _wkv_call(proj3, vfirst, vecs, mul, wl, *, first):
    bsz, seq, _ = proj3.shape
    nc = seq // CH
    per = SLAB // CH

    def col(base):
        blk = base // LANES
        return pl.BlockSpec((None, seq, LANES), lambda b, p, blk=blk: (b, 0, blk + p))

    def fixed(base):
        blk = base // LANES
        return pl.BlockSpec((None, seq, LANES), lambda b, p, blk=blk: (b, 0, blk))

    in_specs = [col(C_R), col(C_K), col(C_V), col(C_GA), fixed(C_DEC), fixed(C_ICL)]
    args = [proj3] * 6
    if not first:
        in_specs += [fixed(C_VRES), pl.BlockSpec((None, seq, LANES), lambda b, p: (b, 0, p))]
        args += [proj3, vfirst]
    in_specs += [pl.BlockSpec((16, LANES), lambda b, p: (0, p)),
                 pl.BlockSpec((8, LANES), lambda b, p: (0, 0)),
                 pl.BlockSpec((None, 3, LANES, 2 * LANES), lambda b, p: (p, 0, 0, 0))]
    args += [vecs, mul, wl]
    out_block = pl.BlockSpec((None, seq, LANES), lambda b, p: (b, 0, p))
    out_sds = jax.ShapeDtypeStruct((bsz, seq, D), BF16)
    out_shape = (out_sds, jax.ShapeDtypeStruct((bsz, seq, D), F32)) if first else out_sds
    out_specs = (out_block, out_block) if first else out_block
    scratch = [pltpu.VMEM((2, 2, per, CH + LANES, LANES), F32),
               pltpu.VMEM((2, 2, per, CH + LANES, LANES), F32),
               pltpu.VMEM((seq, LANES), F32), pltpu.VMEM((seq, LANES), F32), pltpu.VMEM((seq, LANES), F32),
               pltpu.VMEM((2, 4, seq, LANES), F32),
               pltpu.VMEM((seq, LANES), F32),
               pltpu.VMEM((2, nc * 8, LANES), F32)]
    return pl.pallas_call(
        functools.partial(_wkv_kernel, seq=seq, first=first),
        grid=(bsz, D // LANES),
        in_specs=in_specs, out_specs=out_specs, out_shape=out_shape,
        scratch_shapes=scratch,
        compiler_params=pltpu.CompilerParams(dimension_semantics=("arbitrary", "arbitrary"),
                                             vmem_limit_bytes=VMEM_LIMIT),
        name="wkv7",
    )(*args)


def _rms(x, g):
    ms = jnp.mean(x * x, axis=-1, keepdims=True)
    return x * lax.rsqrt(ms + NORM_EPS) * g


def _proj_kernel(xn_ref, w_ref, o_ref):
    o_ref[...] = jnp.dot(xn_ref[...], w_ref[...], preferred_element_type=F32).astype(o_ref.dtype)


PROJ_BLOCK_BUDGET = 40 * 1024 * 1024


def _proj_tiles(t, n, out_bytes):
    best = None
    for tm in (1024, 512, 256):
        if t % tm:
            continue
        for tn in range(LANES, n + 1, LANES):
            if n % tn == 0 and 2 * (tm * D * 2 + D * tn * 2 + tm * tn * out_bytes) <= PROJ_BLOCK_BUDGET:
                if best is None or tm * tn > best[0] * best[1]:
                    best = (tm, tn)
    return best


def _proj_call(xn2, w, out_dtype):
    t = xn2.shape[0]
    n = w.shape[1]
    tm, tn = _proj_tiles(t, n, jnp.dtype(out_dtype).itemsize)
    return pl.pallas_call(
        _proj_kernel,
        grid=(n // tn, t // tm),
        in_specs=[pl.BlockSpec((tm, D), lambda j, i: (i, 0)),
                  pl.BlockSpec((D, tn), lambda j, i: (0, j))],
        out_specs=pl.BlockSpec((tm, tn), lambda j, i: (i, j)),
        out_shape=jax.ShapeDtypeStruct((t, n), out_dtype),
        compiler_params=pltpu.CompilerParams(dimension_semantics=("arbitrary", "arbitrary"),
                                             vmem_limit_bytes=VMEM_LIMIT),
        name="proj",
    )(xn2, w)


def _ret_kernel(q_ref, k_ref, v_ref, gb_ref, cos_ref, sin_ref, lg_ref, gain_ref, o_ref, qr_s, kr_s, y_s, *, seq):
    nc = seq // RET_C
    p = pl.program_id(1)
    scale = HEAD ** -0.5

    def rot(c, _):
        sl = pl.ds(pl.multiple_of(c * RET_C, RET_C), RET_C)
        cs, sn = cos_ref[sl, :], sin_ref[sl, :]
        q = q_ref[sl, :].astype(F32)
        k = k_ref[sl, :].astype(F32)
        qr_s[sl, :] = (q * cs + pltpu.roll(q, 64, 1) * sn) * scale
        kr_s[sl, :] = k * cs + pltpu.roll(k, 64, 1) * sn
        return 0

    lax.fori_loop(0, nc, rot, 0)

    lane = _iota((RET_C, LANES), 1)
    rowi = _iota((RET_C, LANES), 0).astype(F32)
    absd = jnp.abs(_iota((RET_C, RET_C), 0) - _iota((RET_C, RET_C), 1)).astype(F32)
    for j in range(2):
        hmask = ((lane // 32) % 2) == j
        lg = lg_ref[pl.ds(2 * p + j, 1), :]
        lg128 = lg[:, :LANES]
        dmat = jnp.exp(absd * lg)
        xi_f = jnp.exp((rowi + 1.0) * lg128)
        ze_f = jnp.exp((RET_C - 1.0 - rowi) * lg128)
        xi_b = jnp.exp((RET_C - rowi) * lg128)
        ze_b = jnp.exp(rowi * lg128)
        cdec = jnp.exp(RET_C * lg128)
        vs = slice(j * LANES, (j + 1) * LANES)
        fst = jnp.zeros((LANES, LANES), F32)
        for i in range(nc):
            sl = slice(i * RET_C, (i + 1) * RET_C)
            qh = jnp.where(hmask, qr_s[sl, :], 0.0)
            kh = kr_s[sl, :]
            vh = v_ref[sl, vs]
            sc = _mm(qh, kh, _NT) * dmat
            y = _mm(sc, vh)
            if i > 0:
                y = y + _mm(qh * xi_f, fst)
            if i < nc - 1:
                fst = fst * cdec + _mm(jnp.where(hmask, kh, 0.0) * ze_f, vh, _TN)
            y_s[sl, vs] = y
        gst = jnp.zeros((LANES, LANES), F32)
        for i in range(nc - 1, -1, -1):
            sl = slice(i * RET_C, (i + 1) * RET_C)
            y = y_s[sl, vs]
            if i < nc - 1:
                qh = jnp.where(hmask, qr_s[sl, :], 0.0)
                y = y + _mm(qh * xi_b, gst)
            if i > 0:
                kh = jnp.where(hmask, kr_s[sl, :], 0.0)
                gst = gst * cdec + _mm(kh * ze_b, v_ref[sl, vs], _TN)
            mu = jnp.mean(y, axis=-1, keepdims=True)
            dv = y - mu
            var = jnp.mean(dv * dv, axis=-1, keepdims=True)
            out = dv * lax.rsqrt(var + NORM_EPS) * gain_ref[:, vs]
            gb = gb_ref[sl, vs].astype(F32)
            o_ref[sl, vs] = (out * (gb * _sigmoid(gb))).astype(BF16)


def _ret_call(proj3, cos_t, sin_t, lg_t, gain):
    bsz, seq, _ = proj3.shape
    w2 = 2 * LANES

    def spec(base, width):
        blk = base // width
        return pl.BlockSpec((None, seq, width), lambda b, p, blk=blk: (b, 0, blk + p))

    return pl.pallas_call(
        functools.partial(_ret_kernel, seq=seq),
        grid=(bsz, RET_HEADS // 2),
        in_specs=[spec(C_QB, LANES), spec(C_KB, LANES), spec(C_VB, w2), spec(C_GB, w2),
                  pl.BlockSpec((seq, LANES), lambda b, p: (0, 0)),
                  pl.BlockSpec((seq, LANES), lambda b, p: (0, 0)),
                  pl.BlockSpec((RET_HEADS, w2), lambda b, p: (0, 0)),
                  pl.BlockSpec((1, w2), lambda b, p: (0, p))],
        out_specs=pl.BlockSpec((None, seq, w2), lambda b, p: (b, 0, p)),
        out_shape=jax.ShapeDtypeStruct((bsz, seq, D), BF16),
        scratch_shapes=[pltpu.VMEM((seq, LANES), F32), pltpu.VMEM((seq, LANES), F32), pltpu.VMEM((seq, w2), F32)],
        compiler_params=pltpu.CompilerParams(dimension_semantics=("arbitrary", "arbitrary"),
                                             vmem_limit_bytes=VMEM_LIMIT),
        name="retention",
    )(proj3, proj3, proj3, proj3, cos_t, sin_t, lg_t, gain)


def _merge_kernel(x_ref, ya_ref, yb_ref, ma_ref, mb_ref, wa_ref, wb_ref, wo_ref, g_ref, *o_refs, last):
    za = jnp.dot(ya_ref[...], wa_ref[...], preferred_element_type=F32)
    zb = jnp.dot(yb_ref[...], wb_ref[...], preferred_element_type=F32)
    merged = jax.nn.sigmoid(ma_ref[...].astype(F32)) * za + jax.nn.sigmoid(mb_ref[...].astype(F32)) * zb
    x = x_ref[...] + jnp.dot(merged.astype(BF16), wo_ref[...], preferred_element_type=F32)
    if last:
        o_refs[0][...] = _rms(x, g_ref[...])
    else:
        o_refs[0][...] = x
        o_refs[1][...] = _rms(x, g_ref[...]).astype(BF16)


def _merge_call(x2, ya2, yb2, proj2, wa, wb, wo, gain, *, last):
    t = x2.shape[0]
    tm = min(512, t)
    row = pl.BlockSpec((tm, D), lambda i: (i, 0))
    wspec = pl.BlockSpec((D, D), lambda i: (0, 0))
    x_sds = jax.ShapeDtypeStruct((t, D), F32)
    return pl.pallas_call(
        functools.partial(_merge_kernel, last=last),
        grid=(t // tm,),
        in_specs=[row, row, row,
                  pl.BlockSpec((tm, D), lambda i: (i, C_MA // D)),
                  pl.BlockSpec((tm, D), lambda i: (i, C_MB // D)),
                  wspec, wspec, wspec, pl.BlockSpec((1, D), lambda i: (0, 0))],
        out_specs=row if last else (row, row),
        out_shape=x_sds if last else (x_sds, jax.ShapeDtypeStruct((t, D), BF16)),
        compiler_params=pltpu.CompilerParams(dimension_semantics=("arbitrary",),
                                             vmem_limit_bytes=VMEM_LIMIT),
        name="merge",
    )(x2, ya2, yb2, proj2, proj2, wa, wb, wo, gain)


def _norm_kernel(x_ref, g_ref, o_ref):
    o_ref[...] = _rms(x_ref[...], g_ref[...]).astype(BF16)


def _norm_call(x2, gain):
    t = x2.shape[0]
    tm = min(1024, t)
    return pl.pallas_call(
        _norm_kernel,
        grid=(t // tm,),
        in_specs=[pl.BlockSpec((tm, D), lambda i: (i, 0)), pl.BlockSpec((1, D), lambda i: (0, 0))],
        out_specs=pl.BlockSpec((tm, D), lambda i: (i, 0)),
        out_shape=jax.ShapeDtypeStruct((t, D), BF16),
        compiler_params=pltpu.CompilerParams(dimension_semantics=("arbitrary",)),
        name="first_norm",
    )(x2, gain)


def _qk_perm():
    idx = []
    for pair in range(RET_HEADS // 2):
        for lane in range(LANES):
            seg, i = lane // 32, lane % 32
            head, half = 2 * pair + seg % 2, seg // 2
            idx.append(head * HEAD + half * 32 + i)
    return np.asarray(idx, np.int32)


def _pack_w_in(w_in, w_vres_down):
    depth = w_in.shape[0]
    perm = _qk_perm()
    vres = jnp.concatenate([jnp.zeros((1, D, 32), F32), w_vres_down], axis=0)
    vres = jnp.pad(vres, ((0, 0), (0, 0), (0, LANES - 32)))
    w_wkv = jnp.concatenate([w_in[:, :, 0:3072], w_in[:, :, 3328:4352], w_in[:, :, 3072:3328], vres], axis=2)
    w_rest = jnp.concatenate([w_in[:, :, 5376:6400], w_in[:, :, 6400:7424], w_in[:, :, 7424:8448],
                              w_in[:, :, 8448:9472], w_in[:, :, 4352:4864][:, :, perm],
                              w_in[:, :, 4864:5376][:, :, perm]], axis=2)
    assert w_wkv.shape == (depth, D, N_WKV) and w_rest.shape == (depth, D, N_REST)
    return w_wkv.astype(BF16), w_rest.astype(BF16)


def _pair_cat(wf, wb):
    depth, pairs = wf.shape[0], D // LANES
    cat = jnp.concatenate([wf.reshape(depth, LANES, pairs, LANES), wb.reshape(depth, LANES, pairs, LANES)], axis=3)
    return cat.transpose(0, 2, 1, 3)


def _wkv_params(shift_prev, shift_next, w_decay_up, decay_bias, w_iclr_up, iclr_bias, w_vres_up, vres_bias,
                k_k, k_a, r_k, lnx_gain, lnx_bias):
    depth = k_k.shape[0]
    vb = jnp.concatenate([jnp.zeros((1, D), F32), vres_bias], axis=0)
    vecs = jnp.stack([shift_prev[:, 0:1024], shift_prev[:, 1024:2048], shift_prev[:, 2048:3072],
                      shift_next[:, 0:1024], shift_next[:, 1024:2048], shift_next[:, 2048:3072],
                      0.5 * decay_bias[:, 0], 0.5 * decay_bias[:, 1], 0.5 * iclr_bias[:, 0], 0.5 * iclr_bias[:, 1],
                      k_k, k_a, r_k.reshape(depth, D), lnx_gain, lnx_bias, vb], axis=1)
    mul = jnp.concatenate([jnp.stack([shift_prev[:, 3072:3200], shift_prev[:, 3200:3328],
                                      shift_next[:, 3072:3200], shift_next[:, 3200:3328]], axis=1),
                           jnp.zeros((depth, 4, LANES), F32)], axis=1)
    zpad = jnp.zeros((depth, 64, D), F32)
    zmat = jnp.zeros((depth, LANES, D), F32)
    wvr = jnp.concatenate([jnp.zeros((1, LANES, D), F32),
                           jnp.pad(w_vres_up, ((0, 0), (0, LANES - 32), (0, 0)))], axis=0)
    wl = jnp.stack([0.5 * _pair_cat(jnp.concatenate([w_decay_up[:, 0], zpad], axis=1),
                                    jnp.concatenate([zpad, w_decay_up[:, 1]], axis=1)),
                    0.5 * _pair_cat(jnp.concatenate([w_iclr_up[:, 0], zpad], axis=1),
                                    jnp.concatenate([zpad, w_iclr_up[:, 1]], axis=1)),
                    _pair_cat(wvr, zmat)], axis=2)
    return vecs, mul, wl


def _rope_tables(seq):
    half = 32
    freqs = jnp.power(ROPE_BASE, -jnp.arange(half, dtype=F32) / half)
    ang = jnp.arange(seq, dtype=F32)[:, None] * freqs[None, :]
    cos = jnp.tile(jnp.cos(ang), (1, 4))
    sin = jnp.tile(jnp.sin(ang), (1, 4))
    sign = jnp.where(jnp.arange(LANES) < 64, -1.0, 1.0).astype(F32)
    return cos, sin * sign[None, :]


def kernel(x, norm_gain, w_in, w_vres_down, shift_prev, shift_next, w_decay_up, decay_bias, w_iclr_up, iclr_bias,
           w_vres_up, vres_bias, k_k, k_a, r_k, lnx_gain, lnx_bias, w_branch_a, ret_norm_gain, w_branch_b, w_out,
           final_gain):
    bsz, seq, _ = x.shape
    depth = w_in.shape[0]
    assert seq % RET_C == 0 and seq % SLAB == 0 and SLAB % CH == 0
    t = bsz * seq

    w_wkv, w_rest = _pack_w_in(w_in, w_vres_down)
    wa_all, wb_all, wo_all = (w.astype(BF16) for w in (w_branch_a, w_branch_b, w_out))
    cos_t, sin_t = _rope_tables(seq)
    log_g = jnp.log1p(-jnp.exp2(-5.0 - jnp.arange(RET_HEADS, dtype=F32)))
    lg_t = jnp.broadcast_to(log_g[:, None], (RET_HEADS, 2 * LANES))
    vecs_all, mul_all, wl_all = _wkv_params(shift_prev, shift_next, w_decay_up, decay_bias, w_iclr_up, iclr_bias,
                                            w_vres_up, vres_bias, k_k, k_a, r_k, lnx_gain, lnx_bias)

    x2 = x.reshape(t, D)
    xn2 = _norm_call(x2, norm_gain[0][None, :])
    vfirst = None
    for l in range(depth):
        proj3 = _proj_call(xn2, w_wkv[l], F32).reshape(bsz, seq, N_WKV)
        rest2 = _proj_call(xn2, w_rest[l], BF16)
        rest3 = rest2.reshape(bsz, seq, N_REST)

        vecs, mul, wl = vecs_all[l], mul_all[l], wl_all[l]

        if l == 0:
            ya, vfirst = _wkv_call(proj3, None, vecs, mul, wl, first=True)
        else:
            ya = _wkv_call(proj3, vfirst, vecs, mul, wl, first=False)
        yb = _ret_call(rest3, cos_t, sin_t, lg_t, ret_norm_gain[l][None, :])
        last = l == depth - 1
        gain = final_gain if last else norm_gain[l + 1]
        res = _merge_call(x2, ya.reshape(t, D), yb.reshape(t, D), rest2, wa_all[l], wb_all[l], wo_all[l],
                          gain[None, :], last=last)
        if not last:
            x2, xn2 = res

    return res.reshape(bsz, seq, D)
```
